```python
import functools
import jax, jax.numpy as jnp
from jax import lax
import numpy as np

D_MODEL = 1024
BATCH = 16
SEQ = 2048
DEPTH = 1
DEC_BATCH = 8
DEC_SEQ = 16
PAST_LEN = 1024

CHUNK = 64
HEAD_DIM = 64
N_HEADS_A = 8
N_HEADS_B = 8
WIDTH_A = N_HEADS_A * HEAD_DIM
WIDTH_B = N_HEADS_B * HEAD_DIM
MIX_WIDTH = WIDTH_A + WIDTH_B
Q_BLOCK = 128
LEFT_CHUNKS = 8
LEFT_REACH = LEFT_CHUNKS * CHUNK
BAND = LEFT_REACH + CHUNK
REL_CLIP = 128
D_FF = -(-(8 * D_MODEL) // (3 * 256)) * 256
D_IN = 3 * WIDTH_A + N_HEADS_A + 3 * WIDTH_B
EPS = 1e-6
SCALE = HEAD_DIM ** -0.5

kernel_name = "hymba_fox_chunkband_stream_encoder"


def rmsnorm(x, g):
    xf = x.astype(jnp.float32)
    y = xf * lax.rsqrt(jnp.mean(xf * xf, axis=-1, keepdims=True) + EPS)
    return (y * g.astype(jnp.float32)).astype(x.dtype)


def project_in(h, w_in, b_forget):
    z = h @ w_in
    splits = [WIDTH_A, 2 * WIDTH_A, 3 * WIDTH_A, 3 * WIDTH_A + N_HEADS_A,
              3 * WIDTH_A + N_HEADS_A + WIDTH_B, 3 * WIDTH_A + N_HEADS_A + 2 * WIDTH_B]
    qa, ka, va, fa, qb, kb, vb = jnp.split(z, splits, axis=-1)
    heads = lambda t, n: t.reshape(t.shape[:-1] + (n, HEAD_DIM))
    logf = jax.nn.log_sigmoid((fa + b_forget).astype(jnp.float32))
    return (heads(qa, N_HEADS_A), heads(ka, N_HEADS_A), heads(va, N_HEADS_A), logf,
            heads(qb, N_HEADS_B), heads(kb, N_HEADS_B), heads(vb, N_HEADS_B))


def attend(q, k, v, bias, valid):
    s = jnp.einsum('bqhd,bkhd->bhqk', q, k).astype(jnp.float32) * SCALE + bias
    s = jnp.where(valid, s, -jnp.inf)
    p = jax.nn.softmax(s, axis=-1)
    return jnp.einsum('bhqk,bkhd->bqhd', p.astype(v.dtype), v)


def rel_bias_matrix(rel_bias, dist):
    idx = jnp.clip(dist, -REL_CLIP, REL_CLIP) + REL_CLIP
    return rel_bias[:, idx].astype(jnp.float32)


def fox_prompt(q, k, v, logf):
    b, s = q.shape[0], q.shape[1]
    nb = s // Q_BLOCK
    c = jnp.cumsum(logf, axis=1).transpose(0, 2, 1)
    q_blocks = q.reshape(b, nb, Q_BLOCK, N_HEADS_A, HEAD_DIM).transpose(1, 0, 2, 3, 4)
    c_blocks = c.reshape(b, N_HEADS_A, nb, Q_BLOCK).transpose(2, 0, 1, 3)
    spos = jnp.arange(s)

    def one_block(args):
        qb_, cq, blk = args
        tpos = blk * Q_BLOCK + jnp.arange(Q_BLOCK)
        bias = cq[..., :, None] - c[..., None, :]
        valid = spos[None, :] <= tpos[:, None]
        return attend(qb_, k, v, bias, valid)

    out = lax.map(one_block, (q_blocks, c_blocks, jnp.arange(nb)))
    return out.transpose(1, 0, 2, 3, 4).reshape(b, s, N_HEADS_A, HEAD_DIM)


def fox_sample(q, k, v, logf, cache_k, cache_v, cache_logf):
    p_len, t_len = cache_k.shape[1], q.shape[1]
    k_all = jnp.concatenate([cache_k, k], axis=1)
    v_all = jnp.concatenate([cache_v, v], axis=1)
    c = jnp.cumsum(jnp.concatenate([cache_logf.astype(jnp.float32), logf], axis=1), axis=1)
    c = c.transpose(0, 2, 1)
    bias = c[..., p_len:, None] - c[..., None, :]
    valid = jnp.arange(p_len + t_len)[None, :] <= (p_len + jnp.arange(t_len))[:, None]
    return attend(q, k_all, v_all, bias, valid)


def band_prompt(q, k, v, rel_bias):
    b, s = q.shape[0], q.shape[1]
    nc = s // CHUNK
    pad = ((0, 0), (LEFT_REACH, 0), (0, 0), (0, 0))
    kp, vp = jnp.pad(k, pad), jnp.pad(v, pad)
    i = jnp.arange(CHUNK)
    j = jnp.arange(BAND)
    bias = rel_bias_matrix(rel_bias, i[:, None] + LEFT_REACH - j[None, :])

    def one_chunk(ci):
        start = ci * CHUNK
        qc = lax.dynamic_slice_in_dim(q, start, CHUNK, axis=1)
        kband = lax.dynamic_slice_in_dim(kp, start, BAND, axis=1)
        vband = lax.dynamic_slice_in_dim(vp, start, BAND, axis=1)
        valid = (start + j - LEFT_REACH) >= 0
        return attend(qc, kband, vband, bias, valid[None, :])

    out = lax.map(one_chunk, jnp.arange(nc))
    return out.transpose(1, 0, 2, 3, 4).reshape(b, s, N_HEADS_B, HEAD_DIM)


def band_sample(q, k, v, rel_bias, cache_k, cache_v):
    l_len, t_len = cache_k.shape[1], q.shape[1]
    k_all = jnp.concatenate([cache_k, k], axis=1)
    v_all = jnp.concatenate([cache_v, v], axis=1)
    dist = jnp.arange(t_len)[:, None] + l_len - jnp.arange(l_len + t_len)[None, :]
    bias = rel_bias_matrix(rel_bias, dist)
    return attend(q, k_all, v_all, bias, True)


def trunk_layer(x, fox_fn, band_fn, norm_mix_pre, w_in, b_forget, gain_out_a, gain_out_b,
                w_out, norm_mix_post, norm_ffn_pre, w_gate, w_up, w_down, norm_ffn_post):
    h = rmsnorm(x, norm_mix_pre)
    qa, ka, va, logf, qb, kb, vb = project_in(h, w_in, b_forget)
    oa = fox_fn(qa, ka, va, logf)
    ob = band_fn(qb, kb, vb)
    oa = rmsnorm(oa.reshape(oa.shape[:-2] + (WIDTH_A,)), gain_out_a)
    ob = rmsnorm(ob.reshape(ob.shape[:-2] + (WIDTH_B,)), gain_out_b)
    mix = jnp.concatenate([oa, ob], axis=-1) @ w_out
    x = x + rmsnorm(mix, norm_mix_post)
    hf = rmsnorm(x, norm_ffn_pre)
    ff = (jax.nn.silu(hf @ w_gate) * (hf @ w_up)) @ w_down
    x = x + rmsnorm(ff, norm_ffn_post)
    return x, (ka, va, logf, kb, vb)


def setup_inputs(seed: int = 0) -> dict:
    key = jax.random.key(seed)
    ks = jax.random.split(key, 20)
    nrm = jax.random.normal
    f32 = jnp.float32
    b_len = min(LEFT_REACH, PAST_LEN)
    gain = lambda k, n: 1.0 + 0.05 * nrm(k, (DEPTH, n), f32)
    return {
        "x_prompt": nrm(ks[0], (BATCH, SEQ, D_MODEL), f32),
        "x_sample": nrm(ks[1], (DEC_BATCH, DEC_SEQ, D_MODEL), f32),
        "cache_a_k": nrm(ks[2], (DEPTH, DEC_BATCH, PAST_LEN, N_HEADS_A, HEAD_DIM), f32),
        "cache_a_v": nrm(ks[3], (DEPTH, DEC_BATCH, PAST_LEN, N_HEADS_A, HEAD_DIM), f32),
        "cache_a_logf": jax.nn.log_sigmoid(4.0 + nrm(ks[4], (DEPTH, DEC_BATCH, PAST_LEN, N_HEADS_A), f32)),
        "cache_b_k": nrm(ks[5], (DEPTH, DEC_BATCH, b_len, N_HEADS_B, HEAD_DIM), f32),
        "cache_b_v": nrm(ks[6], (DEPTH, DEC_BATCH, b_len, N_HEADS_B, HEAD_DIM), f32),
        "norm_mix_pre": gain(ks[7], D_MODEL),
        "w_in": nrm(ks[8], (DEPTH, D_MODEL, D_IN), f32) * D_MODEL ** -0.5,
        "b_forget": 4.0 + 0.5 * nrm(ks[9], (DEPTH, N_HEADS_A), f32),
        "rel_bias": 0.1 * nrm(ks[10], (DEPTH, N_HEADS_B, 2 * REL_CLIP + 1), f32),
        "gain_out_a": gain(ks[11], WIDTH_A),
        "gain_out_b": gain(ks[12], WIDTH_B),
        "w_out": nrm(ks[13], (DEPTH, MIX_WIDTH, D_MODEL), f32) * MIX_WIDTH ** -0.5,
        "norm_mix_post": gain(ks[14], D_MODEL),
        "norm_ffn_pre": gain(ks[15], D_MODEL),
        "w_gate": nrm(ks[16], (DEPTH, D_MODEL, D_FF), f32) * D_MODEL ** -0.5,
        "w_up": nrm(ks[17], (DEPTH, D_MODEL, D_FF), f32) * D_MODEL ** -0.5,
        "w_down": nrm(ks[18], (DEPTH, D_FF, D_MODEL), f32) * D_FF ** -0.5,
        "norm_ffn_post": gain(ks[19], D_MODEL),
    }


def reference(x_prompt, x_sample, cache_a_k, cache_a_v, cache_a_logf, cache_b_k, cache_b_v,
              norm_mix_pre, w_in, b_forget, rel_bias, gain_out_a, gain_out_b, w_out,
              norm_mix_post, norm_ffn_pre, w_gate, w_up, w_down, norm_ffn_post):
    xp, xs = x_prompt, x_sample
    keep = min(LEFT_REACH, x_prompt.shape[1])
    pa_k, pa_v, pa_f, pb_k, pb_v = [], [], [], [], []
    sa_k, sa_v, sa_f, sb_k, sb_v = [], [], [], [], []
    for l in range(DEPTH):
        weights = (norm_mix_pre[l], w_in[l], b_forget[l], gain_out_a[l], gain_out_b[l], w_out[l],
                   norm_mix_post[l], norm_ffn_pre[l], w_gate[l], w_up[l], w_down[l], norm_ffn_post[l])
        xp, (ka, va, lf, kb, vb) = trunk_layer(
            xp, fox_prompt, functools.partial(band_prompt, rel_bias=rel_bias[l]), *weights)
        pa_k.append(ka); pa_v.append(va); pa_f.append(lf)
        pb_k.append(kb[:, kb.shape[1] - keep:]); pb_v.append(vb[:, vb.shape[1] - keep:])
        xs, (ka, va, lf, kb, vb) = trunk_layer(
            xs,
            functools.partial(fox_sample, cache_k=cache_a_k[l], cache_v=cache_a_v[l], cache_logf=cache_a_logf[l]),
            functools.partial(band_sample, rel_bias=rel_bias[l], cache_k=cache_b_k[l], cache_v=cache_b_v[l]),
            *weights)
        sa_k.append(ka); sa_v.append(va); sa_f.append(lf); sb_k.append(kb); sb_v.append(vb)
    st = jnp.stack
    return (xp, xs, st(pa_k), st(pa_v), st(pa_f), st(pb_k), st(pb_v),
            st(sa_k), st(sa_v), st(sa_f), st(sb_k), st(sb_v))
```

```python
import functools

import jax
import jax.numpy as jnp
from jax import lax
from jax.experimental import pallas as pl
from jax.experimental.pallas import tpu as pltpu

F32 = jnp.float32
BF16 = jnp.bfloat16

HEAD_DIM = 64
N_HEADS = 8
WIDTH = N_HEADS * HEAD_DIM
PAIR = 2 * HEAD_DIM
N_PAIRS = N_HEADS // 2
CHUNK = 64
LEFT_CHUNKS = 8
LEFT_REACH = LEFT_CHUNKS * CHUNK
REL_CLIP = 128
EPS = 1e-6
SCALE = HEAD_DIM ** -0.5
Q_TILE = 256
BAND_KEYS = LEFT_REACH + Q_TILE
TOEPLITZ = 1024
MASKED = -1e30
NEW_PAD = 128
VMEM_LIMIT = 56 * 1024 * 1024


def _params():
    return pltpu.CompilerParams(vmem_limit_bytes=VMEM_LIMIT)


def _resident(shape):
    return pl.BlockSpec(shape, lambda *_: (0,) * len(shape), pipeline_mode=pl.Buffered(1))


def _rms(x, g):
    ms = jnp.mean(x * x, axis=-1, keepdims=True)
    return x * lax.rsqrt(ms + EPS) * g


def _log_sigmoid(x):
    return jnp.minimum(x, 0.0) - jnp.log1p(jnp.exp(-jnp.abs(x)))


def _dot(a, b):
    return jnp.dot(a, b, preferred_element_type=F32)


def _dot_nt(a, b):
    return lax.dot_general(a, b, (((1,), (1,)), ((), ())), preferred_element_type=F32)


def _inproj_kernel(x_ref, g_ref, w_ref, bf_ref,
                   qa_ref, ka_ref, va_ref, qb_ref, kb_ref, vb_ref,
                   ka32_ref, va32_ref, logf_ref, kb32_ref, vb32_ref, *, tiles_per_seq):
    h = _rms(x_ref[...], g_ref[...]).astype(BF16)

    def proj(c):
        return _dot(h, w_ref[:, c * WIDTH:(c + 1) * WIDTH])

    qa_ref[...] = (proj(0) * SCALE).astype(BF16)
    z = proj(1)
    ka32_ref[...] = z
    ka_ref[...] = z.astype(BF16)
    z = proj(2)
    va32_ref[...] = z
    va_ref[...] = z.astype(BF16)
    qb_ref[...] = (proj(3) * SCALE).astype(BF16)
    zk = proj(4)
    kb_ref[...] = zk.astype(BF16)
    zv = proj(5)
    vb_ref[...] = zv.astype(BF16)

    @pl.when(pl.program_id(0) % tiles_per_seq == tiles_per_seq - 1)
    def _():
        kb32_ref[...] = zk
        vb32_ref[...] = zv

    f = _dot(h, w_ref[:, 6 * WIDTH:6 * WIDTH + PAIR])
    ft = f.T
    logf_ref[...] = _log_sigmoid(ft[0:N_HEADS, :] + bf_ref[...])


def _inproj(x, gain, w_cat, b_col, *, tm, tiles_per_seq):
    n_tok, d_model = x.shape
    grid = n_tok // tm
    bf = lambda: pl.BlockSpec((tm, WIDTH), lambda i: (i, 0))
    keep = pl.BlockSpec((tm, WIDTH), lambda i: (i // tiles_per_seq, 0))
    n_keep = (grid // tiles_per_seq) * tm
    out_shape = (
        [jax.ShapeDtypeStruct((n_tok, WIDTH), BF16)] * 6
        + [jax.ShapeDtypeStruct((n_tok, WIDTH), F32)] * 2
        + [jax.ShapeDtypeStruct((N_HEADS, n_tok), F32)]
        + [jax.ShapeDtypeStruct((n_keep, WIDTH), F32)] * 2
    )
    out_specs = [bf() for _ in range(8)] + [pl.BlockSpec((N_HEADS, tm), lambda i: (0, i)), keep, keep]
    return pl.pallas_call(
        functools.partial(_inproj_kernel, tiles_per_seq=tiles_per_seq),
        grid=(grid,),
        in_specs=[
            pl.BlockSpec((tm, d_model), lambda i: (i, 0)),
            _resident((1, d_model)),
            _resident(w_cat.shape),
            _resident((N_HEADS, 1)),
        ],
        out_specs=out_specs,
        out_shape=out_shape,
        compiler_params=_params(),
        name="inproj",
    )(x, gain, w_cat, b_col)


def _lane_cumsum(x):
    n = x.shape[-1]
    lane = lax.broadcasted_iota(jnp.int32, x.shape, x.ndim - 1)
    shift = 1
    while shift < n:
        x = x + jnp.where(lane >= shift, pltpu.roll(x, shift, axis=x.ndim - 1), 0.0)
        shift *= 2
    return x


def _cumsum_kernel(lf_ref, crow_ref, ccol_ref):
    c = _lane_cumsum(lf_ref[...])
    crow_ref[...] = c
    seq = c.shape[1]
    ct = jnp.concatenate([c, jnp.zeros((PAIR - N_HEADS, seq), F32)], axis=0).T
    for p in range(N_PAIRS):
        ccol_ref[p] = ct[:, 2 * p:2 * p + 2]


def _cumsum(logf_t, *, batch, seq):
    n_tok = batch * seq
    return pl.pallas_call(
        _cumsum_kernel,
        grid=(batch,),
        in_specs=[pl.BlockSpec((N_HEADS, seq), lambda b: (0, b))],
        out_specs=[
            pl.BlockSpec((N_HEADS, seq), lambda b: (0, b)),
            pl.BlockSpec((N_PAIRS, seq, 2), lambda b: (0, b, 0)),
        ],
        out_shape=[
            jax.ShapeDtypeStruct((N_HEADS, n_tok), F32),
            jax.ShapeDtypeStruct((N_PAIRS, n_tok, 2), F32),
        ],
        compiler_params=_params(),
        name="cumsum",
    )(logf_t)


def _split_heads_q(q):
    lane = lax.broadcasted_iota(jnp.int32, q.shape, 1)
    zero = jnp.zeros_like(q)
    return jnp.concatenate([jnp.where(lane < HEAD_DIM, q, zero),
                            jnp.where(lane >= HEAD_DIM, q, zero)], axis=0)


def _augment_v(v, h):
    lane = lax.broadcasted_iota(jnp.int32, v.shape, 1)
    own = (lane < HEAD_DIM) if h == 0 else (lane >= HEAD_DIM)
    return jnp.where(own, v, jnp.ones_like(v))


def _merge_heads_out(acc0, acc1):
    lane = lax.broadcasted_iota(jnp.int32, acc0.shape, 1)
    first = lane < HEAD_DIM
    num = jnp.where(first, acc0, acc1)
    den = pltpu.roll(jnp.where(first, acc1, acc0), HEAD_DIM, axis=1)
    return num / den


def _fox_kernel(q_ref, k_ref, v_ref, crow_ref, ccol_ref, o_ref, v0_ref, v1_ref, *, seq):
    v = v_ref[...]
    v0_ref[...] = _augment_v(v, 0)
    v1_ref[...] = _augment_v(v, 1)
    vaug = (v0_ref, v1_ref)
    row = lax.broadcasted_iota(jnp.int32, (Q_TILE, Q_TILE), 0)
    col = lax.broadcasted_iota(jnp.int32, (Q_TILE, Q_TILE), 1)
    causal = col <= row

    for qb in range(seq // Q_TILE):
        qs, ke = qb * Q_TILE, (qb + 1) * Q_TILE
        s = _dot_nt(_split_heads_q(q_ref[qs:ke, :]), k_ref[0:ke, :])
        accs = []
        for h in range(2):
            ck = crow_ref[0, h:h + 1, 0:ke]
            cq = ccol_ref[0, qs:ke, h:h + 1]
            t = s[h * Q_TILE:(h + 1) * Q_TILE, :] - ck
            t_diag = jnp.where(causal, t[:, qs:ke], -jnp.inf)
            m = jnp.max(t_diag, axis=1, keepdims=True)
            if qb > 0:
                t_past = t[:, 0:qs]
                m = jnp.maximum(m, jnp.max(t_past, axis=1, keepdims=True))
            shift = cq - (m + cq)
            acc = _dot(jnp.exp(t_diag + shift).astype(BF16), vaug[h][qs:ke, :])
            if qb > 0:
                acc = acc + _dot(jnp.exp(t_past + shift).astype(BF16), vaug[h][0:qs, :])
            accs.append(acc)
        o_ref[qs:ke, :] = _merge_heads_out(accs[0], accs[1]).astype(o_ref.dtype)


def _fox(q, k, v, crow, ccol, *, batch, seq):
    n_tok = batch * seq
    slab = lambda: pl.BlockSpec((seq, PAIR), lambda b, p: (b, p))
    return pl.pallas_call(
        functools.partial(_fox_kernel, seq=seq),
        grid=(batch, N_PAIRS),
        in_specs=[
            slab(), slab(), slab(),
            pl.BlockSpec((1, 2, seq), lambda b, p: (p, 0, b)),
            pl.BlockSpec((1, seq, 2), lambda b, p: (p, b, 0)),
        ],
        out_specs=slab(),
        out_shape=jax.ShapeDtypeStruct((n_tok, WIDTH), BF16),
        scratch_shapes=[pltpu.VMEM((seq, PAIR), BF16), pltpu.VMEM((seq, PAIR), BF16)],
        compiler_params=_params(),
        name="fox",
    )(q, k, v, crow, ccol)


def _bias_kernel(rb_ref, tabm_ref, tabr_ref):
    rb = rb_ref[...]
    n_rows = rb.shape[1]
    r = lax.broadcasted_iota(jnp.int32, (n_rows, TOEPLITZ), 0)
    m = lax.broadcasted_iota(jnp.int32, (n_rows, TOEPLITZ), 1)
    e = jnp.where(m < TOEPLITZ // 2, m, m - TOEPLITZ)
    idx = jnp.clip(LEFT_REACH - Q_TILE - e, -REL_CLIP, REL_CLIP) + REL_CLIP
    onehot = jnp.where(r == idx, 1.0, 0.0).astype(BF16)
    hi = rb.astype(BF16)
    rest = rb - hi.astype(F32)
    mid = rest.astype(BF16)
    lo = (rest - mid.astype(F32)).astype(BF16)
    u = _dot(hi, onehot) + _dot(mid, onehot) + _dot(lo, onehot)

    i = lax.broadcasted_iota(jnp.int32, (Q_TILE, BAND_KEYS), 0)
    j = lax.broadcasted_iota(jnp.int32, (Q_TILE, BAND_KEYS), 1)
    qc = jnp.right_shift(i, 6)
    kc = jnp.right_shift(j, 6)
    valid = (kc >= qc) & (kc <= qc + LEFT_CHUNKS)
    n_new = tabr_ref.shape[1]
    for h in range(N_HEADS):
        base = jnp.broadcast_to(u[h:h + 1, :], (Q_TILE, TOEPLITZ))
        t = pltpu.roll(base, Q_TILE, axis=1, stride=1, stride_axis=0)[:, 0:BAND_KEYS]
        tabm_ref[h] = jnp.where(valid, t, MASKED)
        tabr_ref[h] = t[0:n_new, :]


def _bias_tables(rb_pad, *, n_new):
    return pl.pallas_call(
        _bias_kernel,
        out_shape=[
            jax.ShapeDtypeStruct((N_HEADS, Q_TILE, BAND_KEYS), F32),
            jax.ShapeDtypeStruct((N_HEADS, n_new, BAND_KEYS), F32),
        ],
        compiler_params=_params(),
        name="bias_tables",
    )(rb_pad)


def _band_kernel(q_ref, k_ref, v_ref, tab_ref, o_ref, v0_ref, v1_ref, *, seq):
    v = v_ref[...]
    v0_ref[...] = _augment_v(v, 0)
    v1_ref[...] = _augment_v(v, 1)
    vaug = (v0_ref, v1_ref)

    for qb in range(seq // Q_TILE):
        qs, ke = qb * Q_TILE, (qb + 1) * Q_TILE
        ks = max(0, qs - LEFT_REACH)
        s = _dot_nt(_split_heads_q(q_ref[qs:ke, :]), k_ref[ks:ke, :])
        accs = []
        for h in range(2):
            t = s[h * Q_TILE:(h + 1) * Q_TILE, :] + tab_ref[h, :, BAND_KEYS - (ke - ks):BAND_KEYS]
            m = jnp.max(t, axis=1, keepdims=True)
            accs.append(_dot(jnp.exp(t - m).astype(BF16), vaug[h][ks:ke, :]))
        o_ref[qs:ke, :] = _merge_heads_out(accs[0], accs[1]).astype(o_ref.dtype)


def _band(q, k, v, tabm, *, batch, seq):
    n_tok = batch * seq
    slab = lambda: pl.BlockSpec((seq, PAIR), lambda p, b: (b, p))
    return pl.pallas_call(
        functools.partial(_band_kernel, seq=seq),
        grid=(N_PAIRS, batch),
        in_specs=[
            slab(), slab(), slab(),
            pl.BlockSpec((2, Q_TILE, BAND_KEYS), lambda p, b: (p, 0, 0)),
        ],
        out_specs=slab(),
        out_shape=jax.ShapeDtypeStruct((n_tok, WIDTH), BF16),
        scratch_shapes=[pltpu.VMEM((seq, PAIR), BF16), pltpu.VMEM((seq, PAIR), BF16)],
        compiler_params=_params(),
        name="band",
    )(q, k, v, tabm)


def _two_part_attention(q2, k_cache, v_cache, kn_ref, vn_ref, bias_cache, bias_new, n_q):
    s_c = _dot_nt(q2, k_cache)
    s_n = _dot_nt(q2, kn_ref[...])
    accs = []
    for h in range(2):
        rows = slice(h * n_q, (h + 1) * n_q)
        t_c = s_c[rows, :] + bias_cache(h)
        t_n = s_n[rows, :] + bias_new(h)
        m = jnp.maximum(jnp.max(t_c, axis=1, keepdims=True), jnp.max(t_n, axis=1, keepdims=True))
        acc = _dot(jnp.exp(t_c - m).astype(BF16), _augment_v(v_cache, h))
        acc = acc + _dot(jnp.exp(t_n - m).astype(BF16), _augment_v(vn_ref[...], h))
        accs.append(acc)
    return _merge_heads_out(accs[0], accs[1])


def _sample_attn_kernel(qa_ref, ka_ref, va_ref, qb_ref, kb_ref, vb_ref,
                        cak_ref, cav_ref, cbk_ref, cbv_ref, clf_ref, nlf_ref, tab_ref,
                        oa_ref, ob_ref, kn_ref, vn_ref, *, n_q):
    row = lax.broadcasted_iota(jnp.int32, (n_q, NEW_PAD), 0)
    col = lax.broadcasted_iota(jnp.int32, (n_q, NEW_PAD), 1)

    def stage_new(k_new, v_new):
        kn_ref[...] = jnp.zeros_like(kn_ref)
        vn_ref[...] = jnp.zeros_like(vn_ref)
        kn_ref[0:n_q, :] = k_new
        vn_ref[0:n_q, :] = v_new

    c_cache = _lane_cumsum(clf_ref[0, 0])
    n_cache = c_cache.shape[1]
    c_new = _lane_cumsum(nlf_ref[0, 0]) + c_cache[:, n_cache - 1:n_cache]

    def fox_cache(h):
        cq = jnp.sum(jnp.where(col == row, c_new[h:h + 1, :], 0.0), axis=1, keepdims=True)
        return cq - c_cache[h:h + 1, :]

    def fox_new(h):
        cq = jnp.sum(jnp.where(col == row, c_new[h:h + 1, :], 0.0), axis=1, keepdims=True)
        return jnp.where(col <= row, cq - c_new[h:h + 1, :], -jnp.inf)

    stage_new(ka_ref[...], va_ref[...])
    oa = _two_part_attention(_split_heads_q(qa_ref[...]), cak_ref[...].astype(BF16), cav_ref[...].astype(BF16),
                             kn_ref, vn_ref, fox_cache, fox_new, n_q)
    oa_ref[...] = oa.astype(oa_ref.dtype)

    n_band = cbk_ref.shape[0]

    def band_cache(h):
        return tab_ref[h, :, 0:n_band]

    def band_new(h):
        return jnp.where(col < n_q, tab_ref[h, :, n_band:n_band + NEW_PAD], -jnp.inf)

    stage_new(kb_ref[...], vb_ref[...])
    ob = _two_part_attention(_split_heads_q(qb_ref[...]), cbk_ref[...].astype(BF16), cbv_ref[...].astype(BF16),
                             kn_ref, vn_ref, band_cache, band_new, n_q)
    ob_ref[...] = ob.astype(ob_ref.dtype)


def _sample_attn(qa, ka, va, qb, kb, vb, cak, cav, cbk, cbv, clf, nlf, tabr, *, batch, n_q):
    n_tok = batch * n_q
    past = cak.shape[0] // batch
    n_band = cbk.shape[0] // batch
    new = lambda: pl.BlockSpec((n_q, PAIR), lambda b, p: (b, p))
    return pl.pallas_call(
        functools.partial(_sample_attn_kernel, n_q=n_q),
        grid=(batch, N_PAIRS),
        in_specs=[
            new(), new(), new(), new(), new(), new(),
            pl.BlockSpec((past, PAIR), lambda b, p: (b, p)),
            pl.BlockSpec((past, PAIR), lambda b, p: (b, p)),
            pl.BlockSpec((n_band, PAIR), lambda b, p: (b, p)),
            pl.BlockSpec((n_band, PAIR), lambda b, p: (b, p)),
            pl.BlockSpec((1, 1, 2, past), lambda b, p: (b, p, 0, 0)),
            pl.BlockSpec((1, 1, 2, NEW_PAD), lambda b, p: (b, p, 0, 0)),
            pl.BlockSpec((2, n_q, BAND_KEYS), lambda b, p: (p, 0, 0)),
        ],
        out_specs=[new(), new()],
        out_shape=[jax.ShapeDtypeStruct((n_tok, WIDTH), BF16)] * 2,
        scratch_shapes=[pltpu.VMEM((NEW_PAD, PAIR), BF16), pltpu.VMEM((NEW_PAD, PAIR), BF16)],
        compiler_params=_params(),
        name="sample_attn",
    )(qa, ka, va, qb, kb, vb, cak, cav, cbk, cbv, clf, nlf, tabr)


def _post_kernel(x_ref, oa_ref, ob_ref, ga_ref, gb_ref, wo_ref, gpost_ref, gpre_ref,
                 wg_ref, wu_ref, wd_ref, gffn_ref, y_ref):
    na = _rms(oa_ref[...].astype(F32), ga_ref[...]).astype(BF16)
    nb = _rms(ob_ref[...].astype(F32), gb_ref[...]).astype(BF16)
    mix = _dot(na, wo_ref[0:WIDTH, :]) + _dot(nb, wo_ref[WIDTH:2 * WIDTH, :])
    x1 = x_ref[...] + _rms(mix, gpost_ref[...])
    hf = _rms(x1, gpre_ref[...]).astype(BF16)
    g = _dot(hf, wg_ref[...])
    u = _dot(hf, wu_ref[...])
    a = (g * jax.nn.sigmoid(g) * u).astype(BF16)
    ff = _dot(a, wd_ref[...])
    y_ref[...] = x1 + _rms(ff, gffn_ref[...])


def _post(x, oa, ob, ga, gb, wo, gpost, gpre, wg, wu, wd, gffn, *, tm):
    n_tok, d_model = x.shape
    row = lambda w: pl.BlockSpec((tm, w), lambda i: (i, 0))
    return pl.pallas_call(
        _post_kernel,
        grid=(n_tok // tm,),
        in_specs=[
            row(d_model), row(WIDTH), row(WIDTH),
            _resident(ga.shape), _resident(gb.shape), _resident(wo.shape),
            _resident(gpost.shape), _resident(gpre.shape),
            _resident(wg.shape), _resident(wu.shape), _resident(wd.shape), _resident(gffn.shape),
        ],
        out_specs=row(d_model),
        out_shape=jax.ShapeDtypeStruct((n_tok, d_model), F32),
        compiler_params=_params(),
        name="post",
    )(x, oa, ob, ga, gb, wo, gpost, gpre, wg, wu, wd, gffn)


def _row(v):
    return v.reshape(1, -1).astype(F32)


def kernel(x_prompt, x_sample, cache_a_k, cache_a_v, cache_a_logf, cache_b_k, cache_b_v, norm_mix_pre, w_in, b_forget, rel_bias, gain_out_a, gain_out_b, w_out, norm_mix_post, norm_ffn_pre, w_gate, w_up, w_down, norm_ffn_post):
    batch, seq, d_model = x_prompt.shape
    dec_batch, dec_seq, _ = x_sample.shape
    depth = w_in.shape[0]
    past = cache_a_k.shape[2]
    n_band = cache_b_k.shape[2]
    keep = min(LEFT_REACH, seq)
    assert seq % Q_TILE == 0 and keep == LEFT_REACH and n_band == LEFT_REACH and dec_seq <= CHUNK
    n_tok_p, n_tok_s = batch * seq, dec_batch * dec_seq
    tm_p = keep

    xp = x_prompt.reshape(n_tok_p, d_model)
    xs = x_sample.reshape(n_tok_s, d_model)
    outs = [[] for _ in range(10)]
    fa0, fa1 = 3 * WIDTH, 3 * WIDTH + N_HEADS
    for l in range(depth):
        wl = w_in[l]
        w_cat = jnp.concatenate(
            [wl[:, :fa0], wl[:, fa1:], jnp.pad(wl[:, fa0:fa1], ((0, 0), (0, PAIR - N_HEADS)))], axis=1).astype(BF16)
        b_col = b_forget[l].reshape(N_HEADS, 1).astype(F32)
        rb_pad = jnp.pad(rel_bias[l].astype(F32), ((0, 16 - N_HEADS), (0, 384 - (2 * REL_CLIP + 1))))
        tail = (_row(gain_out_a[l]), _row(gain_out_b[l]), w_out[l].astype(BF16), _row(norm_mix_post[l]),
                _row(norm_ffn_pre[l]), w_gate[l].astype(BF16), w_up[l].astype(BF16), w_down[l].astype(BF16),
                _row(norm_ffn_post[l]))
        tabm, tabr = _bias_tables(rb_pad, n_new=dec_seq)

        qa, ka, va, qb, kb, vb, ka32, va32, logf_t, kb32, vb32 = _inproj(
            xp, _row(norm_mix_pre[l]), w_cat, b_col, tm=tm_p, tiles_per_seq=seq // tm_p)
        crow, ccol = _cumsum(logf_t, batch=batch, seq=seq)
        oa = _fox(qa, ka, va, crow.reshape(N_PAIRS, 2, n_tok_p), ccol, batch=batch, seq=seq)
        ob = _band(qb, kb, vb, tabm, batch=batch, seq=seq)
        xp = _post(xp, oa, ob, *tail, tm=256)
        outs[0].append(ka32.reshape(batch, seq, N_HEADS, HEAD_DIM))
        outs[1].append(va32.reshape(batch, seq, N_HEADS, HEAD_DIM))
        outs[2].append(logf_t.T.reshape(batch, seq, N_HEADS))
        outs[3].append(kb32.reshape(batch, keep, N_HEADS, HEAD_DIM))
        outs[4].append(vb32.reshape(batch, keep, N_HEADS, HEAD_DIM))

        qa, ka, va, qb, kb, vb, ka32, va32, logf_t, kb32, vb32 = _inproj(
            xs, _row(norm_mix_pre[l]), w_cat, b_col, tm=n_tok_s, tiles_per_seq=1)
        clf = cache_a_logf[l].astype(F32).transpose(0, 2, 1).reshape(dec_batch, N_PAIRS, 2, past)
        nlf = logf_t.reshape(N_PAIRS, 2, dec_batch, dec_seq).transpose(2, 0, 1, 3)
        nlf = jnp.pad(nlf, ((0, 0), (0, 0), (0, 0), (0, NEW_PAD - dec_seq)))
        oa, ob = _sample_attn(
            qa, ka, va, qb, kb, vb,
            cache_a_k[l].reshape(dec_batch * past, WIDTH), cache_a_v[l].reshape(dec_batch * past, WIDTH),
            cache_b_k[l].reshape(dec_batch * n_band, WIDTH), cache_b_v[l].reshape(dec_batch * n_band, WIDTH),
            clf, nlf, tabr, batch=dec_batch, n_q=dec_seq)
        xs = _post(xs, oa, ob, *tail, tm=n_tok_s)
        outs[5].append(ka32.reshape(dec_batch, dec_seq, N_HEADS, HEAD_DIM))
        outs[6].append(va32.reshape(dec_batch, dec_seq, N_HEADS, HEAD_DIM))
        outs[7].append(logf_t.T.reshape(dec_batch, dec_seq, N_HEADS))
        outs[8].append(kb32.reshape(dec_batch, dec_seq, N_HEADS, HEAD_DIM))
        outs[9].append(vb32.reshape(dec_batch, dec_seq, N_HEADS, HEAD_DIM))

    st = jnp.stack
    return (xp.reshape(batch, seq, d_model), xs.reshape(dec_batch, dec_seq, d_model),
            *[st(o) for o in outs])
```

```python
import functools

import jax
import jax.numpy as jnp
from jax import lax
from jax.experimental import pallas as pl
from jax.experimental.pallas import tpu as pltpu

F32 = jnp.float32
BF16 = jnp.bfloat16

HEAD_DIM = 64
N_HEADS = 8
WIDTH = N_HEADS * HEAD_DIM
PAIR = 2 * HEAD_DIM
N_PAIRS = N_HEADS // 2
CHUNK = 64
LEFT_CHUNKS = 8
LEFT_REACH = LEFT_CHUNKS * CHUNK
REL_CLIP = 128
EPS = 1e-6
SCALE = HEAD_DIM ** -0.5
Q_TILE = 256
BAND_KEYS = LEFT_REACH + Q_TILE
TOEPLITZ = 1024
MASKED = -1e30
NEW_PAD = 128
VMEM_LIMIT = 56 * 1024 * 1024


def _params():
    return pltpu.CompilerParams(vmem_limit_bytes=VMEM_LIMIT)


def _resident(shape):
    return pl.BlockSpec(shape, lambda *_: (0,) * len(shape), pipeline_mode=pl.Buffered(1))


def _rms(x, g):
    ms = jnp.mean(x * x, axis=-1, keepdims=True)
    return x * lax.rsqrt(ms + EPS) * g


def _log_sigmoid(x):
    return jnp.minimum(x, 0.0) - jnp.log1p(jnp.exp(-jnp.abs(x)))


def _dot(a, b):
    return jnp.dot(a, b, preferred_element_type=F32)


def _dot_nt(a, b):
    return lax.dot_general(a, b, (((1,), (1,)), ((), ())), preferred_element_type=F32)


def _inproj_kernel(x_ref, g_ref, w_ref, bf_ref,
                   qa_ref, ka_ref, va_ref, qb_ref, kb_ref, vb_ref,
                   ka32_ref, va32_ref, logf_ref, kb32_ref, vb32_ref):
    h = _rms(x_ref[...], g_ref[...]).astype(BF16)

    def proj(c):
        return _dot(h, w_ref[:, c * WIDTH:(c + 1) * WIDTH])

    qa_ref[...] = (proj(0) * SCALE).astype(BF16)
    z = proj(1)
    ka32_ref[...] = z
    ka_ref[...] = z.astype(BF16)
    z = proj(2)
    va32_ref[...] = z
    va_ref[...] = z.astype(BF16)
    qb_ref[...] = (proj(3) * SCALE).astype(BF16)
    z = proj(4)
    kb32_ref[...] = z
    kb_ref[...] = z.astype(BF16)
    z = proj(5)
    vb32_ref[...] = z
    vb_ref[...] = z.astype(BF16)
    f = _dot(h, w_ref[:, 6 * WIDTH:6 * WIDTH + PAIR])
    logf_ref[...] = _log_sigmoid(f.T[0:N_HEADS, :] + bf_ref[...])


def _inproj(x, gain, w_cat, b_col):
    n_tok, d_model = x.shape
    full = lambda w: pl.BlockSpec((n_tok, w), lambda i: (0, 0))
    out_shape = (
        [jax.ShapeDtypeStruct((n_tok, WIDTH), BF16)] * 6
        + [jax.ShapeDtypeStruct((n_tok, WIDTH), F32)] * 2
        + [jax.ShapeDtypeStruct((N_HEADS, n_tok), F32)]
        + [jax.ShapeDtypeStruct((n_tok, WIDTH), F32)] * 2
    )
    out_specs = ([full(WIDTH) for _ in range(8)]
                 + [pl.BlockSpec((N_HEADS, n_tok), lambda i: (0, 0)), full(WIDTH), full(WIDTH)])
    return pl.pallas_call(
        _inproj_kernel,
        grid=(1,),
        in_specs=[full(d_model), _resident((1, d_model)), _resident(w_cat.shape), _resident((N_HEADS, 1))],
        out_specs=out_specs,
        out_shape=out_shape,
        compiler_params=_params(),
        name="inproj_sample",
    )(x, gain, w_cat, b_col)


def _inproj_t_kernel(x_ref, g_ref, wq_ref, wt_ref, bf_ref,
                     qa_ref, qb_ref, kat_ref, vat_ref, kbt_ref, vbt_ref,
                     kat32_ref, vat32_ref, logf_ref, kbt32_ref, vbt32_ref, *, tiles_per_seq):
    h = _rms(x_ref[...], g_ref[...]).astype(BF16)
    zq = _dot(h, wq_ref[...])
    qa_ref[...] = (zq[:, 0:WIDTH] * SCALE).astype(BF16)
    qb_ref[...] = (zq[:, WIDTH:2 * WIDTH] * SCALE).astype(BF16)

    def proj_t(c, rows=WIDTH):
        return _dot_nt(wt_ref[c * WIDTH:c * WIDTH + rows, :], h)

    z = proj_t(0)
    kat32_ref[0] = z
    kat_ref[0] = z.astype(BF16)
    z = proj_t(1)
    vat32_ref[0] = z
    vat_ref[0] = z.astype(BF16)
    zk = proj_t(2)
    kbt_ref[0] = zk.astype(BF16)
    zv = proj_t(3, WIDTH + 2 * N_HEADS)
    vbt_ref[0] = zv[0:WIDTH].astype(BF16)
    logf_ref[...] = _log_sigmoid(zv[WIDTH:WIDTH + N_HEADS] + bf_ref[...])

    @pl.when(pl.program_id(0) % tiles_per_seq == tiles_per_seq - 1)
    def _():
        kbt32_ref[0] = zk
        vbt32_ref[0] = zv[0:WIDTH]


def _inproj_t(x, gain, wq, wt, b_col, *, batch, seq, tm):
    n_tok, d_model = x.shape
    tps = seq // tm
    tok = lambda: pl.BlockSpec((tm, WIDTH), lambda i: (i, 0))
    feat = lambda: pl.BlockSpec((1, WIDTH, tm), lambda i: (i // tps, 0, i % tps))
    keep = lambda: pl.BlockSpec((1, WIDTH, tm), lambda i: (i // tps, 0, 0))
    out_shape = (
        [jax.ShapeDtypeStruct((n_tok, WIDTH), BF16)] * 2
        + [jax.ShapeDtypeStruct((batch, WIDTH, seq), BF16)] * 4
        + [jax.ShapeDtypeStruct((batch, WIDTH, seq), F32)] * 2
        + [jax.ShapeDtypeStruct((N_HEADS, n_tok), F32)]
        + [jax.ShapeDtypeStruct((batch, WIDTH, tm), F32)] * 2
    )
    out_specs = ([tok(), tok()] + [feat() for _ in range(6)]
                 + [pl.BlockSpec((N_HEADS, tm), lambda i: (0, i)), keep(), keep()])
    return pl.pallas_call(
        functools.partial(_inproj_t_kernel, tiles_per_seq=tps),
        grid=(n_tok // tm,),
        in_specs=[
            pl.BlockSpec((tm, d_model), lambda i: (i, 0)),
            _resident((1, d_model)),
            _resident(wq.shape),
            _resident(wt.shape),
            _resident((N_HEADS, 1)),
        ],
        out_specs=out_specs,
        out_shape=out_shape,
        compiler_params=_params(),
        name="inproj",
    )(x, gain, wq, wt, b_col)


def _lane_cumsum(x):
    n = x.shape[-1]
    lane = lax.broadcasted_iota(jnp.int32, x.shape, x.ndim - 1)
    shift = 1
    while shift < n:
        x = x + jnp.where(lane >= shift, pltpu.roll(x, shift, axis=x.ndim - 1), 0.0)
        shift *= 2
    return x


def _cumsum_kernel(lf_ref, crow_ref, ccol_ref):
    c = _lane_cumsum(lf_ref[...])
    crow_ref[...] = c
    seq = c.shape[1]
    ct = jnp.concatenate([c, jnp.zeros((PAIR - N_HEADS, seq), F32)], axis=0).T
    for p in range(N_PAIRS):
        ccol_ref[p] = ct[:, 2 * p:2 * p + 2]


def _cumsum(logf_t, *, batch, seq):
    n_tok = batch * seq
    return pl.pallas_call(
        _cumsum_kernel,
        grid=(batch,),
        in_specs=[pl.BlockSpec((N_HEADS, seq), lambda b: (0, b))],
        out_specs=[
            pl.BlockSpec((N_HEADS, seq), lambda b: (0, b)),
            pl.BlockSpec((N_PAIRS, seq, 2), lambda b: (0, b, 0)),
        ],
        out_shape=[
            jax.ShapeDtypeStruct((N_HEADS, n_tok), F32),
            jax.ShapeDtypeStruct((N_PAIRS, n_tok, 2), F32),
        ],
        compiler_params=_params(),
        name="cumsum",
    )(logf_t)


def _split_heads_q(q):
    lane = lax.broadcasted_iota(jnp.int32, q.shape, 1)
    zero = jnp.zeros_like(q)
    return jnp.concatenate([jnp.where(lane < HEAD_DIM, q, zero),
                            jnp.where(lane >= HEAD_DIM, q, zero)], axis=0)


def _augment_v(v, h, axis):
    idx = lax.broadcasted_iota(jnp.int32, v.shape, axis)
    own = (idx < HEAD_DIM) if h == 0 else (idx >= HEAD_DIM)
    return jnp.where(own, v, jnp.ones_like(v))


def _merge_heads_out(acc0, acc1):
    lane = lax.broadcasted_iota(jnp.int32, acc0.shape, 1)
    first = lane < HEAD_DIM
    num = jnp.where(first, acc0, acc1)
    den = pltpu.roll(jnp.where(first, acc1, acc0), HEAD_DIM, axis=1)
    return num / den


def _fox_kernel(q_ref, kt_ref, vt_ref, crow_ref, ccol_ref, o_ref, v0_ref, v1_ref, *, seq):
    vt = vt_ref[0]
    v0_ref[...] = _augment_v(vt, 0, 0)
    v1_ref[...] = _augment_v(vt, 1, 0)
    vaug = (v0_ref, v1_ref)
    row = lax.broadcasted_iota(jnp.int32, (Q_TILE, Q_TILE), 0)
    col = lax.broadcasted_iota(jnp.int32, (Q_TILE, Q_TILE), 1)
    causal = col <= row

    for qb in range(seq // Q_TILE):
        qs, ke = qb * Q_TILE, (qb + 1) * Q_TILE
        s = _dot(_split_heads_q(q_ref[qs:ke, :]), kt_ref[0, :, 0:ke])
        accs = []
        for h in range(2):
            ck = crow_ref[0, h:h + 1, 0:ke]
            cq = ccol_ref[0, qs:ke, h:h + 1]
            t = s[h * Q_TILE:(h + 1) * Q_TILE, :] - ck
            t_diag = jnp.where(causal, t[:, qs:ke], -jnp.inf)
            m = jnp.max(t_diag, axis=1, keepdims=True)
            if qb > 0:
                t_past = t[:, 0:qs]
                m = jnp.maximum(m, jnp.max(t_past, axis=1, keepdims=True))
            shift = cq - (m + cq)
            acc = _dot_nt(jnp.exp(t_diag + shift).astype(BF16), vaug[h][:, qs:ke])
            if qb > 0:
                acc = acc + _dot_nt(jnp.exp(t_past + shift).astype(BF16), vaug[h][:, 0:qs])
            accs.append(acc)
        o_ref[qs:ke, :] = _merge_heads_out(accs[0], accs[1]).astype(o_ref.dtype)


def _fox(q, kt, vt, crow, ccol, *, batch, seq):
    n_tok = batch * seq
    tok = lambda: pl.BlockSpec((seq, PAIR), lambda b, p: (b, p))
    feat = lambda: pl.BlockSpec((1, PAIR, seq), lambda b, p: (b, p, 0))
    return pl.pallas_call(
        functools.partial(_fox_kernel, seq=seq),
        grid=(batch, N_PAIRS),
        in_specs=[
            tok(), feat(), feat(),
            pl.BlockSpec((1, 2, seq), lambda b, p: (p, 0, b)),
            pl.BlockSpec((1, seq, 2), lambda b, p: (p, b, 0)),
        ],
        out_specs=tok(),
        out_shape=jax.ShapeDtypeStruct((n_tok, WIDTH), BF16),
        scratch_shapes=[pltpu.VMEM((PAIR, seq), BF16), pltpu.VMEM((PAIR, seq), BF16)],
        compiler_params=_params(),
        name="fox",
    )(q, kt, vt, crow, ccol)


def _bias_kernel(rb_ref, tabm_ref, tabr_ref):
    rb = rb_ref[...]
    n_rows = rb.shape[1]
    r = lax.broadcasted_iota(jnp.int32, (n_rows, TOEPLITZ), 0)
    m = lax.broadcasted_iota(jnp.int32, (n_rows, TOEPLITZ), 1)
    e = jnp.where(m < TOEPLITZ // 2, m, m - TOEPLITZ)
    idx = jnp.clip(LEFT_REACH - Q_TILE - e, -REL_CLIP, REL_CLIP) + REL_CLIP
    onehot = jnp.where(r == idx, 1.0, 0.0).astype(BF16)
    hi = rb.astype(BF16)
    rest = rb - hi.astype(F32)
    mid = rest.astype(BF16)
    lo = (rest - mid.astype(F32)).astype(BF16)
    u = _dot(hi, onehot) + _dot(mid, onehot) + _dot(lo, onehot)

    i = lax.broadcasted_iota(jnp.int32, (Q_TILE, BAND_KEYS), 0)
    j = lax.broadcasted_iota(jnp.int32, (Q_TILE, BAND_KEYS), 1)
    qc = jnp.right_shift(i, 6)
    kc = jnp.right_shift(j, 6)
    valid = (kc >= qc) & (kc <= qc + LEFT_CHUNKS)
    n_new = tabr_ref.shape[1]
    for h in range(N_HEADS):
        base = jnp.broadcast_to(u[h:h + 1, :], (Q_TILE, TOEPLITZ))
        t = pltpu.roll(base, Q_TILE, axis=1, stride=1, stride_axis=0)[:, 0:BAND_KEYS]
        tabm_ref[h] = jnp.where(valid, t, MASKED)
        tabr_ref[h] = t[0:n_new, :]


def _bias_tables(rb_pad, *, n_new):
    return pl.pallas_call(
        _bias_kernel,
        out_shape=[
            jax.ShapeDtypeStruct((N_HEADS, Q_TILE, BAND_KEYS), F32),
            jax.ShapeDtypeStruct((N_HEADS, n_new, BAND_KEYS), F32),
        ],
        compiler_params=_params(),
        name="bias_tables",
    )(rb_pad)


def _band_kernel(q_ref, kt_ref, vt_ref, tab_ref, o_ref, v0_ref, v1_ref, *, seq):
    vt = vt_ref[0]
    v0_ref[...] = _augment_v(vt, 0, 0)
    v1_ref[...] = _augment_v(vt, 1, 0)
    vaug = (v0_ref, v1_ref)

    for qb in range(seq // Q_TILE):
        qs, ke = qb * Q_TILE, (qb + 1) * Q_TILE
        ks = max(0, qs - LEFT_REACH)
        s = _dot(_split_heads_q(q_ref[qs:ke, :]), kt_ref[0, :, ks:ke])
        accs = []
        for h in range(2):
            t = s[h * Q_TILE:(h + 1) * Q_TILE, :] + tab_ref[h, :, BAND_KEYS - (ke - ks):BAND_KEYS]
            m = jnp.max(t, axis=1, keepdims=True)
            accs.append(_dot_nt(jnp.exp(t - m).astype(BF16), vaug[h][:, ks:ke]))
        o_ref[qs:ke, :] = _merge_heads_out(accs[0], accs[1]).astype(o_ref.dtype)


def _band(q, kt, vt, tabm, *, batch, seq):
    n_tok = batch * seq
    tok = lambda: pl.BlockSpec((seq, PAIR), lambda p, b: (b, p))
    feat = lambda: pl.BlockSpec((1, PAIR, seq), lambda p, b: (b, p, 0))
    return pl.pallas_call(
        functools.partial(_band_kernel, seq=seq),
        grid=(N_PAIRS, batch),
        in_specs=[
            tok(), feat(), feat(),
            pl.BlockSpec((2, Q_TILE, BAND_KEYS), lambda p, b: (p, 0, 0)),
        ],
        out_specs=tok(),
        out_shape=jax.ShapeDtypeStruct((n_tok, WIDTH), BF16),
        scratch_shapes=[pltpu.VMEM((PAIR, seq), BF16), pltpu.VMEM((PAIR, seq), BF16)],
        compiler_params=_params(),
        name="band",
    )(q, kt, vt, tabm)


def _two_part_attention(q2, kt_cache, vt_cache, kn_ref, vn_ref, bias_cache, bias_new, n_q):
    s_c = _dot(q2, kt_cache)
    s_n = _dot_nt(q2, kn_ref[...])
    accs = []
    for h in range(2):
        rows = slice(h * n_q, (h + 1) * n_q)
        t_c = s_c[rows, :] + bias_cache(h)
        t_n = s_n[rows, :] + bias_new(h)
        m = jnp.maximum(jnp.max(t_c, axis=1, keepdims=True), jnp.max(t_n, axis=1, keepdims=True))
        acc = _dot_nt(jnp.exp(t_c - m).astype(BF16), _augment_v(vt_cache, h, 0))
        acc = acc + _dot(jnp.exp(t_n - m).astype(BF16), _augment_v(vn_ref[...], h, 1))
        accs.append(acc)
    return _merge_heads_out(accs[0], accs[1])


def _sample_attn_kernel(qa_ref, ka_ref, va_ref, qb_ref, kb_ref, vb_ref,
                        cak_ref, cav_ref, cbk_ref, cbv_ref, clf_ref, nlf_ref, tab_ref,
                        oa_ref, ob_ref, kn_ref, vn_ref, *, n_q):
    row = lax.broadcasted_iota(jnp.int32, (n_q, NEW_PAD), 0)
    col = lax.broadcasted_iota(jnp.int32, (n_q, NEW_PAD), 1)

    def stage_new(k_new, v_new):
        kn_ref[...] = jnp.zeros_like(kn_ref)
        vn_ref[...] = jnp.zeros_like(vn_ref)
        kn_ref[0:n_q, :] = k_new
        vn_ref[0:n_q, :] = v_new

    c_cache = _lane_cumsum(clf_ref[0, 0])
    n_cache = c_cache.shape[1]
    c_new = _lane_cumsum(nlf_ref[0, 0]) + c_cache[:, n_cache - 1:n_cache]

    def fox_cache(h):
        cq = jnp.sum(jnp.where(col == row, c_new[h:h + 1, :], 0.0), axis=1, keepdims=True)
        return cq - c_cache[h:h + 1, :]

    def fox_new(h):
        cq = jnp.sum(jnp.where(col == row, c_new[h:h + 1, :], 0.0), axis=1, keepdims=True)
        return jnp.where(col <= row, cq - c_new[h:h + 1, :], -jnp.inf)

    stage_new(ka_ref[...], va_ref[...])
    oa = _two_part_attention(_split_heads_q(qa_ref[...]), cak_ref[0].astype(BF16), cav_ref[0].astype(BF16),
                             kn_ref, vn_ref, fox_cache, fox_new, n_q)
    oa_ref[...] = oa.astype(oa_ref.dtype)

    n_band = cbk_ref.shape[2]

    def band_cache(h):
        return tab_ref[h, :, 0:n_band]

    def band_new(h):
        return jnp.where(col < n_q, tab_ref[h, :, n_band:n_band + NEW_PAD], -jnp.inf)

    stage_new(kb_ref[...], vb_ref[...])
    ob = _two_part_attention(_split_heads_q(qb_ref[...]), cbk_ref[0].astype(BF16), cbv_ref[0].astype(BF16),
                             kn_ref, vn_ref, band_cache, band_new, n_q)
    ob_ref[...] = ob.astype(ob_ref.dtype)


def _sample_attn(qa, ka, va, qb, kb, vb, cakt, cavt, cbkt, cbvt, clf, nlf, tabr, *, batch, n_q):
    n_tok = batch * n_q
    past = cakt.shape[2]
    n_band = cbkt.shape[2]
    new = lambda: pl.BlockSpec((n_q, PAIR), lambda b, p: (b, p))
    cache = lambda n: pl.BlockSpec((1, PAIR, n), lambda b, p: (b, p, 0))
    return pl.pallas_call(
        functools.partial(_sample_attn_kernel, n_q=n_q),
        grid=(batch, N_PAIRS),
        in_specs=[
            new(), new(), new(), new(), new(), new(),
            cache(past), cache(past), cache(n_band), cache(n_band),
            pl.BlockSpec((1, 1, 2, past), lambda b, p: (b, p, 0, 0)),
            pl.BlockSpec((1, 1, 2, NEW_PAD), lambda b, p: (b, p, 0, 0)),
            pl.BlockSpec((2, n_q, BAND_KEYS), lambda b, p: (p, 0, 0)),
        ],
        out_specs=[new(), new()],
        out_shape=[jax.ShapeDtypeStruct((n_tok, WIDTH), BF16)] * 2,
        scratch_shapes=[pltpu.VMEM((NEW_PAD, PAIR), BF16), pltpu.VMEM((NEW_PAD, PAIR), BF16)],
        compiler_params=_params(),
        name="sample_attn",
    )(qa, ka, va, qb, kb, vb, cakt, cavt, cbkt, cbvt, clf, nlf, tabr)


def _post_kernel(x_ref, oa_ref, ob_ref, ga_ref, gb_ref, wo_ref, gpost_ref, gpre_ref,
                 wg_ref, wu_ref, wd_ref, gffn_ref, y_ref):
    na = _rms(oa_ref[...].astype(F32), ga_ref[...]).astype(BF16)
    nb = _rms(ob_ref[...].astype(F32), gb_ref[...]).astype(BF16)
    mix = _dot(na, wo_ref[0:WIDTH, :]) + _dot(nb, wo_ref[WIDTH:2 * WIDTH, :])
    x1 = x_ref[...] + _rms(mix, gpost_ref[...])
    hf = _rms(x1, gpre_ref[...]).astype(BF16)
    g = _dot(hf, wg_ref[...])
    u = _dot(hf, wu_ref[...])
    a = (g * jax.nn.sigmoid(g) * u).astype(BF16)
    ff = _dot(a, wd_ref[...])
    y_ref[...] = x1 + _rms(ff, gffn_ref[...])


def _post(x, oa, ob, ga, gb, wo, gpost, gpre, wg, wu, wd, gffn, *, tm):
    n_tok, d_model = x.shape
    row = lambda w: pl.BlockSpec((tm, w), lambda i: (i, 0))
    return pl.pallas_call(
        _post_kernel,
        grid=(n_tok // tm,),
        in_specs=[
            row(d_model), row(WIDTH), row(WIDTH),
            _resident(ga.shape), _resident(gb.shape), _resident(wo.shape),
            _resident(gpost.shape), _resident(gpre.shape),
            _resident(wg.shape), _resident(wu.shape), _resident(wd.shape), _resident(gffn.shape),
        ],
        out_specs=row(d_model),
        out_shape=jax.ShapeDtypeStruct((n_tok, d_model), F32),
        compiler_params=_params(),
        name="post",
    )(x, oa, ob, ga, gb, wo, gpost, gpre, wg, wu, wd, gffn)


def _row(v):
    return v.reshape(1, -1).astype(F32)


def _state_from_feature_major(t, batch, n):
    return t.reshape(batch, N_HEADS, HEAD_DIM, n).transpose(0, 3, 1, 2)


def _cache_feature_major(c):
    b, n = c.shape[0], c.shape[1]
    return c.transpose(0, 2, 3, 1).reshape(b, WIDTH, n)


def kernel(x_prompt, x_sample, cache_a_k, cache_a_v, cache_a_logf, cache_b_k, cache_b_v, norm_mix_pre, w_in, b_forget, rel_bias, gain_out_a, gain_out_b, w_out, norm_mix_post, norm_ffn_pre, w_gate, w_up, w_down, norm_ffn_post):
    batch, seq, d_model = x_prompt.shape
    dec_batch, dec_seq, _ = x_sample.shape
    depth = w_in.shape[0]
    past = cache_a_k.shape[2]
    n_band = cache_b_k.shape[2]
    keep = min(LEFT_REACH, seq)
    assert seq % Q_TILE == 0 and keep == LEFT_REACH and n_band == LEFT_REACH and dec_seq <= CHUNK
    n_tok_p, n_tok_s = batch * seq, dec_batch * dec_seq
    tm_p = keep

    xp = x_prompt.reshape(n_tok_p, d_model)
    xs = x_sample.reshape(n_tok_s, d_model)
    outs = [[] for _ in range(10)]
    fa0, fa1 = 3 * WIDTH, 3 * WIDTH + N_HEADS
    for l in range(depth):
        wl = w_in[l]
        qa_w, ka_w, va_w = wl[:, 0:WIDTH], wl[:, WIDTH:2 * WIDTH], wl[:, 2 * WIDTH:fa0]
        f_w = wl[:, fa0:fa1]
        qb_w, kb_w, vb_w = wl[:, fa1:fa1 + WIDTH], wl[:, fa1 + WIDTH:fa1 + 2 * WIDTH], wl[:, fa1 + 2 * WIDTH:]
        w_cat = jnp.concatenate(
            [qa_w, ka_w, va_w, qb_w, kb_w, vb_w, jnp.pad(f_w, ((0, 0), (0, PAIR - N_HEADS)))], axis=1).astype(BF16)
        w_q = jnp.concatenate([qa_w, qb_w], axis=1).astype(BF16)
        w_t = jnp.concatenate(
            [ka_w, va_w, kb_w, vb_w, jnp.pad(f_w, ((0, 0), (0, N_HEADS)))], axis=1).T.astype(BF16)
        b_col = b_forget[l].reshape(N_HEADS, 1).astype(F32)
        rb_pad = jnp.pad(rel_bias[l].astype(F32), ((0, 16 - N_HEADS), (0, 384 - (2 * REL_CLIP + 1))))
        tail = (_row(gain_out_a[l]), _row(gain_out_b[l]), w_out[l].astype(BF16), _row(norm_mix_post[l]),
                _row(norm_ffn_pre[l]), w_gate[l].astype(BF16), w_up[l].astype(BF16), w_down[l].astype(BF16),
                _row(norm_ffn_post[l]))
        tabm, tabr = _bias_tables(rb_pad, n_new=dec_seq)

        qa, qb, kat, vat, kbt, vbt, kat32, vat32, logf_t, kbt32, vbt32 = _inproj_t(
            xp, _row(norm_mix_pre[l]), w_q, w_t, b_col, batch=batch, seq=seq, tm=tm_p)
        crow, ccol = _cumsum(logf_t, batch=batch, seq=seq)
        oa = _fox(qa, kat, vat, crow.reshape(N_PAIRS, 2, n_tok_p), ccol, batch=batch, seq=seq)
        ob = _band(qb, kbt, vbt, tabm, batch=batch, seq=seq)
        xp = _post(xp, oa, ob, *tail, tm=256)
        outs[0].append(_state_from_feature_major(kat32, batch, seq))
        outs[1].append(_state_from_feature_major(vat32, batch, seq))
        outs[2].append(logf_t.T.reshape(batch, seq, N_HEADS))
        outs[3].append(_state_from_feature_major(kbt32, batch, keep))
        outs[4].append(_state_from_feature_major(vbt32, batch, keep))

        qa, ka, va, qb, kb, vb, ka32, va32, logf_t, kb32, vb32 = _inproj(xs, _row(norm_mix_pre[l]), w_cat, b_col)
        clf = cache_a_logf[l].astype(F32).transpose(0, 2, 1).reshape(dec_batch, N_PAIRS, 2, past)
        nlf = logf_t.reshape(N_PAIRS, 2, dec_batch, dec_seq).transpose(2, 0, 1, 3)
        nlf = jnp.pad(nlf, ((0, 0), (0, 0), (0, 0), (0, NEW_PAD - dec_seq)))
        oa, ob = _sample_attn(
            qa, ka, va, qb, kb, vb,
            _cache_feature_major(cache_a_k[l]), _cache_feature_major(cache_a_v[l]),
            _cache_feature_major(cache_b_k[l]), _cache_feature_major(cache_b_v[l]),
            clf, nlf, tabr, batch=dec_batch, n_q=dec_seq)
        xs = _post(xs, oa, ob, *tail, tm=n_tok_s)
        outs[5].append(ka32.reshape(dec_batch, dec_seq, N_HEADS, HEAD_DIM))
        outs[6].append(va32.reshape(dec_batch, dec_seq, N_HEADS, HEAD_DIM))
        outs[7].append(logf_t.T.reshape(dec_batch, dec_seq, N_HEADS))
        outs[8].append(kb32.reshape(dec_batch, dec_seq, N_HEADS, HEAD_DIM))
        outs[9].append(vb32.reshape(dec_batch, dec_seq, N_HEADS, HEAD_DIM))

    st = jnp.stack
    return (xp.reshape(batch, seq, d_model), xs.reshape(dec_batch, dec_seq, d_model),
            *[st(o) for o in outs])
```

```python
import functools

import jax
import jax.numpy as jnp
from jax import lax
from jax.experimental import pallas as pl
from jax.experimental.pallas import tpu as pltpu

F32 = jnp.float32
BF16 = jnp.bfloat16

HEAD_DIM = 64
N_HEADS = 8
WIDTH = N_HEADS * HEAD_DIM
PAIR = 2 * HEAD_DIM
N_PAIRS = N_HEADS // 2
CHUNK = 64
LEFT_CHUNKS = 8
LEFT_REACH = LEFT_CHUNKS * CHUNK
REL_CLIP = 128
EPS = 1e-6
SCALE = HEAD_DIM ** -0.5
Q_TILE = 256
BAND_KEYS = LEFT_REACH + Q_TILE
TOEPLITZ = 1024
MASKED = -1e30
NEW_PAD = 128
VMEM_LIMIT = 56 * 1024 * 1024


def _params():
    return pltpu.CompilerParams(vmem_limit_bytes=VMEM_LIMIT)


def _resident(shape):
    return pl.BlockSpec(shape, lambda *_: (0,) * len(shape), pipeline_mode=pl.Buffered(1))


def _rms(x, g):
    ms = jnp.mean(x * x, axis=-1, keepdims=True)
    return x * lax.rsqrt(ms + EPS) * g


def _log_sigmoid(x):
    return jnp.minimum(x, 0.0) - jnp.log1p(jnp.exp(-jnp.abs(x)))


def _dot(a, b):
    return jnp.dot(a, b, preferred_element_type=F32)


def _dot_nt(a, b):
    return lax.dot_general(a, b, (((1,), (1,)), ((), ())), preferred_element_type=F32)


def _inproj_kernel(x_ref, g_ref, w_ref, bf_ref,
                   qa_ref, ka_ref, va_ref, qb_ref, kb_ref, vb_ref,
                   ka32_ref, va32_ref, logf_ref, kb32_ref, vb32_ref):
    h = _rms(x_ref[...], g_ref[...]).astype(BF16)

    def proj(c):
        return _dot(h, w_ref[:, c * WIDTH:(c + 1) * WIDTH])

    qa_ref[...] = (proj(0) * SCALE).astype(BF16)
    z = proj(1)
    ka32_ref[...] = z
    ka_ref[...] = z.astype(BF16)
    z = proj(2)
    va32_ref[...] = z
    va_ref[...] = z.astype(BF16)
    qb_ref[...] = (proj(3) * SCALE).astype(BF16)
    z = proj(4)
    kb32_ref[...] = z
    kb_ref[...] = z.astype(BF16)
    z = proj(5)
    vb32_ref[...] = z
    vb_ref[...] = z.astype(BF16)
    f = _dot(h, w_ref[:, 6 * WIDTH:6 * WIDTH + PAIR])
    logf_ref[...] = _log_sigmoid(f.T[0:N_HEADS, :] + bf_ref[...])


def _inproj(x, gain, w_cat, b_col):
    n_tok, d_model = x.shape
    full = lambda w: pl.BlockSpec((n_tok, w), lambda i: (0, 0))
    out_shape = (
        [jax.ShapeDtypeStruct((n_tok, WIDTH), BF16)] * 6
        + [jax.ShapeDtypeStruct((n_tok, WIDTH), F32)] * 2
        + [jax.ShapeDtypeStruct((N_HEADS, n_tok), F32)]
        + [jax.ShapeDtypeStruct((n_tok, WIDTH), F32)] * 2
    )
    out_specs = ([full(WIDTH) for _ in range(8)]
                 + [pl.BlockSpec((N_HEADS, n_tok), lambda i: (0, 0)), full(WIDTH), full(WIDTH)])
    return pl.pallas_call(
        _inproj_kernel,
        grid=(1,),
        in_specs=[full(d_model), _resident((1, d_model)), _resident(w_cat.shape), _resident((N_HEADS, 1))],
        out_specs=out_specs,
        out_shape=out_shape,
        compiler_params=_params(),
        name="inproj_sample",
    )(x, gain, w_cat, b_col)


def _inproj_t_kernel(x_ref, g_ref, wq_ref, wt_ref, bf_ref,
                     qa_ref, qb_ref, kat_ref, vat_ref, kbt_ref, vbt_ref,
                     kat32_ref, vat32_ref, logf_ref, kbt32_ref, vbt32_ref, *, tiles_per_seq):
    h = _rms(x_ref[...], g_ref[...]).astype(BF16)
    zq = _dot(h, wq_ref[...])
    qa_ref[...] = (zq[:, 0:WIDTH] * SCALE).astype(BF16)
    qb_ref[...] = (zq[:, WIDTH:2 * WIDTH] * SCALE).astype(BF16)

    def proj_t(c, rows=WIDTH):
        return _dot_nt(wt_ref[c * WIDTH:c * WIDTH + rows, :], h)

    z = proj_t(0)
    kat32_ref[0] = z
    kat_ref[0] = z.astype(BF16)
    z = proj_t(1)
    vat32_ref[0] = z
    vat_ref[0] = z.astype(BF16)
    zk = proj_t(2)
    kbt_ref[0] = zk.astype(BF16)
    zv = proj_t(3, WIDTH + 2 * N_HEADS)
    vbt_ref[0] = zv[0:WIDTH].astype(BF16)
    logf_ref[...] = _log_sigmoid(zv[WIDTH:WIDTH + N_HEADS] + bf_ref[...])

    @pl.when(pl.program_id(0) % tiles_per_seq == tiles_per_seq - 1)
    def _():
        kbt32_ref[0] = zk
        vbt32_ref[0] = zv[0:WIDTH]


def _inproj_t(x, gain, wq, wt, b_col, *, batch, seq, tm):
    n_tok, d_model = x.shape
    tps = seq // tm
    tok = lambda: pl.BlockSpec((tm, WIDTH), lambda i: (i, 0))
    feat = lambda: pl.BlockSpec((1, WIDTH, tm), lambda i: (i // tps, 0, i % tps))
    keep = lambda: pl.BlockSpec((1, WIDTH, tm), lambda i: (i // tps, 0, 0))
    out_shape = (
        [jax.ShapeDtypeStruct((n_tok, WIDTH), BF16)] * 2
        + [jax.ShapeDtypeStruct((batch, WIDTH, seq), BF16)] * 4
        + [jax.ShapeDtypeStruct((batch, WIDTH, seq), F32)] * 2
        + [jax.ShapeDtypeStruct((N_HEADS, n_tok), F32)]
        + [jax.ShapeDtypeStruct((batch, WIDTH, tm), F32)] * 2
    )
    out_specs = ([tok(), tok()] + [feat() for _ in range(6)]
                 + [pl.BlockSpec((N_HEADS, tm), lambda i: (0, i)), keep(), keep()])
    return pl.pallas_call(
        functools.partial(_inproj_t_kernel, tiles_per_seq=tps),
        grid=(n_tok // tm,),
        in_specs=[
            pl.BlockSpec((tm, d_model), lambda i: (i, 0)),
            _resident((1, d_model)),
            _resident(wq.shape),
            _resident(wt.shape),
            _resident((N_HEADS, 1)),
        ],
        out_specs=out_specs,
        out_shape=out_shape,
        compiler_params=_params(),
        name="inproj",
    )(x, gain, wq, wt, b_col)


def _lane_cumsum(x):
    n = x.shape[-1]
    lane = lax.broadcasted_iota(jnp.int32, x.shape, x.ndim - 1)
    shift = 1
    while shift < n:
        x = x + jnp.where(lane >= shift, pltpu.roll(x, shift, axis=x.ndim - 1), 0.0)
        shift *= 2
    return x


def _cumsum_kernel(lf_ref, crow_ref, ccol_ref):
    c = _lane_cumsum(lf_ref[...])
    crow_ref[...] = c
    seq = c.shape[1]
    ct = jnp.concatenate([c, jnp.zeros((PAIR - N_HEADS, seq), F32)], axis=0).T
    for p in range(N_PAIRS):
        ccol_ref[p] = ct[:, 2 * p:2 * p + 2]


def _cumsum(logf_t, *, batch, seq):
    n_tok = batch * seq
    return pl.pallas_call(
        _cumsum_kernel,
        grid=(batch,),
        in_specs=[pl.BlockSpec((N_HEADS, seq), lambda b: (0, b))],
        out_specs=[
            pl.BlockSpec((N_HEADS, seq), lambda b: (0, b)),
            pl.BlockSpec((N_PAIRS, seq, 2), lambda b: (0, b, 0)),
        ],
        out_shape=[
            jax.ShapeDtypeStruct((N_HEADS, n_tok), F32),
            jax.ShapeDtypeStruct((N_PAIRS, n_tok, 2), F32),
        ],
        compiler_params=_params(),
        name="cumsum",
    )(logf_t)


def _split_heads_q(q):
    lane = lax.broadcasted_iota(jnp.int32, q.shape, 1)
    zero = jnp.zeros_like(q)
    return jnp.concatenate([jnp.where(lane < HEAD_DIM, q, zero),
                            jnp.where(lane >= HEAD_DIM, q, zero)], axis=0)


def _augment_v(v, h, axis):
    idx = lax.broadcasted_iota(jnp.int32, v.shape, axis)
    own = (idx < HEAD_DIM) if h == 0 else (idx >= HEAD_DIM)
    return jnp.where(own, v, jnp.ones_like(v))


def _merge_heads_out(acc0, acc1):
    lane = lax.broadcasted_iota(jnp.int32, acc0.shape, 1)
    first = lane < HEAD_DIM
    num = jnp.where(first, acc0, acc1)
    den = pltpu.roll(jnp.where(first, acc1, acc0), HEAD_DIM, axis=1)
    return num / den


def _fox_kernel(q_ref, kt_ref, vt_ref, crow_ref, ccol_ref, o_ref, v0_ref, v1_ref, *, seq):
    vt = vt_ref[0]
    v0_ref[...] = _augment_v(vt, 0, 0)
    v1_ref[...] = _augment_v(vt, 1, 0)
    vaug = (v0_ref, v1_ref)
    row = lax.broadcasted_iota(jnp.int32, (Q_TILE, Q_TILE), 0)
    col = lax.broadcasted_iota(jnp.int32, (Q_TILE, Q_TILE), 1)
    causal = col <= row

    n_qb = seq // Q_TILE

    def scores(qb):
        qs, ke = qb * Q_TILE, (qb + 1) * Q_TILE
        return _dot(_split_heads_q(q_ref[qs:ke, :]), kt_ref[0, :, 0:ke])

    s_next = scores(0)
    for qb in range(n_qb):
        qs, ke = qb * Q_TILE, (qb + 1) * Q_TILE
        s = s_next
        if qb + 1 < n_qb:
            s_next = scores(qb + 1)
        accs = []
        for h in range(2):
            ck = crow_ref[0, h:h + 1, 0:ke]
            cq = ccol_ref[0, qs:ke, h:h + 1]
            t = s[h * Q_TILE:(h + 1) * Q_TILE, :] - ck
            t_diag = jnp.where(causal, t[:, qs:ke], -jnp.inf)
            m = jnp.max(t_diag, axis=1, keepdims=True)
            if qb > 0:
                t_past = t[:, 0:qs]
                m = jnp.maximum(m, jnp.max(t_past, axis=1, keepdims=True))
            shift = cq - (m + cq)
            acc = _dot_nt(jnp.exp(t_diag + shift).astype(BF16), vaug[h][:, qs:ke])
            if qb > 0:
                acc = acc + _dot_nt(jnp.exp(t_past + shift).astype(BF16), vaug[h][:, 0:qs])
            accs.append(acc)
        o_ref[qs:ke, :] = _merge_heads_out(accs[0], accs[1]).astype(o_ref.dtype)


def _fox(q, kt, vt, crow, ccol, *, batch, seq):
    n_tok = batch * seq
    tok = lambda: pl.BlockSpec((seq, PAIR), lambda b, p: (b, p))
    feat = lambda: pl.BlockSpec((1, PAIR, seq), lambda b, p: (b, p, 0))
    return pl.pallas_call(
        functools.partial(_fox_kernel, seq=seq),
        grid=(batch, N_PAIRS),
        in_specs=[
            tok(), feat(), feat(),
            pl.BlockSpec((1, 2, seq), lambda b, p: (p, 0, b)),
            pl.BlockSpec((1, seq, 2), lambda b, p: (p, b, 0)),
        ],
        out_specs=tok(),
        out_shape=jax.ShapeDtypeStruct((n_tok, WIDTH), BF16),
        scratch_shapes=[pltpu.VMEM((PAIR, seq), BF16), pltpu.VMEM((PAIR, seq), BF16)],
        compiler_params=_params(),
        name="fox",
    )(q, kt, vt, crow, ccol)


def _bias_kernel(rb_ref, tabm_ref, tabr_ref):
    rb = rb_ref[...]
    n_rows = rb.shape[1]
    r = lax.broadcasted_iota(jnp.int32, (n_rows, TOEPLITZ), 0)
    m = lax.broadcasted_iota(jnp.int32, (n_rows, TOEPLITZ), 1)
    e = jnp.where(m < TOEPLITZ // 2, m, m - TOEPLITZ)
    idx = jnp.clip(LEFT_REACH - Q_TILE - e, -REL_CLIP, REL_CLIP) + REL_CLIP
    onehot = jnp.where(r == idx, 1.0, 0.0).astype(BF16)
    hi = rb.astype(BF16)
    rest = rb - hi.astype(F32)
    mid = rest.astype(BF16)
    lo = (rest - mid.astype(F32)).astype(BF16)
    u = _dot(hi, onehot) + _dot(mid, onehot) + _dot(lo, onehot)

    i = lax.broadcasted_iota(jnp.int32, (Q_TILE, BAND_KEYS), 0)
    j = lax.broadcasted_iota(jnp.int32, (Q_TILE, BAND_KEYS), 1)
    qc = jnp.right_shift(i, 6)
    kc = jnp.right_shift(j, 6)
    valid = (kc >= qc) & (kc <= qc + LEFT_CHUNKS)
    n_new = tabr_ref.shape[1]
    for h in range(N_HEADS):
        base = jnp.broadcast_to(u[h:h + 1, :], (Q_TILE, TOEPLITZ))
        t = pltpu.roll(base, Q_TILE, axis=1, stride=1, stride_axis=0)[:, 0:BAND_KEYS]
        tabm_ref[h] = jnp.where(valid, t, MASKED)
        tabr_ref[h] = t[0:n_new, :]


def _bias_tables(rb_pad, *, n_new):
    return pl.pallas_call(
        _bias_kernel,
        out_shape=[
            jax.ShapeDtypeStruct((N_HEADS, Q_TILE, BAND_KEYS), F32),
            jax.ShapeDtypeStruct((N_HEADS, n_new, BAND_KEYS), F32),
        ],
        compiler_params=_params(),
        name="bias_tables",
    )(rb_pad)


def _band_kernel(q_ref, kt_ref, vt_ref, tab_ref, o_ref, v0_ref, v1_ref, *, seq):
    vt = vt_ref[0]
    v0_ref[...] = _augment_v(vt, 0, 0)
    v1_ref[...] = _augment_v(vt, 1, 0)
    vaug = (v0_ref, v1_ref)

    n_qb = seq // Q_TILE

    def scores(qb):
        qs, ke = qb * Q_TILE, (qb + 1) * Q_TILE
        ks = max(0, qs - LEFT_REACH)
        return _dot(_split_heads_q(q_ref[qs:ke, :]), kt_ref[0, :, ks:ke])

    s_next = scores(0)
    for qb in range(n_qb):
        qs, ke = qb * Q_TILE, (qb + 1) * Q_TILE
        ks = max(0, qs - LEFT_REACH)
        s = s_next
        if qb + 1 < n_qb:
            s_next = scores(qb + 1)
        accs = []
        for h in range(2):
            t = s[h * Q_TILE:(h + 1) * Q_TILE, :] + tab_ref[h, :, BAND_KEYS - (ke - ks):BAND_KEYS]
            m = jnp.max(t, axis=1, keepdims=True)
            accs.append(_dot_nt(jnp.exp(t - m).astype(BF16), vaug[h][:, ks:ke]))
        o_ref[qs:ke, :] = _merge_heads_out(accs[0], accs[1]).astype(o_ref.dtype)


def _band(q, kt, vt, tabm, *, batch, seq):
    n_tok = batch * seq
    tok = lambda: pl.BlockSpec((seq, PAIR), lambda p, b: (b, p))
    feat = lambda: pl.BlockSpec((1, PAIR, seq), lambda p, b: (b, p, 0))
    return pl.pallas_call(
        functools.partial(_band_kernel, seq=seq),
        grid=(N_PAIRS, batch),
        in_specs=[
            tok(), feat(), feat(),
            pl.BlockSpec((2, Q_TILE, BAND_KEYS), lambda p, b: (p, 0, 0)),
        ],
        out_specs=tok(),
        out_shape=jax.ShapeDtypeStruct((n_tok, WIDTH), BF16),
        scratch_shapes=[pltpu.VMEM((PAIR, seq), BF16), pltpu.VMEM((PAIR, seq), BF16)],
        compiler_params=_params(),
        name="band",
    )(q, kt, vt, tabm)


def _two_part_attention(q2, kt_cache, vt_cache, kn_ref, vn_ref, bias_cache, bias_new, n_q):
    s_c = _dot(q2, kt_cache)
    s_n = _dot_nt(q2, kn_ref[...])
    accs = []
    for h in range(2):
        rows = slice(h * n_q, (h + 1) * n_q)
        t_c = s_c[rows, :] + bias_cache(h)
        t_n = s_n[rows, :] + bias_new(h)
        m = jnp.maximum(jnp.max(t_c, axis=1, keepdims=True), jnp.max(t_n, axis=1, keepdims=True))
        acc = _dot_nt(jnp.exp(t_c - m).astype(BF16), _augment_v(vt_cache, h, 0))
        acc = acc + _dot(jnp.exp(t_n - m).astype(BF16), _augment_v(vn_ref[...], h, 1))
        accs.append(acc)
    return _merge_heads_out(accs[0], accs[1])


def _sample_attn_kernel(qa_ref, ka_ref, va_ref, qb_ref, kb_ref, vb_ref,
                        cak_ref, cav_ref, cbk_ref, cbv_ref, clf_ref, nlf_ref, tab_ref,
                        oa_ref, ob_ref, kn_ref, vn_ref, *, n_q):
    row = lax.broadcasted_iota(jnp.int32, (n_q, NEW_PAD), 0)
    col = lax.broadcasted_iota(jnp.int32, (n_q, NEW_PAD), 1)

    def stage_new(k_new, v_new):
        kn_ref[...] = jnp.zeros_like(kn_ref)
        vn_ref[...] = jnp.zeros_like(vn_ref)
        kn_ref[0:n_q, :] = k_new
        vn_ref[0:n_q, :] = v_new

    c_cache = _lane_cumsum(clf_ref[0, 0])
    n_cache = c_cache.shape[1]
    c_new = _lane_cumsum(nlf_ref[0, 0]) + c_cache[:, n_cache - 1:n_cache]

    def fox_cache(h):
        cq = jnp.sum(jnp.where(col == row, c_new[h:h + 1, :], 0.0), axis=1, keepdims=True)
        return cq - c_cache[h:h + 1, :]

    def fox_new(h):
        cq = jnp.sum(jnp.where(col == row, c_new[h:h + 1, :], 0.0), axis=1, keepdims=True)
        return jnp.where(col <= row, cq - c_new[h:h + 1, :], -jnp.inf)

    stage_new(ka_ref[...], va_ref[...])
    oa = _two_part_attention(_split_heads_q(qa_ref[...]), cak_ref[0].astype(BF16), cav_ref[0].astype(BF16),
                             kn_ref, vn_ref, fox_cache, fox_new, n_q)
    oa_ref[...] = oa.astype(oa_ref.dtype)

    n_band = cbk_ref.shape[2]

    def band_cache(h):
        return tab_ref[h, :, 0:n_band]

    def band_new(h):
        return jnp.where(col < n_q, tab_ref[h, :, n_band:n_band + NEW_PAD], -jnp.inf)

    stage_new(kb_ref[...], vb_ref[...])
    ob = _two_part_attention(_split_heads_q(qb_ref[...]), cbk_ref[0].astype(BF16), cbv_ref[0].astype(BF16),
                             kn_ref, vn_ref, band_cache, band_new, n_q)
    ob_ref[...] = ob.astype(ob_ref.dtype)


def _sample_attn(qa, ka, va, qb, kb, vb, cakt, cavt, cbkt, cbvt, clf, nlf, tabr, *, batch, n_q):
    n_tok = batch * n_q
    past = cakt.shape[2]
    n_band = cbkt.shape[2]
    new = lambda: pl.BlockSpec((n_q, PAIR), lambda b, p: (b, p))
    cache = lambda n: pl.BlockSpec((1, PAIR, n), lambda b, p: (b, p, 0))
    return pl.pallas_call(
        functools.partial(_sample_attn_kernel, n_q=n_q),
        grid=(batch, N_PAIRS),
        in_specs=[
            new(), new(), new(), new(), new(), new(),
            cache(past), cache(past), cache(n_band), cache(n_band),
            pl.BlockSpec((1, 1, 2, past), lambda b, p: (b, p, 0, 0)),
            pl.BlockSpec((1, 1, 2, NEW_PAD), lambda b, p: (b, p, 0, 0)),
            pl.BlockSpec((2, n_q, BAND_KEYS), lambda b, p: (p, 0, 0)),
        ],
        out_specs=[new(), new()],
        out_shape=[jax.ShapeDtypeStruct((n_tok, WIDTH), BF16)] * 2,
        scratch_shapes=[pltpu.VMEM((NEW_PAD, PAIR), BF16), pltpu.VMEM((NEW_PAD, PAIR), BF16)],
        compiler_params=_params(),
        name="sample_attn",
    )(qa, ka, va, qb, kb, vb, cakt, cavt, cbkt, cbvt, clf, nlf, tabr)


def _post_kernel(x_ref, oa_ref, ob_ref, ga_ref, gb_ref, wo_ref, gpost_ref, gpre_ref,
                 wg_ref, wu_ref, wd_ref, gffn_ref, y_ref):
    na = _rms(oa_ref[...].astype(F32), ga_ref[...]).astype(BF16)
    nb = _rms(ob_ref[...].astype(F32), gb_ref[...]).astype(BF16)
    mix = _dot(na, wo_ref[0:WIDTH, :]) + _dot(nb, wo_ref[WIDTH:2 * WIDTH, :])
    x1 = x_ref[...] + _rms(mix, gpost_ref[...])
    hf = _rms(x1, gpre_ref[...]).astype(BF16)
    g = _dot(hf, wg_ref[...])
    u = _dot(hf, wu_ref[...])
    a = (g * jax.nn.sigmoid(g) * u).astype(BF16)
    ff = _dot(a, wd_ref[...])
    y_ref[...] = x1 + _rms(ff, gffn_ref[...])


def _post(x, oa, ob, ga, gb, wo, gpost, gpre, wg, wu, wd, gffn, *, tm):
    n_tok, d_model = x.shape
    row = lambda w: pl.BlockSpec((tm, w), lambda i: (i, 0))
    return pl.pallas_call(
        _post_kernel,
        grid=(n_tok // tm,),
        in_specs=[
            row(d_model), row(WIDTH), row(WIDTH),
            _resident(ga.shape), _resident(gb.shape), _resident(wo.shape),
            _resident(gpost.shape), _resident(gpre.shape),
            _resident(wg.shape), _resident(wu.shape), _resident(wd.shape), _resident(gffn.shape),
        ],
        out_specs=row(d_model),
        out_shape=jax.ShapeDtypeStruct((n_tok, d_model), F32),
        compiler_params=_params(),
        name="post",
    )(x, oa, ob, ga, gb, wo, gpost, gpre, wg, wu, wd, gffn)


def _row(v):
    return v.reshape(1, -1).astype(F32)


def _state_from_feature_major(t, batch, n):
    return t.reshape(batch, N_HEADS, HEAD_DIM, n).transpose(0, 3, 1, 2)


def _cache_feature_major(c):
    b, n = c.shape[0], c.shape[1]
    return c.transpose(0, 2, 3, 1).reshape(b, WIDTH, n)


def kernel(x_prompt, x_sample, cache_a_k, cache_a_v, cache_a_logf, cache_b_k, cache_b_v, norm_mix_pre, w_in, b_forget, rel_bias, gain_out_a, gain_out_b, w_out, norm_mix_post, norm_ffn_pre, w_gate, w_up, w_down, norm_ffn_post):
    batch, seq, d_model = x_prompt.shape
    dec_batch, dec_seq, _ = x_sample.shape
    depth = w_in.shape[0]
    past = cache_a_k.shape[2]
    n_band = cache_b_k.shape[2]
    keep = min(LEFT_REACH, seq)
    assert seq % Q_TILE == 0 and keep == LEFT_REACH and n_band == LEFT_REACH and dec_seq <= CHUNK
    n_tok_p, n_tok_s = batch * seq, dec_batch * dec_seq
    tm_p = keep

    xp = x_prompt.reshape(n_tok_p, d_model)
    xs = x_sample.reshape(n_tok_s, d_model)
    outs = [[] for _ in range(10)]
    fa0, fa1 = 3 * WIDTH, 3 * WIDTH + N_HEADS
    for l in range(depth):
        wl = w_in[l]
        qa_w, ka_w, va_w = wl[:, 0:WIDTH], wl[:, WIDTH:2 * WIDTH], wl[:, 2 * WIDTH:fa0]
        f_w = wl[:, fa0:fa1]
        qb_w, kb_w, vb_w = wl[:, fa1:fa1 + WIDTH], wl[:, fa1 + WIDTH:fa1 + 2 * WIDTH], wl[:, fa1 + 2 * WIDTH:]
        w_cat = jnp.concatenate(
            [qa_w, ka_w, va_w, qb_w, kb_w, vb_w, jnp.pad(f_w, ((0, 0), (0, PAIR - N_HEADS)))], axis=1).astype(BF16)
        w_q = jnp.concatenate([qa_w, qb_w], axis=1).astype(BF16)
        w_t = jnp.concatenate(
            [ka_w, va_w, kb_w, vb_w, jnp.pad(f_w, ((0, 0), (0, N_HEADS)))], axis=1).T.astype(BF16)
        b_col = b_forget[l].reshape(N_HEADS, 1).astype(F32)
        rb_pad = jnp.pad(rel_bias[l].astype(F32), ((0, 16 - N_HEADS), (0, 384 - (2 * REL_CLIP + 1))))
        tail = (_row(gain_out_a[l]), _row(gain_out_b[l]), w_out[l].astype(BF16), _row(norm_mix_post[l]),
                _row(norm_ffn_pre[l]), w_gate[l].astype(BF16), w_up[l].astype(BF16), w_down[l].astype(BF16),
                _row(norm_ffn_post[l]))
        tabm, tabr = _bias_tables(rb_pad, n_new=dec_seq)

        qa, qb, kat, vat, kbt, vbt, kat32, vat32, logf_t, kbt32, vbt32 = _inproj_t(
            xp, _row(norm_mix_pre[l]), w_q, w_t, b_col, batch=batch, seq=seq, tm=tm_p)
        crow, ccol = _cumsum(logf_t, batch=batch, seq=seq)
        oa = _fox(qa, kat, vat, crow.reshape(N_PAIRS, 2, n_tok_p), ccol, batch=batch, seq=seq)
        ob = _band(qb, kbt, vbt, tabm, batch=batch, seq=seq)
        xp = _post(xp, oa, ob, *tail, tm=256)
        outs[0].append(_state_from_feature_major(kat32, batch, seq))
        outs[1].append(_state_from_feature_major(vat32, batch, seq))
        outs[2].append(logf_t.T.reshape(batch, seq, N_HEADS))
        outs[3].append(_state_from_feature_major(kbt32, batch, keep))
        outs[4].append(_state_from_feature_major(vbt32, batch, keep))

        qa, ka, va, qb, kb, vb, ka32, va32, logf_t, kb32, vb32 = _inproj(xs, _row(norm_mix_pre[l]), w_cat, b_col)
        clf = cache_a_logf[l].astype(F32).transpose(0, 2, 1).reshape(dec_batch, N_PAIRS, 2, past)
        nlf = logf_t.reshape(N_PAIRS, 2, dec_batch, dec_seq).transpose(2, 0, 1, 3)
        nlf = jnp.pad(nlf, ((0, 0), (0, 0), (0, 0), (0, NEW_PAD - dec_seq)))
        oa, ob = _sample_attn(
            qa, ka, va, qb, kb, vb,
            _cache_feature_major(cache_a_k[l]), _cache_feature_major(cache_a_v[l]),
            _cache_feature_major(cache_b_k[l]), _cache_feature_major(cache_b_v[l]),
            clf, nlf, tabr, batch=dec_batch, n_q=dec_seq)
        xs = _post(xs, oa, ob, *tail, tm=n_tok_s)
        outs[5].append(ka32.reshape(dec_batch, dec_seq, N_HEADS, HEAD_DIM))
        outs[6].append(va32.reshape(dec_batch, dec_seq, N_HEADS, HEAD_DIM))
        outs[7].append(logf_t.T.reshape(dec_batch, dec_seq, N_HEADS))
        outs[8].append(kb32.reshape(dec_batch, dec_seq, N_HEADS, HEAD_DIM))
        outs[9].append(vb32.reshape(dec_batch, dec_seq, N_HEADS, HEAD_DIM))

    st = jnp.stack
    return (xp.reshape(batch, seq, d_model), xs.reshape(dec_batch, dec_seq, d_model),
            *[st(o) for o in outs])
```

```python
import functools

import jax
import jax.numpy as jnp
from jax import lax
from jax.experimental import pallas as pl
from jax.experimental.pallas import tpu as pltpu

F32 = jnp.float32
BF16 = jnp.bfloat16

HEAD_DIM = 64
N_HEADS = 8
WIDTH = N_HEADS * HEAD_DIM
PAIR = 2 * HEAD_DIM
N_PAIRS = N_HEADS // 2
CHUNK = 64
LEFT_CHUNKS = 8
LEFT_REACH = LEFT_CHUNKS * CHUNK
REL_CLIP = 128
EPS = 1e-6
SCALE = HEAD_DIM ** -0.5
LOG2E = 1.4426950408889634
Q_TILE = 256
BAND_KEYS = LEFT_REACH + Q_TILE
TOEPLITZ = 1024
MASKED = -1e30
NEW_PAD = 128
VMEM_LIMIT = 56 * 1024 * 1024


def _params():
    return pltpu.CompilerParams(vmem_limit_bytes=VMEM_LIMIT)


def _resident(shape):
    return pl.BlockSpec(shape, lambda *_: (0,) * len(shape), pipeline_mode=pl.Buffered(1))


def _rms(x, g):
    ms = jnp.mean(x * x, axis=-1, keepdims=True)
    return x * lax.rsqrt(ms + EPS) * g


def _log_sigmoid(x):
    return jnp.minimum(x, 0.0) - jnp.log1p(jnp.exp(-jnp.abs(x)))


def _dot(a, b):
    return jnp.dot(a, b, preferred_element_type=F32)


def _dot_nt(a, b):
    return lax.dot_general(a, b, (((1,), (1,)), ((), ())), preferred_element_type=F32)


def _inproj_kernel(x_ref, g_ref, w_ref, bf_ref,
                   qa_ref, ka_ref, va_ref, qb_ref, kb_ref, vb_ref,
                   ka32_ref, va32_ref, logf_ref, kb32_ref, vb32_ref):
    h = _rms(x_ref[...], g_ref[...]).astype(BF16)

    def proj(c):
        return _dot(h, w_ref[:, c * WIDTH:(c + 1) * WIDTH])

    qa_ref[...] = (proj(0) * SCALE).astype(BF16)
    z = proj(1)
    ka32_ref[...] = z
    ka_ref[...] = z.astype(BF16)
    z = proj(2)
    va32_ref[...] = z
    va_ref[...] = z.astype(BF16)
    qb_ref[...] = (proj(3) * SCALE).astype(BF16)
    z = proj(4)
    kb32_ref[...] = z
    kb_ref[...] = z.astype(BF16)
    z = proj(5)
    vb32_ref[...] = z
    vb_ref[...] = z.astype(BF16)
    f = _dot(h, w_ref[:, 6 * WIDTH:6 * WIDTH + PAIR])
    logf_ref[...] = _log_sigmoid(f.T[0:N_HEADS, :] + bf_ref[...])


def _inproj(x, gain, w_cat, b_col):
    n_tok, d_model = x.shape
    full = lambda w: pl.BlockSpec((n_tok, w), lambda i: (0, 0))
    out_shape = (
        [jax.ShapeDtypeStruct((n_tok, WIDTH), BF16)] * 6
        + [jax.ShapeDtypeStruct((n_tok, WIDTH), F32)] * 2
        + [jax.ShapeDtypeStruct((N_HEADS, n_tok), F32)]
        + [jax.ShapeDtypeStruct((n_tok, WIDTH), F32)] * 2
    )
    out_specs = ([full(WIDTH) for _ in range(8)]
                 + [pl.BlockSpec((N_HEADS, n_tok), lambda i: (0, 0)), full(WIDTH), full(WIDTH)])
    return pl.pallas_call(
        _inproj_kernel,
        grid=(1,),
        in_specs=[full(d_model), _resident((1, d_model)), _resident(w_cat.shape), _resident((N_HEADS, 1))],
        out_specs=out_specs,
        out_shape=out_shape,
        compiler_params=_params(),
        name="inproj_sample",
    )(x, gain, w_cat, b_col)


def _inproj_t_kernel(x_ref, g_ref, wq_ref, wt_ref, bf_ref,
                     qa_ref, qb_ref, kat_ref, vat_ref, kbt_ref, vbt_ref,
                     kat32_ref, vat32_ref, logf_ref, kbt32_ref, vbt32_ref, *, tiles_per_seq):
    h = _rms(x_ref[...], g_ref[...]).astype(BF16)
    zq = _dot(h, wq_ref[...])
    qa_ref[...] = (zq[:, 0:WIDTH] * (SCALE * LOG2E)).astype(BF16)
    qb_ref[...] = (zq[:, WIDTH:2 * WIDTH] * (SCALE * LOG2E)).astype(BF16)

    def proj_t(c, rows=WIDTH):
        return _dot_nt(wt_ref[c * WIDTH:c * WIDTH + rows, :], h)

    z = proj_t(0)
    kat32_ref[0] = z
    kat_ref[0] = z.astype(BF16)
    z = proj_t(1)
    vat32_ref[0] = z
    vat_ref[0] = z.astype(BF16)
    zk = proj_t(2)
    kbt_ref[0] = zk.astype(BF16)
    zv = proj_t(3, WIDTH + 2 * N_HEADS)
    vbt_ref[0] = zv[0:WIDTH].astype(BF16)
    logf_ref[...] = _log_sigmoid(zv[WIDTH:WIDTH + N_HEADS] + bf_ref[...])

    @pl.when(pl.program_id(0) % tiles_per_seq == tiles_per_seq - 1)
    def _():
        kbt32_ref[0] = zk
        vbt32_ref[0] = zv[0:WIDTH]


def _inproj_t(x, gain, wq, wt, b_col, *, batch, seq, tm):
    n_tok, d_model = x.shape
    tps = seq // tm
    tok = lambda: pl.BlockSpec((tm, WIDTH), lambda i: (i, 0))
    feat = lambda: pl.BlockSpec((1, WIDTH, tm), lambda i: (i // tps, 0, i % tps))
    keep = lambda: pl.BlockSpec((1, WIDTH, tm), lambda i: (i // tps, 0, 0))
    out_shape = (
        [jax.ShapeDtypeStruct((n_tok, WIDTH), BF16)] * 2
        + [jax.ShapeDtypeStruct((batch, WIDTH, seq), BF16)] * 4
        + [jax.ShapeDtypeStruct((batch, WIDTH, seq), F32)] * 2
        + [jax.ShapeDtypeStruct((N_HEADS, n_tok), F32)]
        + [jax.ShapeDtypeStruct((batch, WIDTH, tm), F32)] * 2
    )
    out_specs = ([tok(), tok()] + [feat() for _ in range(6)]
                 + [pl.BlockSpec((N_HEADS, tm), lambda i: (0, i)), keep(), keep()])
    return pl.pallas_call(
        functools.partial(_inproj_t_kernel, tiles_per_seq=tps),
        grid=(n_tok // tm,),
        in_specs=[
            pl.BlockSpec((tm, d_model), lambda i: (i, 0)),
            _resident((1, d_model)),
            _resident(wq.shape),
            _resident(wt.shape),
            _resident((N_HEADS, 1)),
        ],
        out_specs=out_specs,
        out_shape=out_shape,
        compiler_params=_params(),
        name="inproj",
    )(x, gain, wq, wt, b_col)


def _lane_cumsum(x):
    n = x.shape[-1]
    lane = lax.broadcasted_iota(jnp.int32, x.shape, x.ndim - 1)
    shift = 1
    while shift < n:
        x = x + jnp.where(lane >= shift, pltpu.roll(x, shift, axis=x.ndim - 1), 0.0)
        shift *= 2
    return x


def _cumsum_kernel(lf_ref, crow_ref, ccol_ref):
    c = _lane_cumsum(lf_ref[...])
    crow_ref[...] = c
    seq = c.shape[1]
    ct = jnp.concatenate([c, jnp.zeros((PAIR - N_HEADS, seq), F32)], axis=0).T
    for p in range(N_PAIRS):
        ccol_ref[p] = ct[:, 2 * p:2 * p + 2]


def _cumsum(logf_t, *, batch, seq):
    n_tok = batch * seq
    return pl.pallas_call(
        _cumsum_kernel,
        grid=(batch,),
        in_specs=[pl.BlockSpec((N_HEADS, seq), lambda b: (0, b))],
        out_specs=[
            pl.BlockSpec((N_HEADS, seq), lambda b: (0, b)),
            pl.BlockSpec((N_PAIRS, seq, 2), lambda b: (0, b, 0)),
        ],
        out_shape=[
            jax.ShapeDtypeStruct((N_HEADS, n_tok), F32),
            jax.ShapeDtypeStruct((N_PAIRS, n_tok, 2), F32),
        ],
        compiler_params=_params(),
        name="cumsum",
    )(logf_t)


def _split_heads_q(q):
    lane = lax.broadcasted_iota(jnp.int32, q.shape, 1)
    zero = jnp.zeros_like(q)
    return jnp.concatenate([jnp.where(lane < HEAD_DIM, q, zero),
                            jnp.where(lane >= HEAD_DIM, q, zero)], axis=0)


def _augment_v(v, h, axis):
    idx = lax.broadcasted_iota(jnp.int32, v.shape, axis)
    own = (idx < HEAD_DIM) if h == 0 else (idx >= HEAD_DIM)
    return jnp.where(own, v, jnp.ones_like(v))


def _merge_heads_out(acc0, acc1):
    lane = lax.broadcasted_iota(jnp.int32, acc0.shape, 1)
    first = lane < HEAD_DIM
    num = jnp.where(first, acc0, acc1)
    den = pltpu.roll(jnp.where(first, acc1, acc0), HEAD_DIM, axis=1)
    return num / den


def _split3(x):
    hi = x.astype(BF16).astype(F32)
    rest = x - hi
    mid = rest.astype(BF16).astype(F32)
    return hi, mid, rest - mid


def _fox_kernel(q_ref, kt_ref, vt_ref, crow_ref, ccol_ref, o_ref, v0_ref, v1_ref, kx_ref, *, seq):
    vt = vt_ref[0]
    v0_ref[...] = _augment_v(vt, 0, 0)
    v1_ref[...] = _augment_v(vt, 1, 0)
    vaug = (v0_ref, v1_ref)
    row = lax.broadcasted_iota(jnp.int32, (Q_TILE, Q_TILE), 0)
    col = lax.broadcasted_iota(jnp.int32, (Q_TILE, Q_TILE), 1)
    causal = col <= row

    n_ext = 16
    er = lax.broadcasted_iota(jnp.int32, (n_ext, seq), 0)
    ext = jnp.zeros((n_ext, seq), F32)
    for h in range(2):
        for j, piece in enumerate(_split3(crow_ref[0, h:h + 1, :] * LOG2E)):
            ext = jnp.where(er == 3 * h + j, piece, ext)
    kx_ref[0:PAIR, :] = kt_ref[0]
    kx_ref[PAIR:PAIR + n_ext, :] = ext.astype(BF16)
    kx_ref[PAIR + n_ext:2 * PAIR, :] = jnp.zeros((PAIR - n_ext, seq), BF16)
    cr = lax.broadcasted_iota(jnp.int32, (2 * Q_TILE, PAIR), 0)
    cl = lax.broadcasted_iota(jnp.int32, (2 * Q_TILE, PAIR), 1)
    first = jnp.where(cr >= Q_TILE, 3, 0)
    coef = jnp.where((cl >= first) & (cl < first + 3), -1.0, 0.0).astype(BF16)

    n_qb = seq // Q_TILE

    def scores(qb):
        qs, ke = qb * Q_TILE, (qb + 1) * Q_TILE
        q2 = jnp.concatenate([_split_heads_q(q_ref[qs:ke, :]), coef], axis=1)
        return _dot(q2, kx_ref[:, 0:ke])

    s_next = scores(0)
    for qb in range(n_qb):
        qs, ke = qb * Q_TILE, (qb + 1) * Q_TILE
        s = s_next
        if qb + 1 < n_qb:
            s_next = scores(qb + 1)
        accs = []
        for h in range(2):
            cq = ccol_ref[0, qs:ke, h:h + 1] * LOG2E
            t = s[h * Q_TILE:(h + 1) * Q_TILE, :]
            t_diag = jnp.where(causal, t[:, qs:ke], -jnp.inf)
            m = jnp.max(t_diag, axis=1, keepdims=True)
            if qb > 0:
                t_past = t[:, 0:qs]
                m = jnp.maximum(m, jnp.max(t_past, axis=1, keepdims=True))
            shift = cq - (m + cq)
            acc = _dot_nt(jnp.exp2(t_diag + shift).astype(BF16), vaug[h][:, qs:ke])
            if qb > 0:
                acc = acc + _dot_nt(jnp.exp2(t_past + shift).astype(BF16), vaug[h][:, 0:qs])
            accs.append(acc)
        o_ref[qs:ke, :] = _merge_heads_out(accs[0], accs[1]).astype(o_ref.dtype)


def _fox(q, kt, vt, crow, ccol, *, batch, seq):
    n_tok = batch * seq
    tok = lambda: pl.BlockSpec((seq, PAIR), lambda b, p: (b, p))
    feat = lambda: pl.BlockSpec((1, PAIR, seq), lambda b, p: (b, p, 0))
    return pl.pallas_call(
        functools.partial(_fox_kernel, seq=seq),
        grid=(batch, N_PAIRS),
        in_specs=[
            tok(), feat(), feat(),
            pl.BlockSpec((1, 2, seq), lambda b, p: (p, 0, b)),
            pl.BlockSpec((1, seq, 2), lambda b, p: (p, b, 0)),
        ],
        out_specs=tok(),
        out_shape=jax.ShapeDtypeStruct((n_tok, WIDTH), BF16),
        scratch_shapes=[pltpu.VMEM((PAIR, seq), BF16), pltpu.VMEM((PAIR, seq), BF16),
                        pltpu.VMEM((2 * PAIR, seq), BF16)],
        compiler_params=_params(),
        name="fox",
    )(q, kt, vt, crow, ccol)


def _bias_kernel(rb_ref, tabm_ref, tabr_ref):
    rb = rb_ref[...]
    n_rows = rb.shape[1]
    r = lax.broadcasted_iota(jnp.int32, (n_rows, TOEPLITZ), 0)
    m = lax.broadcasted_iota(jnp.int32, (n_rows, TOEPLITZ), 1)
    e = jnp.where(m < TOEPLITZ // 2, m, m - TOEPLITZ)
    idx = jnp.clip(LEFT_REACH - Q_TILE - e, -REL_CLIP, REL_CLIP) + REL_CLIP
    onehot = jnp.where(r == idx, 1.0, 0.0).astype(BF16)
    hi = rb.astype(BF16)
    rest = rb - hi.astype(F32)
    mid = rest.astype(BF16)
    lo = (rest - mid.astype(F32)).astype(BF16)
    u = _dot(hi, onehot) + _dot(mid, onehot) + _dot(lo, onehot)

    i = lax.broadcasted_iota(jnp.int32, (Q_TILE, BAND_KEYS), 0)
    j = lax.broadcasted_iota(jnp.int32, (Q_TILE, BAND_KEYS), 1)
    qc = jnp.right_shift(i, 6)
    kc = jnp.right_shift(j, 6)
    valid = (kc >= qc) & (kc <= qc + LEFT_CHUNKS)
    n_new = tabr_ref.shape[1]
    for h in range(N_HEADS):
        base = jnp.broadcast_to(u[h:h + 1, :], (Q_TILE, TOEPLITZ))
        t = pltpu.roll(base, Q_TILE, axis=1, stride=1, stride_axis=0)[:, 0:BAND_KEYS]
        tabm_ref[h] = jnp.where(valid, t * LOG2E, MASKED)
        tabr_ref[h] = t[0:n_new, :]


def _bias_tables(rb_pad, *, n_new):
    return pl.pallas_call(
        _bias_kernel,
        out_shape=[
            jax.ShapeDtypeStruct((N_HEADS, Q_TILE, BAND_KEYS), F32),
            jax.ShapeDtypeStruct((N_HEADS, n_new, BAND_KEYS), F32),
        ],
        compiler_params=_params(),
        name="bias_tables",
    )(rb_pad)


def _band_kernel(q_ref, kt_ref, vt_ref, tab_ref, o_ref, v0_ref, v1_ref, *, seq):
    vt = vt_ref[0]
    v0_ref[...] = _augment_v(vt, 0, 0)
    v1_ref[...] = _augment_v(vt, 1, 0)
    vaug = (v0_ref, v1_ref)

    n_qb = seq // Q_TILE

    def scores(qb):
        qs, ke = qb * Q_TILE, (qb + 1) * Q_TILE
        ks = max(0, qs - LEFT_REACH)
        return _dot(_split_heads_q(q_ref[qs:ke, :]), kt_ref[0, :, ks:ke])

    s_next = scores(0)
    for qb in range(n_qb):
        qs, ke = qb * Q_TILE, (qb + 1) * Q_TILE
        ks = max(0, qs - LEFT_REACH)
        s = s_next
        if qb + 1 < n_qb:
            s_next = scores(qb + 1)
        accs = []
        for h in range(2):
            t = s[h * Q_TILE:(h + 1) * Q_TILE, :] + tab_ref[h, :, BAND_KEYS - (ke - ks):BAND_KEYS]
            m = jnp.max(t, axis=1, keepdims=True)
            accs.append(_dot_nt(jnp.exp2(t - m).astype(BF16), vaug[h][:, ks:ke]))
        o_ref[qs:ke, :] = _merge_heads_out(accs[0], accs[1]).astype(o_ref.dtype)


def _band(q, kt, vt, tabm, *, batch, seq):
    n_tok = batch * seq
    tok = lambda: pl.BlockSpec((seq, PAIR), lambda p, b: (b, p))
    feat = lambda: pl.BlockSpec((1, PAIR, seq), lambda p, b: (b, p, 0))
    return pl.pallas_call(
        functools.partial(_band_kernel, seq=seq),
        grid=(N_PAIRS, batch),
        in_specs=[
            tok(), feat(), feat(),
            pl.BlockSpec((2, Q_TILE, BAND_KEYS), lambda p, b: (p, 0, 0)),
        ],
        out_specs=tok(),
        out_shape=jax.ShapeDtypeStruct((n_tok, WIDTH), BF16),
        scratch_shapes=[pltpu.VMEM((PAIR, seq), BF16), pltpu.VMEM((PAIR, seq), BF16)],
        compiler_params=_params(),
        name="band",
    )(q, kt, vt, tabm)


def _two_part_attention(q2, kt_cache, vt_cache, kn_ref, vn_ref, bias_cache, bias_new, n_q):
    s_c = _dot(q2, kt_cache)
    s_n = _dot_nt(q2, kn_ref[...])
    accs = []
    for h in range(2):
        rows = slice(h * n_q, (h + 1) * n_q)
        t_c = s_c[rows, :] + bias_cache(h)
        t_n = s_n[rows, :] + bias_new(h)
        m = jnp.maximum(jnp.max(t_c, axis=1, keepdims=True), jnp.max(t_n, axis=1, keepdims=True))
        acc = _dot_nt(jnp.exp(t_c - m).astype(BF16), _augment_v(vt_cache, h, 0))
        acc = acc + _dot(jnp.exp(t_n - m).astype(BF16), _augment_v(vn_ref[...], h, 1))
        accs.append(acc)
    return _merge_heads_out(accs[0], accs[1])


def _sample_attn_kernel(qa_ref, ka_ref, va_ref, qb_ref, kb_ref, vb_ref,
                        cak_ref, cav_ref, cbk_ref, cbv_ref, clf_ref, nlf_ref, tab_ref,
                        oa_ref, ob_ref, kn_ref, vn_ref, *, n_q):
    row = lax.broadcasted_iota(jnp.int32, (n_q, NEW_PAD), 0)
    col = lax.broadcasted_iota(jnp.int32, (n_q, NEW_PAD), 1)

    def stage_new(k_new, v_new):
        kn_ref[...] = jnp.zeros_like(kn_ref)
        vn_ref[...] = jnp.zeros_like(vn_ref)
        kn_ref[0:n_q, :] = k_new
        vn_ref[0:n_q, :] = v_new

    c_cache = _lane_cumsum(clf_ref[0, 0])
    n_cache = c_cache.shape[1]
    c_new = _lane_cumsum(nlf_ref[0, 0]) + c_cache[:, n_cache - 1:n_cache]

    def fox_cache(h):
        cq = jnp.sum(jnp.where(col == row, c_new[h:h + 1, :], 0.0), axis=1, keepdims=True)
        return cq - c_cache[h:h + 1, :]

    def fox_new(h):
        cq = jnp.sum(jnp.where(col == row, c_new[h:h + 1, :], 0.0), axis=1, keepdims=True)
        return jnp.where(col <= row, cq - c_new[h:h + 1, :], -jnp.inf)

    stage_new(ka_ref[...], va_ref[...])
    oa = _two_part_attention(_split_heads_q(qa_ref[...]), cak_ref[0].astype(BF16), cav_ref[0].astype(BF16),
                             kn_ref, vn_ref, fox_cache, fox_new, n_q)
    oa_ref[...] = oa.astype(oa_ref.dtype)

    n_band = cbk_ref.shape[2]

    def band_cache(h):
        return tab_ref[h, :, 0:n_band]

    def band_new(h):
        return jnp.where(col < n_q, tab_ref[h, :, n_band:n_band + NEW_PAD], -jnp.inf)

    stage_new(kb_ref[...], vb_ref[...])
    ob = _two_part_attention(_split_heads_q(qb_ref[...]), cbk_ref[0].astype(BF16), cbv_ref[0].astype(BF16),
                             kn_ref, vn_ref, band_cache, band_new, n_q)
    ob_ref[...] = ob.astype(ob_ref.dtype)


def _sample_attn(qa, ka, va, qb, kb, vb, cakt, cavt, cbkt, cbvt, clf, nlf, tabr, *, batch, n_q):
    n_tok = batch * n_q
    past = cakt.shape[2]
    n_band = cbkt.shape[2]
    new = lambda: pl.BlockSpec((n_q, PAIR), lambda b, p: (b, p))
    cache = lambda n: pl.BlockSpec((1, PAIR, n), lambda b, p: (b, p, 0))
    return pl.pallas_call(
        functools.partial(_sample_attn_kernel, n_q=n_q),
        grid=(batch, N_PAIRS),
        in_specs=[
            new(), new(), new(), new(), new(), new(),
            cache(past), cache(past), cache(n_band), cache(n_band),
            pl.BlockSpec((1, 1, 2, past), lambda b, p: (b, p, 0, 0)),
            pl.BlockSpec((1, 1, 2, NEW_PAD), lambda b, p: (b, p, 0, 0)),
            pl.BlockSpec((2, n_q, BAND_KEYS), lambda b, p: (p, 0, 0)),
        ],
        out_specs=[new(), new()],
        out_shape=[jax.ShapeDtypeStruct((n_tok, WIDTH), BF16)] * 2,
        scratch_shapes=[pltpu.VMEM((NEW_PAD, PAIR), BF16), pltpu.VMEM((NEW_PAD, PAIR), BF16)],
        compiler_params=_params(),
        name="sample_attn",
    )(qa, ka, va, qb, kb, vb, cakt, cavt, cbkt, cbvt, clf, nlf, tabr)


def _post_kernel(x_ref, oa_ref, ob_ref, ga_ref, gb_ref, wo_ref, gpost_ref, gpre_ref,
                 wg_ref, wu_ref, wd_ref, gffn_ref, y_ref):
    na = _rms(oa_ref[...].astype(F32), ga_ref[...]).astype(BF16)
    nb = _rms(ob_ref[...].astype(F32), gb_ref[...]).astype(BF16)
    mix = _dot(na, wo_ref[0:WIDTH, :]) + _dot(nb, wo_ref[WIDTH:2 * WIDTH, :])
    x1 = x_ref[...] + _rms(mix, gpost_ref[...])
    hf = _rms(x1, gpre_ref[...]).astype(BF16)
    g = _dot(hf, wg_ref[...])
    u = _dot(hf, wu_ref[...])
    a = (g * jax.nn.sigmoid(g) * u).astype(BF16)
    ff = _dot(a, wd_ref[...])
    y_ref[...] = x1 + _rms(ff, gffn_ref[...])


def _post(x, oa, ob, ga, gb, wo, gpost, gpre, wg, wu, wd, gffn, *, tm):
    n_tok, d_model = x.shape
    row = lambda w: pl.BlockSpec((tm, w), lambda i: (i, 0))
    return pl.pallas_call(
        _post_kernel,
        grid=(n_tok // tm,),
        in_specs=[
            row(d_model), row(WIDTH), row(WIDTH),
            _resident(ga.shape), _resident(gb.shape), _resident(wo.shape),
            _resident(gpost.shape), _resident(gpre.shape),
            _resident(wg.shape), _resident(wu.shape), _resident(wd.shape), _resident(gffn.shape),
        ],
        out_specs=row(d_model),
        out_shape=jax.ShapeDtypeStruct((n_tok, d_model), F32),
        compiler_params=_params(),
        name="post",
    )(x, oa, ob, ga, gb, wo, gpost, gpre, wg, wu, wd, gffn)


def _row(v):
    return v.reshape(1, -1).astype(F32)


def _state_from_feature_major(t, batch, n):
    return t.reshape(batch, N_HEADS, HEAD_DIM, n).transpose(0, 3, 1, 2)


def _cache_feature_major(c):
    b, n = c.shape[0], c.shape[1]
    return c.transpose(0, 2, 3, 1).reshape(b, WIDTH, n)


def kernel(x_prompt, x_sample, cache_a_k, cache_a_v, cache_a_logf, cache_b_k, cache_b_v, norm_mix_pre, w_in, b_forget, rel_bias, gain_out_a, gain_out_b, w_out, norm_mix_post, norm_ffn_pre, w_gate, w_up, w_down, norm_ffn_post):
    batch, seq, d_model = x_prompt.shape
    dec_batch, dec_seq, _ = x_sample.shape
    depth = w_in.shape[0]
    past = cache_a_k.shape[2]
    n_band = cache_b_k.shape[2]
    keep = min(LEFT_REACH, seq)
    assert seq % Q_TILE == 0 and keep == LEFT_REACH and n_band == LEFT_REACH and dec_seq <= CHUNK
    n_tok_p, n_tok_s = batch * seq, dec_batch * dec_seq
    tm_p = keep

    xp = x_prompt.reshape(n_tok_p, d_model)
    xs = x_sample.reshape(n_tok_s, d_model)
    outs = [[] for _ in range(10)]
    fa0, fa1 = 3 * WIDTH, 3 * WIDTH + N_HEADS
    for l in range(depth):
        wl = w_in[l]
        qa_w, ka_w, va_w = wl[:, 0:WIDTH], wl[:, WIDTH:2 * WIDTH], wl[:, 2 * WIDTH:fa0]
        f_w = wl[:, fa0:fa1]
        qb_w, kb_w, vb_w = wl[:, fa1:fa1 + WIDTH], wl[:, fa1 + WIDTH:fa1 + 2 * WIDTH], wl[:, fa1 + 2 * WIDTH:]
        w_cat = jnp.concatenate(
            [qa_w, ka_w, va_w, qb_w, kb_w, vb_w, jnp.pad(f_w, ((0, 0), (0, PAIR - N_HEADS)))], axis=1).astype(BF16)
        w_q = jnp.concatenate([qa_w, qb_w], axis=1).astype(BF16)
        w_t = jnp.concatenate(
            [ka_w, va_w, kb_w, vb_w, jnp.pad(f_w, ((0, 0), (0, N_HEADS)))], axis=1).T.astype(BF16)
        b_col = b_forget[l].reshape(N_HEADS, 1).astype(F32)
        rb_pad = jnp.pad(rel_bias[l].astype(F32), ((0, 16 - N_HEADS), (0, 384 - (2 * REL_CLIP + 1))))
        tail = (_row(gain_out_a[l]), _row(gain_out_b[l]), w_out[l].astype(BF16), _row(norm_mix_post[l]),
                _row(norm_ffn_pre[l]), w_gate[l].astype(BF16), w_up[l].astype(BF16), w_down[l].astype(BF16),
                _row(norm_ffn_post[l]))
        tabm, tabr = _bias_tables(rb_pad, n_new=dec_seq)

        qa, qb, kat, vat, kbt, vbt, kat32, vat32, logf_t, kbt32, vbt32 = _inproj_t(
            xp, _row(norm_mix_pre[l]), w_q, w_t, b_col, batch=batch, seq=seq, tm=tm_p)
        crow, ccol = _cumsum(logf_t, batch=batch, seq=seq)
        oa = _fox(qa, kat, vat, crow.reshape(N_PAIRS, 2, n_tok_p), ccol, batch=batch, seq=seq)
        ob = _band(qb, kbt, vbt, tabm, batch=batch, seq=seq)
        xp = _post(xp, oa, ob, *tail, tm=256)
        outs[0].append(_state_from_feature_major(kat32, batch, seq))
        outs[1].append(_state_from_feature_major(vat32, batch, seq))
        outs[2].append(logf_t.T.reshape(batch, seq, N_HEADS))
        outs[3].append(_state_from_feature_major(kbt32, batch, keep))
        outs[4].append(_state_from_feature_major(vbt32, batch, keep))

        qa, ka, va, qb, kb, vb, ka32, va32, logf_t, kb32, vb32 = _inproj(xs, _row(norm_mix_pre[l]), w_cat, b_col)
        clf = cache_a_logf[l].astype(F32).transpose(0, 2, 1).reshape(dec_batch, N_PAIRS, 2, past)
        nlf = logf_t.reshape(N_PAIRS, 2, dec_batch, dec_seq).transpose(2, 0, 1, 3)
        nlf = jnp.pad(nlf, ((0, 0), (0, 0), (0, 0), (0, NEW_PAD - dec_seq)))
        oa, ob = _sample_attn(
            qa, ka, va, qb, kb, vb,
            _cache_feature_major(cache_a_k[l]), _cache_feature_major(cache_a_v[l]),
            _cache_feature_major(cache_b_k[l]), _cache_feature_major(cache_b_v[l]),
            clf, nlf, tabr, batch=dec_batch, n_q=dec_seq)
        xs = _post(xs, oa, ob, *tail, tm=n_tok_s)
        outs[5].append(ka32.reshape(dec_batch, dec_seq, N_HEADS, HEAD_DIM))
        outs[6].append(va32.reshape(dec_batch, dec_seq, N_HEADS, HEAD_DIM))
        outs[7].append(logf_t.T.reshape(dec_batch, dec_seq, N_HEADS))
        outs[8].append(kb32.reshape(dec_batch, dec_seq, N_HEADS, HEAD_DIM))
        outs[9].append(vb32.reshape(dec_batch, dec_seq, N_HEADS, HEAD_DIM))

    st = jnp.stack
    return (xp.reshape(batch, seq, d_model), xs.reshape(dec_batch, dec_seq, d_model),
            *[st(o) for o in outs])
```

```python
import functools

import jax
import jax.numpy as jnp
from jax import lax
from jax.experimental import pallas as pl
from jax.experimental.pallas import tpu as pltpu

F32 = jnp.float32
BF16 = jnp.bfloat16

HEAD_DIM = 64
N_HEADS = 8
WIDTH = N_HEADS * HEAD_DIM
PAIR = 2 * HEAD_DIM
N_PAIRS = N_HEADS // 2
CHUNK = 64
LEFT_CHUNKS = 8
LEFT_REACH = LEFT_CHUNKS * CHUNK
REL_CLIP = 128
EPS = 1e-6
SCALE = HEAD_DIM ** -0.5
LOG2E = 1.4426950408889634
Q_TILE = 256
BAND_KEYS = LEFT_REACH + Q_TILE
TOEPLITZ = 1024
MASKED = -1e30
NEW_PAD = 128
VMEM_LIMIT = 56 * 1024 * 1024


def _params():
    return pltpu.CompilerParams(vmem_limit_bytes=VMEM_LIMIT)


def _resident(shape):
    return pl.BlockSpec(shape, lambda *_: (0,) * len(shape), pipeline_mode=pl.Buffered(1))


def _rms(x, g):
    ms = jnp.mean(x * x, axis=-1, keepdims=True)
    return x * lax.rsqrt(ms + EPS) * g


def _log_sigmoid(x):
    return jnp.minimum(x, 0.0) - jnp.log1p(jnp.exp(-jnp.abs(x)))


def _dot(a, b):
    return jnp.dot(a, b, preferred_element_type=F32)


def _dot_nt(a, b):
    return lax.dot_general(a, b, (((1,), (1,)), ((), ())), preferred_element_type=F32)


def _inproj_kernel(x_ref, g_ref, w_ref, bf_ref,
                   qa_ref, ka_ref, va_ref, qb_ref, kb_ref, vb_ref,
                   ka32_ref, va32_ref, logf_ref, kb32_ref, vb32_ref):
    h = _rms(x_ref[...], g_ref[...]).astype(BF16)

    def proj(c):
        return _dot(h, w_ref[:, c * WIDTH:(c + 1) * WIDTH])

    qa_ref[...] = (proj(0) * SCALE).astype(BF16)
    z = proj(1)
    ka32_ref[...] = z
    ka_ref[...] = z.astype(BF16)
    z = proj(2)
    va32_ref[...] = z
    va_ref[...] = z.astype(BF16)
    qb_ref[...] = (proj(3) * SCALE).astype(BF16)
    z = proj(4)
    kb32_ref[...] = z
    kb_ref[...] = z.astype(BF16)
    z = proj(5)
    vb32_ref[...] = z
    vb_ref[...] = z.astype(BF16)
    f = _dot(h, w_ref[:, 6 * WIDTH:6 * WIDTH + PAIR])
    logf_ref[...] = _log_sigmoid(f.T[0:N_HEADS, :] + bf_ref[...])


def _inproj(x, gain, w_cat, b_col):
    n_tok, d_model = x.shape
    full = lambda w: pl.BlockSpec((n_tok, w), lambda i: (0, 0))
    out_shape = (
        [jax.ShapeDtypeStruct((n_tok, WIDTH), BF16)] * 6
        + [jax.ShapeDtypeStruct((n_tok, WIDTH), F32)] * 2
        + [jax.ShapeDtypeStruct((N_HEADS, n_tok), F32)]
        + [jax.ShapeDtypeStruct((n_tok, WIDTH), F32)] * 2
    )
    out_specs = ([full(WIDTH) for _ in range(8)]
                 + [pl.BlockSpec((N_HEADS, n_tok), lambda i: (0, 0)), full(WIDTH), full(WIDTH)])
    return pl.pallas_call(
        _inproj_kernel,
        grid=(1,),
        in_specs=[full(d_model), _resident((1, d_model)), _resident(w_cat.shape), _resident((N_HEADS, 1))],
        out_specs=out_specs,
        out_shape=out_shape,
        compiler_params=_params(),
        name="inproj_sample",
    )(x, gain, w_cat, b_col)


def _inproj_t_kernel(x_ref, g_ref, wq_ref, wt_ref, bf_ref,
                     qa_ref, qb_ref, kat_ref, vat_ref, kbt_ref, vbt_ref,
                     kat32_ref, vat32_ref, logf_ref, kbt32_ref, vbt32_ref, *, tiles_per_seq):
    h = _rms(x_ref[...], g_ref[...]).astype(BF16)
    zq = _dot(h, wq_ref[...])
    qa_ref[...] = (zq[:, 0:WIDTH] * (SCALE * LOG2E)).astype(BF16)
    qb_ref[...] = (zq[:, WIDTH:2 * WIDTH] * (SCALE * LOG2E)).astype(BF16)

    def proj_t(c, rows=WIDTH):
        return _dot_nt(wt_ref[c * WIDTH:c * WIDTH + rows, :], h)

    z = proj_t(0)
    kat32_ref[0] = z
    kat_ref[0] = z.astype(BF16)
    z = proj_t(1)
    vat32_ref[0] = z
    vat_ref[0] = z.astype(BF16)
    zk = proj_t(2)
    kbt_ref[0] = zk.astype(BF16)
    zv = proj_t(3, WIDTH + 2 * N_HEADS)
    vbt_ref[0] = zv[0:WIDTH].astype(BF16)
    logf_ref[...] = _log_sigmoid(zv[WIDTH:WIDTH + N_HEADS] + bf_ref[...])

    @pl.when(pl.program_id(0) % tiles_per_seq == tiles_per_seq - 1)
    def _():
        kbt32_ref[0] = zk
        vbt32_ref[0] = zv[0:WIDTH]


def _inproj_t(x, gain, wq, wt, b_col, *, batch, seq, tm):
    n_tok, d_model = x.shape
    tps = seq // tm
    tok = lambda: pl.BlockSpec((tm, WIDTH), lambda i: (i, 0))
    feat = lambda: pl.BlockSpec((1, WIDTH, tm), lambda i: (i // tps, 0, i % tps))
    keep = lambda: pl.BlockSpec((1, WIDTH, tm), lambda i: (i // tps, 0, 0))
    out_shape = (
        [jax.ShapeDtypeStruct((n_tok, WIDTH), BF16)] * 2
        + [jax.ShapeDtypeStruct((batch, WIDTH, seq), BF16)] * 4
        + [jax.ShapeDtypeStruct((batch, WIDTH, seq), F32)] * 2
        + [jax.ShapeDtypeStruct((N_HEADS, n_tok), F32)]
        + [jax.ShapeDtypeStruct((batch, WIDTH, tm), F32)] * 2
    )
    out_specs = ([tok(), tok()] + [feat() for _ in range(6)]
                 + [pl.BlockSpec((N_HEADS, tm), lambda i: (0, i)), keep(), keep()])
    return pl.pallas_call(
        functools.partial(_inproj_t_kernel, tiles_per_seq=tps),
        grid=(n_tok // tm,),
        in_specs=[
            pl.BlockSpec((tm, d_model), lambda i: (i, 0)),
            _resident((1, d_model)),
            _resident(wq.shape),
            _resident(wt.shape),
            _resident((N_HEADS, 1)),
        ],
        out_specs=out_specs,
        out_shape=out_shape,
        compiler_params=_params(),
        name="inproj",
    )(x, gain, wq, wt, b_col)


def _lane_cumsum(x):
    n = x.shape[-1]
    lane = lax.broadcasted_iota(jnp.int32, x.shape, x.ndim - 1)
    shift = 1
    while shift < n:
        x = x + jnp.where(lane >= shift, pltpu.roll(x, shift, axis=x.ndim - 1), 0.0)
        shift *= 2
    return x


def _cumsum_kernel(lf_ref, crow_ref, ccol_ref):
    c = _lane_cumsum(lf_ref[...])
    crow_ref[...] = c
    seq = c.shape[1]
    ct = jnp.concatenate([c, jnp.zeros((PAIR - N_HEADS, seq), F32)], axis=0).T
    for p in range(N_PAIRS):
        ccol_ref[p] = ct[:, 2 * p:2 * p + 2]


def _cumsum(logf_t, *, batch, seq):
    n_tok = batch * seq
    return pl.pallas_call(
        _cumsum_kernel,
        grid=(batch,),
        in_specs=[pl.BlockSpec((N_HEADS, seq), lambda b: (0, b))],
        out_specs=[
            pl.BlockSpec((N_HEADS, seq), lambda b: (0, b)),
            pl.BlockSpec((N_PAIRS, seq, 2), lambda b: (0, b, 0)),
        ],
        out_shape=[
            jax.ShapeDtypeStruct((N_HEADS, n_tok), F32),
            jax.ShapeDtypeStruct((N_PAIRS, n_tok, 2), F32),
        ],
        compiler_params=_params(),
        name="cumsum",
    )(logf_t)


def _split_heads_q(q):
    lane = lax.broadcasted_iota(jnp.int32, q.shape, 1)
    zero = jnp.zeros_like(q)
    return jnp.concatenate([jnp.where(lane < HEAD_DIM, q, zero),
                            jnp.where(lane >= HEAD_DIM, q, zero)], axis=0)


def _augment_v(v, h, axis):
    idx = lax.broadcasted_iota(jnp.int32, v.shape, axis)
    own = (idx < HEAD_DIM) if h == 0 else (idx >= HEAD_DIM)
    return jnp.where(own, v, jnp.ones_like(v))


def _merge_heads_out(acc0, acc1):
    lane = lax.broadcasted_iota(jnp.int32, acc0.shape, 1)
    first = lane < HEAD_DIM
    num = jnp.where(first, acc0, acc1)
    den = pltpu.roll(jnp.where(first, acc1, acc0), HEAD_DIM, axis=1)
    return num / den


def _split3(x):
    hi = x.astype(BF16).astype(F32)
    rest = x - hi
    mid = rest.astype(BF16).astype(F32)
    return hi, mid, rest - mid


def _fox_kernel(q_ref, kt_ref, vt_ref, crow_ref, ccol_ref, o_ref, v0_ref, v1_ref, kx_ref, *, seq):
    vt = vt_ref[0]
    v0_ref[...] = _augment_v(vt, 0, 0)
    v1_ref[...] = _augment_v(vt, 1, 0)
    vaug = (v0_ref, v1_ref)
    row = lax.broadcasted_iota(jnp.int32, (Q_TILE, Q_TILE), 0)
    col = lax.broadcasted_iota(jnp.int32, (Q_TILE, Q_TILE), 1)
    causal = col <= row

    n_ext = 16
    er = lax.broadcasted_iota(jnp.int32, (n_ext, seq), 0)
    ext = jnp.zeros((n_ext, seq), F32)
    for h in range(2):
        for j, piece in enumerate(_split3(crow_ref[0, h:h + 1, :] * LOG2E)):
            ext = jnp.where(er == 3 * h + j, piece, ext)
    kx_ref[0:PAIR, :] = kt_ref[0]
    kx_ref[PAIR:PAIR + n_ext, :] = ext.astype(BF16)
    kx_ref[PAIR + n_ext:2 * PAIR, :] = jnp.zeros((PAIR - n_ext, seq), BF16)
    cr = lax.broadcasted_iota(jnp.int32, (2 * Q_TILE, PAIR), 0)
    cl = lax.broadcasted_iota(jnp.int32, (2 * Q_TILE, PAIR), 1)
    first = jnp.where(cr >= Q_TILE, 3, 0)
    coef = jnp.where((cl >= first) & (cl < first + 3), -1.0, 0.0).astype(BF16)

    n_qb = seq // Q_TILE

    def scores(qb):
        qs, ke = qb * Q_TILE, (qb + 1) * Q_TILE
        q2 = jnp.concatenate([_split_heads_q(q_ref[qs:ke, :]), coef], axis=1)
        return _dot(q2, kx_ref[:, 0:ke])

    s_next = scores(0)
    for qb in range(n_qb):
        qs, ke = qb * Q_TILE, (qb + 1) * Q_TILE
        s = s_next
        if qb + 1 < n_qb:
            s_next = scores(qb + 1)
        accs = []
        for h in range(2):
            cq = ccol_ref[0, qs:ke, h:h + 1] * LOG2E
            t = s[h * Q_TILE:(h + 1) * Q_TILE, :]
            t_diag = jnp.where(causal, t[:, qs:ke], -jnp.inf)
            m = jnp.max(t_diag, axis=1, keepdims=True)
            if qb > 0:
                t_past = t[:, 0:qs]
                m = jnp.maximum(m, jnp.max(t_past, axis=1, keepdims=True))
            shift = cq - (m + cq)
            acc = _dot_nt(jnp.exp2(t_diag + shift).astype(BF16), vaug[h][:, qs:ke])
            if qb > 0:
                acc = acc + _dot_nt(jnp.exp2(t_past + shift).astype(BF16), vaug[h][:, 0:qs])
            accs.append(acc)
        o_ref[qs:ke, :] = _merge_heads_out(accs[0], accs[1]).astype(o_ref.dtype)


def _fox(q, kt, vt, crow, ccol, *, batch, seq):
    n_tok = batch * seq
    tok = lambda: pl.BlockSpec((seq, PAIR), lambda b, p: (b, p))
    feat = lambda: pl.BlockSpec((1, PAIR, seq), lambda b, p: (b, p, 0))
    return pl.pallas_call(
        functools.partial(_fox_kernel, seq=seq),
        grid=(batch, N_PAIRS),
        in_specs=[
            tok(), feat(), feat(),
            pl.BlockSpec((1, 2, seq), lambda b, p: (p, 0, b)),
            pl.BlockSpec((1, seq, 2), lambda b, p: (p, b, 0)),
        ],
        out_specs=tok(),
        out_shape=jax.ShapeDtypeStruct((n_tok, WIDTH), BF16),
        scratch_shapes=[pltpu.VMEM((PAIR, seq), BF16), pltpu.VMEM((PAIR, seq), BF16),
                        pltpu.VMEM((2 * PAIR, seq), BF16)],
        compiler_params=_params(),
        name="fox",
    )(q, kt, vt, crow, ccol)


def _bias_kernel(rb_ref, tabm_ref, tabr_ref):
    rb = rb_ref[...]
    n_rows = rb.shape[1]
    r = lax.broadcasted_iota(jnp.int32, (n_rows, TOEPLITZ), 0)
    m = lax.broadcasted_iota(jnp.int32, (n_rows, TOEPLITZ), 1)
    e = jnp.where(m < TOEPLITZ // 2, m, m - TOEPLITZ)
    idx = jnp.clip(LEFT_REACH - Q_TILE - e, -REL_CLIP, REL_CLIP) + REL_CLIP
    onehot = jnp.where(r == idx, 1.0, 0.0).astype(BF16)
    hi = rb.astype(BF16)
    rest = rb - hi.astype(F32)
    mid = rest.astype(BF16)
    lo = (rest - mid.astype(F32)).astype(BF16)
    u = _dot(hi, onehot) + _dot(mid, onehot) + _dot(lo, onehot)

    i = lax.broadcasted_iota(jnp.int32, (Q_TILE, BAND_KEYS), 0)
    j = lax.broadcasted_iota(jnp.int32, (Q_TILE, BAND_KEYS), 1)
    qc = jnp.right_shift(i, 6)
    kc = jnp.right_shift(j, 6)
    valid = (kc >= qc) & (kc <= qc + LEFT_CHUNKS)
    n_new = tabr_ref.shape[1]
    for h in range(N_HEADS):
        base = jnp.broadcast_to(u[h:h + 1, :], (Q_TILE, TOEPLITZ))
        t = pltpu.roll(base, Q_TILE, axis=1, stride=1, stride_axis=0)[:, 0:BAND_KEYS]
        tabm_ref[h] = jnp.where(valid, t * LOG2E, MASKED)
        tabr_ref[h] = t[0:n_new, :]


def _bias_tables(rb_pad, *, n_new):
    return pl.pallas_call(
        _bias_kernel,
        out_shape=[
            jax.ShapeDtypeStruct((N_HEADS, Q_TILE, BAND_KEYS), F32),
            jax.ShapeDtypeStruct((N_HEADS, n_new, BAND_KEYS), F32),
        ],
        compiler_params=_params(),
        name="bias_tables",
    )(rb_pad)


def _band_kernel(q_ref, kt_ref, vt_ref, tab_ref, o_ref, v0_ref, v1_ref, *, seq):
    vt = vt_ref[0]
    v0_ref[...] = _augment_v(vt, 0, 0)
    v1_ref[...] = _augment_v(vt, 1, 0)
    vaug = (v0_ref, v1_ref)

    n_qb = seq // Q_TILE

    def scores(qb):
        qs, ke = qb * Q_TILE, (qb + 1) * Q_TILE
        ks = max(0, qs - LEFT_REACH)
        return _dot(_split_heads_q(q_ref[qs:ke, :]), kt_ref[0, :, ks:ke])

    s_next = scores(0)
    for qb in range(n_qb):
        qs, ke = qb * Q_TILE, (qb + 1) * Q_TILE
        ks = max(0, qs - LEFT_REACH)
        s = s_next
        if qb + 1 < n_qb:
            s_next = scores(qb + 1)
        accs = []
        for h in range(2):
            t = s[h * Q_TILE:(h + 1) * Q_TILE, :] + tab_ref[h, :, BAND_KEYS - (ke - ks):BAND_KEYS]
            m = jnp.max(t, axis=1, keepdims=True)
            accs.append(_dot_nt(jnp.exp2(t - m).astype(BF16), vaug[h][:, ks:ke]))
        o_ref[qs:ke, :] = _merge_heads_out(accs[0], accs[1]).astype(o_ref.dtype)


def _band(q, kt, vt, tabm, *, batch, seq):
    n_tok = batch * seq
    tok = lambda: pl.BlockSpec((seq, PAIR), lambda p, b: (b, p))
    feat = lambda: pl.BlockSpec((1, PAIR, seq), lambda p, b: (b, p, 0))
    return pl.pallas_call(
        functools.partial(_band_kernel, seq=seq),
        grid=(N_PAIRS, batch),
        in_specs=[
            tok(), feat(), feat(),
            pl.BlockSpec((2, Q_TILE, BAND_KEYS), lambda p, b: (p, 0, 0)),
        ],
        out_specs=tok(),
        out_shape=jax.ShapeDtypeStruct((n_tok, WIDTH), BF16),
        scratch_shapes=[pltpu.VMEM((PAIR, seq), BF16), pltpu.VMEM((PAIR, seq), BF16)],
        compiler_params=_params(),
        name="band",
    )(q, kt, vt, tabm)


def _two_part_attention(q2, kt_cache, vt_cache, kn_ref, vn_ref, bias_cache, bias_new, n_q):
    s_c = _dot(q2, kt_cache)
    s_n = _dot_nt(q2, kn_ref[...])
    accs = []
    for h in range(2):
        rows = slice(h * n_q, (h + 1) * n_q)
        t_c = s_c[rows, :] + bias_cache(h)
        t_n = s_n[rows, :] + bias_new(h)
        m = jnp.maximum(jnp.max(t_c, axis=1, keepdims=True), jnp.max(t_n, axis=1, keepdims=True))
        acc = _dot_nt(jnp.exp(t_c - m).astype(BF16), _augment_v(vt_cache, h, 0))
        acc = acc + _dot(jnp.exp(t_n - m).astype(BF16), _augment_v(vn_ref[...], h, 1))
        accs.append(acc)
    return _merge_heads_out(accs[0], accs[1])


def _sample_attn_kernel(qa_ref, ka_ref, va_ref, qb_ref, kb_ref, vb_ref,
                        cak_ref, cav_ref, cbk_ref, cbv_ref, clf_ref, nlf_ref, tab_ref,
                        oa_ref, ob_ref, kn_ref, vn_ref, *, n_q):
    row = lax.broadcasted_iota(jnp.int32, (n_q, NEW_PAD), 0)
    col = lax.broadcasted_iota(jnp.int32, (n_q, NEW_PAD), 1)

    def stage_new(k_new, v_new):
        kn_ref[...] = jnp.zeros_like(kn_ref)
        vn_ref[...] = jnp.zeros_like(vn_ref)
        kn_ref[0:n_q, :] = k_new
        vn_ref[0:n_q, :] = v_new

    c_cache = _lane_cumsum(clf_ref[0, 0])
    n_cache = c_cache.shape[1]
    c_new = _lane_cumsum(nlf_ref[0, 0]) + c_cache[:, n_cache - 1:n_cache]

    def fox_cache(h):
        cq = jnp.sum(jnp.where(col == row, c_new[h:h + 1, :], 0.0), axis=1, keepdims=True)
        return cq - c_cache[h:h + 1, :]

    def fox_new(h):
        cq = jnp.sum(jnp.where(col == row, c_new[h:h + 1, :], 0.0), axis=1, keepdims=True)
        return jnp.where(col <= row, cq - c_new[h:h + 1, :], -jnp.inf)

    stage_new(ka_ref[...], va_ref[...])
    oa = _two_part_attention(_split_heads_q(qa_ref[...]), cak_ref[0].astype(BF16), cav_ref[0].astype(BF16),
                             kn_ref, vn_ref, fox_cache, fox_new, n_q)
    oa_ref[...] = oa.astype(oa_ref.dtype)

    n_band = cbk_ref.shape[2]

    def band_cache(h):
        return tab_ref[h, :, 0:n_band]

    def band_new(h):
        return jnp.where(col < n_q, tab_ref[h, :, n_band:n_band + NEW_PAD], -jnp.inf)

    stage_new(kb_ref[...], vb_ref[...])
    ob = _two_part_attention(_split_heads_q(qb_ref[...]), cbk_ref[0].astype(BF16), cbv_ref[0].astype(BF16),
                             kn_ref, vn_ref, band_cache, band_new, n_q)
    ob_ref[...] = ob.astype(ob_ref.dtype)


def _sample_attn(qa, ka, va, qb, kb, vb, cakt, cavt, cbkt, cbvt, clf, nlf, tabr, *, batch, n_q):
    n_tok = batch * n_q
    past = cakt.shape[2]
    n_band = cbkt.shape[2]
    new = lambda: pl.BlockSpec((n_q, PAIR), lambda b, p: (b, p))
    cache = lambda n: pl.BlockSpec((1, PAIR, n), lambda b, p: (b, p, 0))
    return pl.pallas_call(
        functools.partial(_sample_attn_kernel, n_q=n_q),
        grid=(batch, N_PAIRS),
        in_specs=[
            new(), new(), new(), new(), new(), new(),
            cache(past), cache(past), cache(n_band), cache(n_band),
            pl.BlockSpec((1, 1, 2, past), lambda b, p: (b, p, 0, 0)),
            pl.BlockSpec((1, 1, 2, NEW_PAD), lambda b, p: (b, p, 0, 0)),
            pl.BlockSpec((2, n_q, BAND_KEYS), lambda b, p: (p, 0, 0)),
        ],
        out_specs=[new(), new()],
        out_shape=[jax.ShapeDtypeStruct((n_tok, WIDTH), BF16)] * 2,
        scratch_shapes=[pltpu.VMEM((NEW_PAD, PAIR), BF16), pltpu.VMEM((NEW_PAD, PAIR), BF16)],
        compiler_params=_params(),
        name="sample_attn",
    )(qa, ka, va, qb, kb, vb, cakt, cavt, cbkt, cbvt, clf, nlf, tabr)


def _post_kernel(x_ref, oa_ref, ob_ref, ga_ref, gb_ref, wo_ref, gpost_ref, gpre_ref,
                 wg_ref, wu_ref, wd_ref, gffn_ref, y_ref, *, n_sub):
    sub = x_ref.shape[0] // n_sub
    rows = [slice(i * sub, (i + 1) * sub) for i in range(n_sub)]
    st = [dict() for _ in range(n_sub)]

    def norm(i):
        st[i]["na"] = _rms(oa_ref[rows[i], :].astype(F32), ga_ref[...]).astype(BF16)
        st[i]["nb"] = _rms(ob_ref[rows[i], :].astype(F32), gb_ref[...]).astype(BF16)

    def mix(i):
        st[i]["mix"] = _dot(st[i]["na"], wo_ref[0:WIDTH, :]) + _dot(st[i]["nb"], wo_ref[WIDTH:2 * WIDTH, :])

    def mid(i):
        x1 = x_ref[rows[i], :] + _rms(st[i]["mix"], gpost_ref[...])
        st[i]["x1"] = x1
        st[i]["hf"] = _rms(x1, gpre_ref[...]).astype(BF16)

    def gate_up(i):
        st[i]["g"] = _dot(st[i]["hf"], wg_ref[...])
        st[i]["u"] = _dot(st[i]["hf"], wu_ref[...])

    def silu(i):
        g = st[i]["g"]
        st[i]["a"] = (g * jax.nn.sigmoid(g) * st[i]["u"]).astype(BF16)

    def down(i):
        st[i]["ff"] = _dot(st[i]["a"], wd_ref[...])

    def fin(i):
        y_ref[rows[i], :] = st[i]["x1"] + _rms(st[i]["ff"], gffn_ref[...])

    vpu_stages = [norm, mid, silu, fin]
    mxu_stages = [mix, gate_up, down]
    norm(0)
    for k, mxu in enumerate(mxu_stages):
        for i in range(n_sub):
            mxu(i)
            if i + 1 < n_sub:
                vpu_stages[k](i + 1)
            else:
                vpu_stages[k + 1](0)
    for i in range(1, n_sub):
        fin(i)


def _post(x, oa, ob, ga, gb, wo, gpost, gpre, wg, wu, wd, gffn, *, tm, n_sub):
    n_tok, d_model = x.shape
    row = lambda w: pl.BlockSpec((tm, w), lambda i: (i, 0))
    return pl.pallas_call(
        functools.partial(_post_kernel, n_sub=n_sub),
        grid=(n_tok // tm,),
        in_specs=[
            row(d_model), row(WIDTH), row(WIDTH),
            _resident(ga.shape), _resident(gb.shape), _resident(wo.shape),
            _resident(gpost.shape), _resident(gpre.shape),
            _resident(wg.shape), _resident(wu.shape), _resident(wd.shape), _resident(gffn.shape),
        ],
        out_specs=row(d_model),
        out_shape=jax.ShapeDtypeStruct((n_tok, d_model), F32),
        compiler_params=_params(),
        name="post",
    )(x, oa, ob, ga, gb, wo, gpost, gpre, wg, wu, wd, gffn)


def _row(v):
    return v.reshape(1, -1).astype(F32)


def _state_from_feature_major(t, batch, n):
    return t.reshape(batch, N_HEADS, HEAD_DIM, n).transpose(0, 3, 1, 2)


def _cache_feature_major(c):
    b, n = c.shape[0], c.shape[1]
    return c.transpose(0, 2, 3, 1).reshape(b, WIDTH, n)


def kernel(x_prompt, x_sample, cache_a_k, cache_a_v, cache_a_logf, cache_b_k, cache_b_v, norm_mix_pre, w_in, b_forget, rel_bias, gain_out_a, gain_out_b, w_out, norm_mix_post, norm_ffn_pre, w_gate, w_up, w_down, norm_ffn_post):
    batch, seq, d_model = x_prompt.shape
    dec_batch, dec_seq, _ = x_sample.shape
    depth = w_in.shape[0]
    past = cache_a_k.shape[2]
    n_band = cache_b_k.shape[2]
    keep = min(LEFT_REACH, seq)
    assert seq % Q_TILE == 0 and keep == LEFT_REACH and n_band == LEFT_REACH and dec_seq <= CHUNK
    n_tok_p, n_tok_s = batch * seq, dec_batch * dec_seq
    tm_p = keep

    xp = x_prompt.reshape(n_tok_p, d_model)
    xs = x_sample.reshape(n_tok_s, d_model)
    outs = [[] for _ in range(10)]
    fa0, fa1 = 3 * WIDTH, 3 * WIDTH + N_HEADS
    for l in range(depth):
        wl = w_in[l]
        qa_w, ka_w, va_w = wl[:, 0:WIDTH], wl[:, WIDTH:2 * WIDTH], wl[:, 2 * WIDTH:fa0]
        f_w = wl[:, fa0:fa1]
        qb_w, kb_w, vb_w = wl[:, fa1:fa1 + WIDTH], wl[:, fa1 + WIDTH:fa1 + 2 * WIDTH], wl[:, fa1 + 2 * WIDTH:]
        w_cat = jnp.concatenate(
            [qa_w, ka_w, va_w, qb_w, kb_w, vb_w, jnp.pad(f_w, ((0, 0), (0, PAIR - N_HEADS)))], axis=1).astype(BF16)
        w_q = jnp.concatenate([qa_w, qb_w], axis=1).astype(BF16)
        w_t = jnp.concatenate(
            [ka_w, va_w, kb_w, vb_w, jnp.pad(f_w, ((0, 0), (0, N_HEADS)))], axis=1).T.astype(BF16)
        b_col = b_forget[l].reshape(N_HEADS, 1).astype(F32)
        rb_pad = jnp.pad(rel_bias[l].astype(F32), ((0, 16 - N_HEADS), (0, 384 - (2 * REL_CLIP + 1))))
        tail = (_row(gain_out_a[l]), _row(gain_out_b[l]), w_out[l].astype(BF16), _row(norm_mix_post[l]),
                _row(norm_ffn_pre[l]), w_gate[l].astype(BF16), w_up[l].astype(BF16), w_down[l].astype(BF16),
                _row(norm_ffn_post[l]))
        tabm, tabr = _bias_tables(rb_pad, n_new=dec_seq)

        qa, qb, kat, vat, kbt, vbt, kat32, vat32, logf_t, kbt32, vbt32 = _inproj_t(
            xp, _row(norm_mix_pre[l]), w_q, w_t, b_col, batch=batch, seq=seq, tm=tm_p)
        crow, ccol = _cumsum(logf_t, batch=batch, seq=seq)
        oa = _fox(qa, kat, vat, crow.reshape(N_PAIRS, 2, n_tok_p), ccol, batch=batch, seq=seq)
        ob = _band(qb, kbt, vbt, tabm, batch=batch, seq=seq)
        xp = _post(xp, oa, ob, *tail, tm=512, n_sub=2)
        outs[0].append(_state_from_feature_major(kat32, batch, seq))
        outs[1].append(_state_from_feature_major(vat32, batch, seq))
        outs[2].append(logf_t.T.reshape(batch, seq, N_HEADS))
        outs[3].append(_state_from_feature_major(kbt32, batch, keep))
        outs[4].append(_state_from_feature_major(vbt32, batch, keep))

        qa, ka, va, qb, kb, vb, ka32, va32, logf_t, kb32, vb32 = _inproj(xs, _row(norm_mix_pre[l]), w_cat, b_col)
        clf = cache_a_logf[l].astype(F32).transpose(0, 2, 1).reshape(dec_batch, N_PAIRS, 2, past)
        nlf = logf_t.reshape(N_PAIRS, 2, dec_batch, dec_seq).transpose(2, 0, 1, 3)
        nlf = jnp.pad(nlf, ((0, 0), (0, 0), (0, 0), (0, NEW_PAD - dec_seq)))
        oa, ob = _sample_attn(
            qa, ka, va, qb, kb, vb,
            _cache_feature_major(cache_a_k[l]), _cache_feature_major(cache_a_v[l]),
            _cache_feature_major(cache_b_k[l]), _cache_feature_major(cache_b_v[l]),
            clf, nlf, tabr, batch=dec_batch, n_q=dec_seq)
        xs = _post(xs, oa, ob, *tail, tm=n_tok_s, n_sub=1)
        outs[5].append(ka32.reshape(dec_batch, dec_seq, N_HEADS, HEAD_DIM))
        outs[6].append(va32.reshape(dec_batch, dec_seq, N_HEADS, HEAD_DIM))
        outs[7].append(logf_t.T.reshape(dec_batch, dec_seq, N_HEADS))
        outs[8].append(kb32.reshape(dec_batch, dec_seq, N_HEADS, HEAD_DIM))
        outs[9].append(vb32.reshape(dec_batch, dec_seq, N_HEADS, HEAD_DIM))

    st = jnp.stack
    return (xp.reshape(batch, seq, d_model), xs.reshape(dec_batch, dec_seq, d_model),
            *[st(o) for o in outs])
```

```python
import functools

import jax
import jax.numpy as jnp
from jax import lax
from jax.experimental import pallas as pl
from jax.experimental.pallas import tpu as pltpu

F32 = jnp.float32
BF16 = jnp.bfloat16

HEAD_DIM = 64
N_HEADS = 8
WIDTH = N_HEADS * HEAD_DIM
PAIR = 2 * HEAD_DIM
N_PAIRS = N_HEADS // 2
CHUNK = 64
LEFT_CHUNKS = 8
LEFT_REACH = LEFT_CHUNKS * CHUNK
REL_CLIP = 128
EPS = 1e-6
SCALE = HEAD_DIM ** -0.5
LOG2E = 1.4426950408889634
Q_TILE = 256
BAND_KEYS = LEFT_REACH + Q_TILE
TOEPLITZ = 1024
MASKED = -1e30
NEW_PAD = 128
VMEM_LIMIT = 56 * 1024 * 1024


def _params():
    return pltpu.CompilerParams(vmem_limit_bytes=VMEM_LIMIT)


def _resident(shape):
    return pl.BlockSpec(shape, lambda *_: (0,) * len(shape), pipeline_mode=pl.Buffered(1))


def _rms(x, g):
    ms = jnp.mean(x * x, axis=-1, keepdims=True)
    return x * lax.rsqrt(ms + EPS) * g


def _log_sigmoid(x):
    return jnp.minimum(x, 0.0) - jnp.log1p(jnp.exp(-jnp.abs(x)))


def _dot(a, b):
    return jnp.dot(a, b, preferred_element_type=F32)


def _dot_nt(a, b):
    return lax.dot_general(a, b, (((1,), (1,)), ((), ())), preferred_element_type=F32)


def _inproj_kernel(x_ref, g_ref, w_ref, bf_ref,
                   qa_ref, ka_ref, va_ref, qb_ref, kb_ref, vb_ref,
                   ka32_ref, va32_ref, logf_ref, kb32_ref, vb32_ref):
    h = _rms(x_ref[...], g_ref[...]).astype(BF16)

    def proj(c):
        return _dot(h, w_ref[:, c * WIDTH:(c + 1) * WIDTH])

    qa_ref[...] = (proj(0) * SCALE).astype(BF16)
    z = proj(1)
    ka32_ref[...] = z
    ka_ref[...] = z.astype(BF16)
    z = proj(2)
    va32_ref[...] = z
    va_ref[...] = z.astype(BF16)
    qb_ref[...] = (proj(3) * SCALE).astype(BF16)
    z = proj(4)
    kb32_ref[...] = z
    kb_ref[...] = z.astype(BF16)
    z = proj(5)
    vb32_ref[...] = z
    vb_ref[...] = z.astype(BF16)
    f = _dot(h, w_ref[:, 6 * WIDTH:6 * WIDTH + PAIR])
    logf_ref[...] = _log_sigmoid(f.T[0:N_HEADS, :] + bf_ref[...])


def _inproj(x, gain, w_cat, b_col):
    n_tok, d_model = x.shape
    full = lambda w: pl.BlockSpec((n_tok, w), lambda i: (0, 0))
    out_shape = (
        [jax.ShapeDtypeStruct((n_tok, WIDTH), BF16)] * 6
        + [jax.ShapeDtypeStruct((n_tok, WIDTH), F32)] * 2
        + [jax.ShapeDtypeStruct((N_HEADS, n_tok), F32)]
        + [jax.ShapeDtypeStruct((n_tok, WIDTH), F32)] * 2
    )
    out_specs = ([full(WIDTH) for _ in range(8)]
                 + [pl.BlockSpec((N_HEADS, n_tok), lambda i: (0, 0)), full(WIDTH), full(WIDTH)])
    return pl.pallas_call(
        _inproj_kernel,
        grid=(1,),
        in_specs=[full(d_model), _resident((1, d_model)), _resident(w_cat.shape), _resident((N_HEADS, 1))],
        out_specs=out_specs,
        out_shape=out_shape,
        compiler_params=_params(),
        name="inproj_sample",
    )(x, gain, w_cat, b_col)


def _inproj_t_kernel(x_ref, g_ref, wt_ref, bf_ref,
                     qat_ref, qbt_ref, kat_ref, vat_ref, kbt_ref, vbt_ref,
                     kat32_ref, vat32_ref, logf_ref, kbt32_ref, vbt32_ref, *, tiles_per_seq):
    h = _rms(x_ref[...], g_ref[...]).astype(BF16)

    def proj_t(c, rows=WIDTH):
        return _dot_nt(wt_ref[c * WIDTH:c * WIDTH + rows, :], h)

    z = proj_t(0)
    kat32_ref[0] = z
    kat_ref[0] = z.astype(BF16)
    z = proj_t(1)
    vat32_ref[0] = z
    vat_ref[0] = z.astype(BF16)
    zk = proj_t(2)
    kbt_ref[0] = zk.astype(BF16)
    zv = proj_t(3)
    vbt_ref[0] = zv.astype(BF16)
    qat_ref[0] = (proj_t(4) * (SCALE * LOG2E)).astype(BF16)
    zq = proj_t(5, WIDTH + 2 * N_HEADS)
    qbt_ref[0] = (zq[0:WIDTH] * (SCALE * LOG2E)).astype(BF16)
    logf_ref[...] = _log_sigmoid(zq[WIDTH:WIDTH + N_HEADS] + bf_ref[...])

    @pl.when(pl.program_id(0) % tiles_per_seq == tiles_per_seq - 1)
    def _():
        kbt32_ref[0] = zk
        vbt32_ref[0] = zv


def _inproj_t(x, gain, wt, b_col, *, batch, seq, tm):
    n_tok, d_model = x.shape
    tps = seq // tm
    feat = lambda: pl.BlockSpec((1, WIDTH, tm), lambda i: (i // tps, 0, i % tps))
    keep = lambda: pl.BlockSpec((1, WIDTH, tm), lambda i: (i // tps, 0, 0))
    out_shape = (
        [jax.ShapeDtypeStruct((batch, WIDTH, seq), BF16)] * 6
        + [jax.ShapeDtypeStruct((batch, WIDTH, seq), F32)] * 2
        + [jax.ShapeDtypeStruct((N_HEADS, n_tok), F32)]
        + [jax.ShapeDtypeStruct((batch, WIDTH, tm), F32)] * 2
    )
    out_specs = ([feat() for _ in range(8)]
                 + [pl.BlockSpec((N_HEADS, tm), lambda i: (0, i)), keep(), keep()])
    return pl.pallas_call(
        functools.partial(_inproj_t_kernel, tiles_per_seq=tps),
        grid=(n_tok // tm,),
        in_specs=[
            pl.BlockSpec((tm, d_model), lambda i: (i, 0)),
            _resident((1, d_model)),
            _resident(wt.shape),
            _resident((N_HEADS, 1)),
        ],
        out_specs=out_specs,
        out_shape=out_shape,
        compiler_params=_params(),
        name="inproj",
    )(x, gain, wt, b_col)


def _lane_cumsum(x):
    n = x.shape[-1]
    lane = lax.broadcasted_iota(jnp.int32, x.shape, x.ndim - 1)
    shift = 1
    while shift < n:
        x = x + jnp.where(lane >= shift, pltpu.roll(x, shift, axis=x.ndim - 1), 0.0)
        shift *= 2
    return x


def _split3(x):
    hi = x.astype(BF16).astype(F32)
    rest = x - hi
    mid = rest.astype(BF16).astype(F32)
    return hi, mid, rest - mid


def _cumsum_kernel(lf_ref, c2_ref, kf_ref):
    c2 = _lane_cumsum(lf_ref[...]) * LOG2E
    c2_ref[...] = c2
    seq = c2.shape[1]
    hi, mid, lo = _split3(c2)
    feat = jnp.concatenate([hi, mid, lo, jnp.zeros((PAIR - 3 * N_HEADS, seq), F32)], axis=0)
    kf_ref[0] = feat.T.astype(BF16)


def _cumsum(logf_t, *, batch, seq):
    n_tok = batch * seq
    return pl.pallas_call(
        _cumsum_kernel,
        grid=(batch,),
        in_specs=[pl.BlockSpec((N_HEADS, seq), lambda b: (0, b))],
        out_specs=[
            pl.BlockSpec((N_HEADS, seq), lambda b: (0, b)),
            pl.BlockSpec((1, seq, PAIR), lambda b: (b, 0, 0)),
        ],
        out_shape=[
            jax.ShapeDtypeStruct((N_HEADS, n_tok), F32),
            jax.ShapeDtypeStruct((batch, seq, PAIR), BF16),
        ],
        compiler_params=_params(),
        name="cumsum",
    )(logf_t)


def _split_heads_q(q):
    lane = lax.broadcasted_iota(jnp.int32, q.shape, 1)
    zero = jnp.zeros_like(q)
    return jnp.concatenate([jnp.where(lane < HEAD_DIM, q, zero),
                            jnp.where(lane >= HEAD_DIM, q, zero)], axis=0)


def _augment_v(v, h, axis):
    idx = lax.broadcasted_iota(jnp.int32, v.shape, axis)
    own = (idx < HEAD_DIM) if h == 0 else (idx >= HEAD_DIM)
    return jnp.where(own, v, jnp.ones_like(v))


def _merge_heads_out(acc0, acc1):
    lane = lax.broadcasted_iota(jnp.int32, acc0.shape, 1)
    first = lane < HEAD_DIM
    num = jnp.where(first, acc0, acc1)
    den = pltpu.roll(jnp.where(first, acc1, acc0), HEAD_DIM, axis=1)
    return num / den


def _pair_queries(qt):
    n = qt.shape[1]
    q2 = jnp.concatenate([qt, qt], axis=1)
    r = lax.broadcasted_iota(jnp.int32, q2.shape, 0)
    c = lax.broadcasted_iota(jnp.int32, q2.shape, 1)
    own = jnp.right_shift(r, 6) == jnp.where(c >= n, 1, 0)
    return jnp.where(own, q2, jnp.zeros_like(q2))


def _merge_heads_out_t(acc0, acc1):
    first = lax.broadcasted_iota(jnp.int32, acc0.shape, 0) < HEAD_DIM
    num = jnp.where(first, acc0, acc1)
    den = jnp.where(first, acc0[HEAD_DIM:HEAD_DIM + 1, :], acc1[0:1, :])
    return (num / den).T


def _weighted_values(vaug, p_blocks, key_ranges):
    accs = []
    for h in range(2):
        cols = slice(h * Q_TILE, (h + 1) * Q_TILE)
        acc = None
        for p, (k0, k1) in zip(p_blocks, key_ranges):
            part = _dot(vaug[h][:, k0:k1], p[:, cols])
            acc = part if acc is None else acc + part
        accs.append(acc)
    return accs


def _fox_kernel(qt_ref, kt_ref, vt_ref, c2_ref, kf_ref, o_ref, v0_ref, v1_ref, kx_ref, *, seq):
    pair = pl.program_id(1)
    vt = vt_ref[0]
    v0_ref[...] = _augment_v(vt, 0, 0)
    v1_ref[...] = _augment_v(vt, 1, 0)
    vaug = (v0_ref, v1_ref)

    kx_ref[:, 0:PAIR] = kt_ref[0].astype(F32).T.astype(BF16)
    kx_ref[:, PAIR:2 * PAIR] = kf_ref[0]
    er = lax.broadcasted_iota(jnp.int32, (PAIR, 2 * Q_TILE), 0)
    ec = lax.broadcasted_iota(jnp.int32, (PAIR, 2 * Q_TILE), 1)
    head = 2 * pair + jnp.where(ec >= Q_TILE, 1, 0)
    piece = er - head
    coef = jnp.where((piece == 0) | (piece == N_HEADS) | (piece == 2 * N_HEADS), -1.0, 0.0).astype(BF16)
    key = lax.broadcasted_iota(jnp.int32, (Q_TILE, 2 * Q_TILE), 0)
    qry = jnp.bitwise_and(lax.broadcasted_iota(jnp.int32, (Q_TILE, 2 * Q_TILE), 1), Q_TILE - 1)
    causal = key <= qry

    n_qb = seq // Q_TILE

    def scores(qb):
        qs, ke = qb * Q_TILE, (qb + 1) * Q_TILE
        rhs = jnp.concatenate([_pair_queries(qt_ref[0, :, qs:ke]), coef], axis=0)
        return _dot(kx_ref[0:ke, :], rhs)

    def softmax(qb, s):
        qs, ke = qb * Q_TILE, (qb + 1) * Q_TILE
        cq = jnp.concatenate([c2_ref[0, 0:1, qs:ke], c2_ref[0, 1:2, qs:ke]], axis=1)
        t_diag = jnp.where(causal, s[qs:ke, :], -jnp.inf)
        m = jnp.max(t_diag, axis=0, keepdims=True)
        if qb > 0:
            t_past = s[0:qs, :]
            m = jnp.maximum(m, jnp.max(t_past, axis=0, keepdims=True))
        shift = cq - (m + cq)
        p_blocks = [jnp.exp2(t_diag + shift).astype(BF16)]
        key_ranges = [(qs, ke)]
        if qb > 0:
            p_blocks.append(jnp.exp2(t_past + shift).astype(BF16))
            key_ranges.append((0, qs))
        return p_blocks, key_ranges

    def finish(qb, p_blocks, key_ranges):
        acc0, acc1 = _weighted_values(vaug, p_blocks, key_ranges)
        o_ref[qb * Q_TILE:(qb + 1) * Q_TILE, :] = _merge_heads_out_t(acc0, acc1).astype(o_ref.dtype)

    order = list(range(n_qb - 1, -1, -1))
    s_next = scores(order[0])
    pending = None
    for i, qb in enumerate(order):
        s = s_next
        if i + 1 < n_qb:
            s_next = scores(order[i + 1])
        probs = softmax(qb, s)
        if pending is not None:
            finish(*pending)
        pending = (qb, *probs)
    finish(*pending)


def _fox(qt, kt, vt, c2, kf, *, batch, seq):
    n_tok = batch * seq
    feat = lambda: pl.BlockSpec((1, PAIR, seq), lambda b, p: (b, p, 0))
    return pl.pallas_call(
        functools.partial(_fox_kernel, seq=seq),
        grid=(batch, N_PAIRS),
        in_specs=[
            feat(), feat(), feat(),
            pl.BlockSpec((1, 2, seq), lambda b, p: (p, 0, b)),
            pl.BlockSpec((1, seq, PAIR), lambda b, p: (b, 0, 0)),
        ],
        out_specs=pl.BlockSpec((seq, PAIR), lambda b, p: (b, p)),
        out_shape=jax.ShapeDtypeStruct((n_tok, WIDTH), BF16),
        scratch_shapes=[pltpu.VMEM((PAIR, seq), BF16), pltpu.VMEM((PAIR, seq), BF16),
                        pltpu.VMEM((seq, 2 * PAIR), BF16)],
        compiler_params=_params(),
        name="fox",
    )(qt, kt, vt, c2, kf)


def _bias_kernel(rb_ref, tabm_ref, tabr_ref):
    rb = rb_ref[...]
    n_rows = rb.shape[1]
    r = lax.broadcasted_iota(jnp.int32, (n_rows, TOEPLITZ), 0)
    m = lax.broadcasted_iota(jnp.int32, (n_rows, TOEPLITZ), 1)
    e = jnp.where(m < TOEPLITZ // 2, m, m - TOEPLITZ)
    idx = jnp.clip(LEFT_REACH - Q_TILE - e, -REL_CLIP, REL_CLIP) + REL_CLIP
    onehot = jnp.where(r == idx, 1.0, 0.0).astype(BF16)
    hi = rb.astype(BF16)
    rest = rb - hi.astype(F32)
    mid = rest.astype(BF16)
    lo = (rest - mid.astype(F32)).astype(BF16)
    u = _dot(hi, onehot) + _dot(mid, onehot) + _dot(lo, onehot)

    i = lax.broadcasted_iota(jnp.int32, (Q_TILE, BAND_KEYS), 0)
    j = lax.broadcasted_iota(jnp.int32, (Q_TILE, BAND_KEYS), 1)
    qc = jnp.right_shift(i, 6)
    kc = jnp.right_shift(j, 6)
    valid = (kc >= qc) & (kc <= qc + LEFT_CHUNKS)
    n_new = tabr_ref.shape[1]
    for h in range(N_HEADS):
        base = jnp.broadcast_to(u[h:h + 1, :], (Q_TILE, TOEPLITZ))
        t = pltpu.roll(base, Q_TILE, axis=1, stride=1, stride_axis=0)[:, 0:BAND_KEYS]
        half = (h % 2) * Q_TILE
        tabm_ref[h // 2, :, half:half + Q_TILE] = jnp.where(valid, t * LOG2E, MASKED).T
        tabr_ref[h] = t[0:n_new, :]


def _bias_tables(rb_pad, *, n_new):
    return pl.pallas_call(
        _bias_kernel,
        out_shape=[
            jax.ShapeDtypeStruct((N_PAIRS, BAND_KEYS, 2 * Q_TILE), F32),
            jax.ShapeDtypeStruct((N_HEADS, n_new, BAND_KEYS), F32),
        ],
        compiler_params=_params(),
        name="bias_tables",
    )(rb_pad)


def _band_kernel(qt_ref, kt_ref, vt_ref, tab_ref, o_ref, v0_ref, v1_ref, k_ref, *, seq):
    vt = vt_ref[0]
    v0_ref[...] = _augment_v(vt, 0, 0)
    v1_ref[...] = _augment_v(vt, 1, 0)
    vaug = (v0_ref, v1_ref)
    k_ref[...] = kt_ref[0].astype(F32).T.astype(BF16)

    n_qb = seq // Q_TILE

    def scores(qb):
        qs, ke = qb * Q_TILE, (qb + 1) * Q_TILE
        ks = max(0, qs - LEFT_REACH)
        return _dot(k_ref[ks:ke, :], _pair_queries(qt_ref[0, :, qs:ke]))

    def softmax(qb, s):
        n_keys = s.shape[0]
        t = s + tab_ref[0, BAND_KEYS - n_keys:BAND_KEYS, :]
        m = jnp.max(t, axis=0, keepdims=True)
        return jnp.exp2(t - m).astype(BF16)

    def finish(qb, p):
        ke = (qb + 1) * Q_TILE
        acc0, acc1 = _weighted_values(vaug, [p], [(ke - p.shape[0], ke)])
        o_ref[qb * Q_TILE:ke, :] = _merge_heads_out_t(acc0, acc1).astype(o_ref.dtype)

    order = list(range(n_qb - 1, -1, -1))
    s_next = scores(order[0])
    pending = None
    for i, qb in enumerate(order):
        s = s_next
        if i + 1 < n_qb:
            s_next = scores(order[i + 1])
        p = softmax(qb, s)
        if pending is not None:
            finish(*pending)
        pending = (qb, p)
    finish(*pending)


def _band(qt, kt, vt, tabm, *, batch, seq):
    n_tok = batch * seq
    feat = lambda: pl.BlockSpec((1, PAIR, seq), lambda p, b: (b, p, 0))
    return pl.pallas_call(
        functools.partial(_band_kernel, seq=seq),
        grid=(N_PAIRS, batch),
        in_specs=[
            feat(), feat(), feat(),
            pl.BlockSpec((1, BAND_KEYS, 2 * Q_TILE), lambda p, b: (p, 0, 0)),
        ],
        out_specs=pl.BlockSpec((seq, PAIR), lambda p, b: (b, p)),
        out_shape=jax.ShapeDtypeStruct((n_tok, WIDTH), BF16),
        scratch_shapes=[pltpu.VMEM((PAIR, seq), BF16), pltpu.VMEM((PAIR, seq), BF16),
                        pltpu.VMEM((seq, PAIR), BF16)],
        compiler_params=_params(),
        name="band",
    )(qt, kt, vt, tabm)


def _two_part_attention(q2, kt_cache, vt_cache, kn_ref, vn_ref, bias_cache, bias_new, n_q):
    s_c = _dot(q2, kt_cache)
    s_n = _dot_nt(q2, kn_ref[...])
    accs = []
    for h in range(2):
        rows = slice(h * n_q, (h + 1) * n_q)
        t_c = s_c[rows, :] + bias_cache(h)
        t_n = s_n[rows, :] + bias_new(h)
        m = jnp.maximum(jnp.max(t_c, axis=1, keepdims=True), jnp.max(t_n, axis=1, keepdims=True))
        acc = _dot_nt(jnp.exp(t_c - m).astype(BF16), _augment_v(vt_cache, h, 0))
        acc = acc + _dot(jnp.exp(t_n - m).astype(BF16), _augment_v(vn_ref[...], h, 1))
        accs.append(acc)
    return _merge_heads_out(accs[0], accs[1])


def _sample_attn_kernel(qa_ref, ka_ref, va_ref, qb_ref, kb_ref, vb_ref,
                        cak_ref, cav_ref, cbk_ref, cbv_ref, clf_ref, nlf_ref, tab_ref,
                        oa_ref, ob_ref, kn_ref, vn_ref, *, n_q):
    row = lax.broadcasted_iota(jnp.int32, (n_q, NEW_PAD), 0)
    col = lax.broadcasted_iota(jnp.int32, (n_q, NEW_PAD), 1)

    def stage_new(k_new, v_new):
        kn_ref[...] = jnp.zeros_like(kn_ref)
        vn_ref[...] = jnp.zeros_like(vn_ref)
        kn_ref[0:n_q, :] = k_new
        vn_ref[0:n_q, :] = v_new

    c_cache = _lane_cumsum(clf_ref[0, 0])
    n_cache = c_cache.shape[1]
    c_new = _lane_cumsum(nlf_ref[0, 0]) + c_cache[:, n_cache - 1:n_cache]

    def fox_cache(h):
        cq = jnp.sum(jnp.where(col == row, c_new[h:h + 1, :], 0.0), axis=1, keepdims=True)
        return cq - c_cache[h:h + 1, :]

    def fox_new(h):
        cq = jnp.sum(jnp.where(col == row, c_new[h:h + 1, :], 0.0), axis=1, keepdims=True)
        return jnp.where(col <= row, cq - c_new[h:h + 1, :], -jnp.inf)

    stage_new(ka_ref[...], va_ref[...])
    oa = _two_part_attention(_split_heads_q(qa_ref[...]), cak_ref[0].astype(BF16), cav_ref[0].astype(BF16),
                             kn_ref, vn_ref, fox_cache, fox_new, n_q)
    oa_ref[...] = oa.astype(oa_ref.dtype)

    n_band = cbk_ref.shape[2]

    def band_cache(h):
        return tab_ref[h, :, 0:n_band]

    def band_new(h):
        return jnp.where(col < n_q, tab_ref[h, :, n_band:n_band + NEW_PAD], -jnp.inf)

    stage_new(kb_ref[...], vb_ref[...])
    ob = _two_part_attention(_split_heads_q(qb_ref[...]), cbk_ref[0].astype(BF16), cbv_ref[0].astype(BF16),
                             kn_ref, vn_ref, band_cache, band_new, n_q)
    ob_ref[...] = ob.astype(ob_ref.dtype)


def _sample_attn(qa, ka, va, qb, kb, vb, cakt, cavt, cbkt, cbvt, clf, nlf, tabr, *, batch, n_q):
    n_tok = batch * n_q
    past = cakt.shape[2]
    n_band = cbkt.shape[2]
    new = lambda: pl.BlockSpec((n_q, PAIR), lambda b, p: (b, p))
    cache = lambda n: pl.BlockSpec((1, PAIR, n), lambda b, p: (b, p, 0))
    return pl.pallas_call(
        functools.partial(_sample_attn_kernel, n_q=n_q),
        grid=(batch, N_PAIRS),
        in_specs=[
            new(), new(), new(), new(), new(), new(),
            cache(past), cache(past), cache(n_band), cache(n_band),
            pl.BlockSpec((1, 1, 2, past), lambda b, p: (b, p, 0, 0)),
            pl.BlockSpec((1, 1, 2, NEW_PAD), lambda b, p: (b, p, 0, 0)),
            pl.BlockSpec((2, n_q, BAND_KEYS), lambda b, p: (p, 0, 0)),
        ],
        out_specs=[new(), new()],
        out_shape=[jax.ShapeDtypeStruct((n_tok, WIDTH), BF16)] * 2,
        scratch_shapes=[pltpu.VMEM((NEW_PAD, PAIR), BF16), pltpu.VMEM((NEW_PAD, PAIR), BF16)],
        compiler_params=_params(),
        name="sample_attn",
    )(qa, ka, va, qb, kb, vb, cakt, cavt, cbkt, cbvt, clf, nlf, tabr)


def _post_kernel(x_ref, oa_ref, ob_ref, ga_ref, gb_ref, wo_ref, gpost_ref, gpre_ref,
                 wg_ref, wu_ref, wd_ref, gffn_ref, y_ref, *, n_sub):
    sub = x_ref.shape[0] // n_sub
    rows = [slice(i * sub, (i + 1) * sub) for i in range(n_sub)]
    st = [dict() for _ in range(n_sub)]

    def norm(i):
        st[i]["na"] = _rms(oa_ref[rows[i], :].astype(F32), ga_ref[...]).astype(BF16)
        st[i]["nb"] = _rms(ob_ref[rows[i], :].astype(F32), gb_ref[...]).astype(BF16)

    def mix(i):
        st[i]["mix"] = _dot(st[i]["na"], wo_ref[0:WIDTH, :]) + _dot(st[i]["nb"], wo_ref[WIDTH:2 * WIDTH, :])

    def mid(i):
        x1 = x_ref[rows[i], :] + _rms(st[i]["mix"], gpost_ref[...])
        st[i]["x1"] = x1
        st[i]["hf"] = _rms(x1, gpre_ref[...]).astype(BF16)

    def gate_up(i):
        st[i]["g"] = _dot(st[i]["hf"], wg_ref[...])
        st[i]["u"] = _dot(st[i]["hf"], wu_ref[...])

    def silu(i):
        g = st[i]["g"]
        st[i]["a"] = (g * jax.nn.sigmoid(g) * st[i]["u"]).astype(BF16)

    def down(i):
        st[i]["ff"] = _dot(st[i]["a"], wd_ref[...])

    def fin(i):
        y_ref[rows[i], :] = st[i]["x1"] + _rms(st[i]["ff"], gffn_ref[...])

    vpu_stages = [norm, mid, silu, fin]
    mxu_stages = [mix, gate_up, down]
    norm(0)
    for k, mxu in enumerate(mxu_stages):
        for i in range(n_sub):
            mxu(i)
            if i + 1 < n_sub:
                vpu_stages[k](i + 1)
            else:
                vpu_stages[k + 1](0)
    for i in range(1, n_sub):
        fin(i)


def _post(x, oa, ob, ga, gb, wo, gpost, gpre, wg, wu, wd, gffn, *, tm, n_sub):
    n_tok, d_model = x.shape
    row = lambda w: pl.BlockSpec((tm, w), lambda i: (i, 0))
    return pl.pallas_call(
        functools.partial(_post_kernel, n_sub=n_sub),
        grid=(n_tok // tm,),
        in_specs=[
            row(d_model), row(WIDTH), row(WIDTH),
            _resident(ga.shape), _resident(gb.shape), _resident(wo.shape),
            _resident(gpost.shape), _resident(gpre.shape),
            _resident(wg.shape), _resident(wu.shape), _resident(wd.shape), _resident(gffn.shape),
        ],
        out_specs=row(d_model),
        out_shape=jax.ShapeDtypeStruct((n_tok, d_model), F32),
        compiler_params=_params(),
        name="post",
    )(x, oa, ob, ga, gb, wo, gpost, gpre, wg, wu, wd, gffn)


def _row(v):
    return v.reshape(1, -1).astype(F32)


def _state_from_feature_major(t, batch, n):
    return t.reshape(batch, N_HEADS, HEAD_DIM, n).transpose(0, 3, 1, 2)


def _cache_feature_major(c):
    b, n = c.shape[0], c.shape[1]
    return c.transpose(0, 2, 3, 1).reshape(b, WIDTH, n)


def kernel(x_prompt, x_sample, cache_a_k, cache_a_v, cache_a_logf, cache_b_k, cache_b_v, norm_mix_pre, w_in, b_forget, rel_bias, gain_out_a, gain_out_b, w_out, norm_mix_post, norm_ffn_pre, w_gate, w_up, w_down, norm_ffn_post):
    batch, seq, d_model = x_prompt.shape
    dec_batch, dec_seq, _ = x_sample.shape
    depth = w_in.shape[0]
    past = cache_a_k.shape[2]
    n_band = cache_b_k.shape[2]
    keep = min(LEFT_REACH, seq)
    assert seq % Q_TILE == 0 and keep == LEFT_REACH and n_band == LEFT_REACH and dec_seq <= CHUNK
    n_tok_p, n_tok_s = batch * seq, dec_batch * dec_seq
    tm_p = keep

    xp = x_prompt.reshape(n_tok_p, d_model)
    xs = x_sample.reshape(n_tok_s, d_model)
    outs = [[] for _ in range(10)]
    fa0, fa1 = 3 * WIDTH, 3 * WIDTH + N_HEADS
    for l in range(depth):
        wl = w_in[l]
        qa_w, ka_w, va_w = wl[:, 0:WIDTH], wl[:, WIDTH:2 * WIDTH], wl[:, 2 * WIDTH:fa0]
        f_w = wl[:, fa0:fa1]
        qb_w, kb_w, vb_w = wl[:, fa1:fa1 + WIDTH], wl[:, fa1 + WIDTH:fa1 + 2 * WIDTH], wl[:, fa1 + 2 * WIDTH:]
        w_cat = jnp.concatenate(
            [qa_w, ka_w, va_w, qb_w, kb_w, vb_w, jnp.pad(f_w, ((0, 0), (0, PAIR - N_HEADS)))], axis=1).astype(BF16)
        w_t = jnp.concatenate(
            [ka_w, va_w, kb_w, vb_w, qa_w, qb_w, jnp.pad(f_w, ((0, 0), (0, N_HEADS)))], axis=1).T.astype(BF16)
        b_col = b_forget[l].reshape(N_HEADS, 1).astype(F32)
        rb_pad = jnp.pad(rel_bias[l].astype(F32), ((0, 16 - N_HEADS), (0, 384 - (2 * REL_CLIP + 1))))
        tail = (_row(gain_out_a[l]), _row(gain_out_b[l]), w_out[l].astype(BF16), _row(norm_mix_post[l]),
                _row(norm_ffn_pre[l]), w_gate[l].astype(BF16), w_up[l].astype(BF16), w_down[l].astype(BF16),
                _row(norm_ffn_post[l]))
        tabm, tabr = _bias_tables(rb_pad, n_new=dec_seq)

        qat, qbt, kat, vat, kbt, vbt, kat32, vat32, logf_t, kbt32, vbt32 = _inproj_t(
            xp, _row(norm_mix_pre[l]), w_t, b_col, batch=batch, seq=seq, tm=tm_p)
        c2, kf = _cumsum(logf_t, batch=batch, seq=seq)
        oa = _fox(qat, kat, vat, c2.reshape(N_PAIRS, 2, n_tok_p), kf, batch=batch, seq=seq)
        ob = _band(qbt, kbt, vbt, tabm, batch=batch, seq=seq)
        xp = _post(xp, oa, ob, *tail, tm=512, n_sub=2)
        outs[0].append(_state_from_feature_major(kat32, batch, seq))
        outs[1].append(_state_from_feature_major(vat32, batch, seq))
        outs[2].append(logf_t.T.reshape(batch, seq, N_HEADS))
        outs[3].append(_state_from_feature_major(kbt32, batch, keep))
        outs[4].append(_state_from_feature_major(vbt32, batch, keep))

        qa, ka, va, qb, kb, vb, ka32, va32, logf_t, kb32, vb32 = _inproj(xs, _row(norm_mix_pre[l]), w_cat, b_col)
        clf = cache_a_logf[l].astype(F32).transpose(0, 2, 1).reshape(dec_batch, N_PAIRS, 2, past)
        nlf = logf_t.reshape(N_PAIRS, 2, dec_batch, dec_seq).transpose(2, 0, 1, 3)
        nlf = jnp.pad(nlf, ((0, 0), (0, 0), (0, 0), (0, NEW_PAD - dec_seq)))
        oa, ob = _sample_attn(
            qa, ka, va, qb, kb, vb,
            _cache_feature_major(cache_a_k[l]), _cache_feature_major(cache_a_v[l]),
            _cache_feature_major(cache_b_k[l]), _cache_feature_major(cache_b_v[l]),
            clf, nlf, tabr, batch=dec_batch, n_q=dec_seq)
        xs = _post(xs, oa, ob, *tail, tm=n_tok_s, n_sub=1)
        outs[5].append(ka32.reshape(dec_batch, dec_seq, N_HEADS, HEAD_DIM))
        outs[6].append(va32.reshape(dec_batch, dec_seq, N_HEADS, HEAD_DIM))
        outs[7].append(logf_t.T.reshape(dec_batch, dec_seq, N_HEADS))
        outs[8].append(kb32.reshape(dec_batch, dec_seq, N_HEADS, HEAD_DIM))
        outs[9].append(vb32.reshape(dec_batch, dec_seq, N_HEADS, HEAD_DIM))

    st = jnp.stack
    return (xp.reshape(batch, seq, d_model), xs.reshape(dec_batch, dec_seq, d_model),
            *[st(o) for o in outs])
```

```python
import functools

import jax
import jax.numpy as jnp
from jax import lax
from jax.experimental import pallas as pl
from jax.experimental.pallas import tpu as pltpu

F32 = jnp.float32
BF16 = jnp.bfloat16

HEAD_DIM = 64
N_HEADS = 8
WIDTH = N_HEADS * HEAD_DIM
PAIR = 2 * HEAD_DIM
N_PAIRS = N_HEADS // 2
CHUNK = 64
LEFT_CHUNKS = 8
LEFT_REACH = LEFT_CHUNKS * CHUNK
REL_CLIP = 128
EPS = 1e-6
SCALE = HEAD_DIM ** -0.5
LOG2E = 1.4426950408889634
Q_TILE = 256
BAND_KEYS = LEFT_REACH + Q_TILE
TOEPLITZ = 1024
MASKED = -1e30
NEW_PAD = 128
VMEM_LIMIT = 56 * 1024 * 1024


def _params():
    return pltpu.CompilerParams(vmem_limit_bytes=VMEM_LIMIT)


def _resident(shape):
    return pl.BlockSpec(shape, lambda *_: (0,) * len(shape), pipeline_mode=pl.Buffered(1))


def _rms(x, g):
    ms = jnp.mean(x * x, axis=-1, keepdims=True)
    return x * lax.rsqrt(ms + EPS) * g


def _log_sigmoid(x):
    return jnp.minimum(x, 0.0) - jnp.log1p(jnp.exp(-jnp.abs(x)))


def _dot(a, b):
    return jnp.dot(a, b, preferred_element_type=F32)


def _dot_nt(a, b):
    return lax.dot_general(a, b, (((1,), (1,)), ((), ())), preferred_element_type=F32)


def _inproj_kernel(x_ref, g_ref, w_ref, bf_ref,
                   qa_ref, ka_ref, va_ref, qb_ref, kb_ref, vb_ref,
                   ka32_ref, va32_ref, logf_ref, kb32_ref, vb32_ref):
    h = _rms(x_ref[...], g_ref[...]).astype(BF16)

    def proj(c):
        return _dot(h, w_ref[:, c * WIDTH:(c + 1) * WIDTH])

    qa_ref[...] = (proj(0) * SCALE).astype(BF16)
    z = proj(1)
    ka32_ref[...] = z
    ka_ref[...] = z.astype(BF16)
    z = proj(2)
    va32_ref[...] = z
    va_ref[...] = z.astype(BF16)
    qb_ref[...] = (proj(3) * SCALE).astype(BF16)
    z = proj(4)
    kb32_ref[...] = z
    kb_ref[...] = z.astype(BF16)
    z = proj(5)
    vb32_ref[...] = z
    vb_ref[...] = z.astype(BF16)
    f = _dot(h, w_ref[:, 6 * WIDTH:6 * WIDTH + PAIR])
    logf_ref[...] = _log_sigmoid(f.T[0:N_HEADS, :] + bf_ref[...])


def _inproj(x, gain, w_cat, b_col):
    n_tok, d_model = x.shape
    full = lambda w: pl.BlockSpec((n_tok, w), lambda i: (0, 0))
    out_shape = (
        [jax.ShapeDtypeStruct((n_tok, WIDTH), BF16)] * 6
        + [jax.ShapeDtypeStruct((n_tok, WIDTH), F32)] * 2
        + [jax.ShapeDtypeStruct((N_HEADS, n_tok), F32)]
        + [jax.ShapeDtypeStruct((n_tok, WIDTH), F32)] * 2
    )
    out_specs = ([full(WIDTH) for _ in range(8)]
                 + [pl.BlockSpec((N_HEADS, n_tok), lambda i: (0, 0)), full(WIDTH), full(WIDTH)])
    return pl.pallas_call(
        _inproj_kernel,
        grid=(1,),
        in_specs=[full(d_model), _resident((1, d_model)), _resident(w_cat.shape), _resident((N_HEADS, 1))],
        out_specs=out_specs,
        out_shape=out_shape,
        compiler_params=_params(),
        name="inproj_sample",
    )(x, gain, w_cat, b_col)


def _inproj_t_kernel(x_ref, g_ref, wt_ref, bf_ref,
                     qat_ref, qbt_ref, kat_ref, vat_ref, kbt_ref, vbt_ref,
                     kat32_ref, vat32_ref, logf_ref, kbt32_ref, vbt32_ref, *, tiles_per_seq):
    h = _rms(x_ref[...], g_ref[...]).astype(BF16)

    def proj_t(c, rows=WIDTH):
        return _dot_nt(wt_ref[c * WIDTH:c * WIDTH + rows, :], h)

    z = proj_t(0)
    kat32_ref[0] = z
    kat_ref[0] = z.astype(BF16)
    z = proj_t(1)
    vat32_ref[0] = z
    vat_ref[0] = z.astype(BF16)
    zk = proj_t(2)
    kbt_ref[0] = zk.astype(BF16)
    zv = proj_t(3)
    vbt_ref[0] = zv.astype(BF16)
    qat_ref[0] = (proj_t(4) * (SCALE * LOG2E)).astype(BF16)
    zq = proj_t(5, WIDTH + 2 * N_HEADS)
    qbt_ref[0] = (zq[0:WIDTH] * (SCALE * LOG2E)).astype(BF16)
    logf_ref[...] = _log_sigmoid(zq[WIDTH:WIDTH + N_HEADS] + bf_ref[...])

    @pl.when(pl.program_id(0) % tiles_per_seq == tiles_per_seq - 1)
    def _():
        kbt32_ref[0] = zk
        vbt32_ref[0] = zv


def _inproj_t(x, gain, wt, b_col, *, batch, seq, tm):
    n_tok, d_model = x.shape
    tps = seq // tm
    feat = lambda: pl.BlockSpec((1, WIDTH, tm), lambda i: (i // tps, 0, i % tps))
    keep = lambda: pl.BlockSpec((1, WIDTH, tm), lambda i: (i // tps, 0, 0))
    out_shape = (
        [jax.ShapeDtypeStruct((batch, WIDTH, seq), BF16)] * 6
        + [jax.ShapeDtypeStruct((batch, WIDTH, seq), F32)] * 2
        + [jax.ShapeDtypeStruct((N_HEADS, n_tok), F32)]
        + [jax.ShapeDtypeStruct((batch, WIDTH, tm), F32)] * 2
    )
    out_specs = ([feat() for _ in range(8)]
                 + [pl.BlockSpec((N_HEADS, tm), lambda i: (0, i)), keep(), keep()])
    return pl.pallas_call(
        functools.partial(_inproj_t_kernel, tiles_per_seq=tps),
        grid=(n_tok // tm,),
        in_specs=[
            pl.BlockSpec((tm, d_model), lambda i: (i, 0)),
            _resident((1, d_model)),
            _resident(wt.shape),
            _resident((N_HEADS, 1)),
        ],
        out_specs=out_specs,
        out_shape=out_shape,
        compiler_params=_params(),
        name="inproj",
    )(x, gain, wt, b_col)


def _lane_cumsum(x):
    n = x.shape[-1]
    lane = lax.broadcasted_iota(jnp.int32, x.shape, x.ndim - 1)
    shift = 1
    while shift < n:
        x = x + jnp.where(lane >= shift, pltpu.roll(x, shift, axis=x.ndim - 1), 0.0)
        shift *= 2
    return x


def _split3(x):
    hi = x.astype(BF16).astype(F32)
    rest = x - hi
    mid = rest.astype(BF16).astype(F32)
    return hi, mid, rest - mid


def _cumsum_kernel(lf_ref, c2_ref, kf_ref):
    c2 = _lane_cumsum(lf_ref[...]) * LOG2E
    c2_ref[...] = c2
    seq = c2.shape[1]
    hi, mid, lo = _split3(c2)
    feat = jnp.concatenate([hi, mid, lo, jnp.zeros((PAIR - 3 * N_HEADS, seq), F32)], axis=0)
    kf_ref[0] = feat.T.astype(BF16)


def _cumsum(logf_t, *, batch, seq):
    n_tok = batch * seq
    return pl.pallas_call(
        _cumsum_kernel,
        grid=(batch,),
        in_specs=[pl.BlockSpec((N_HEADS, seq), lambda b: (0, b))],
        out_specs=[
            pl.BlockSpec((N_HEADS, seq), lambda b: (0, b)),
            pl.BlockSpec((1, seq, PAIR), lambda b: (b, 0, 0)),
        ],
        out_shape=[
            jax.ShapeDtypeStruct((N_HEADS, n_tok), F32),
            jax.ShapeDtypeStruct((batch, seq, PAIR), BF16),
        ],
        compiler_params=_params(),
        name="cumsum",
    )(logf_t)


def _split_heads_q(q):
    lane = lax.broadcasted_iota(jnp.int32, q.shape, 1)
    zero = jnp.zeros_like(q)
    return jnp.concatenate([jnp.where(lane < HEAD_DIM, q, zero),
                            jnp.where(lane >= HEAD_DIM, q, zero)], axis=0)


def _augment_v(v, h, axis):
    idx = lax.broadcasted_iota(jnp.int32, v.shape, axis)
    own = (idx < HEAD_DIM) if h == 0 else (idx >= HEAD_DIM)
    return jnp.where(own, v, jnp.ones_like(v))


def _merge_heads_out(acc0, acc1):
    lane = lax.broadcasted_iota(jnp.int32, acc0.shape, 1)
    first = lane < HEAD_DIM
    num = jnp.where(first, acc0, acc1)
    den = pltpu.roll(jnp.where(first, acc1, acc0), HEAD_DIM, axis=1)
    return num / den


def _pair_queries(qt):
    n = qt.shape[1]
    q2 = jnp.concatenate([qt, qt], axis=1)
    r = lax.broadcasted_iota(jnp.int32, q2.shape, 0)
    c = lax.broadcasted_iota(jnp.int32, q2.shape, 1)
    own = jnp.right_shift(r, 6) == jnp.where(c >= n, 1, 0)
    return jnp.where(own, q2, jnp.zeros_like(q2))


def _merge_heads_out_t(acc0, acc1):
    first = lax.broadcasted_iota(jnp.int32, acc0.shape, 0) < HEAD_DIM
    num = jnp.where(first, acc0, acc1)
    den = jnp.where(first, acc0[HEAD_DIM:HEAD_DIM + 1, :], acc1[0:1, :])
    return (num / den).T


def _weighted_values(vaug, p_blocks, key_ranges):
    accs = []
    for h in range(2):
        cols = slice(h * Q_TILE, (h + 1) * Q_TILE)
        acc = None
        for p, (k0, k1) in zip(p_blocks, key_ranges):
            part = _dot(vaug[h][:, k0:k1], p[:, cols])
            acc = part if acc is None else acc + part
        accs.append(acc)
    return accs


def _fox_kernel(qt_ref, kt_ref, vt_ref, c2_ref, kf_ref, o_ref, v0_ref, v1_ref, kx_ref, *, seq):
    pair = pl.program_id(1)
    vt = vt_ref[0]
    v0_ref[...] = _augment_v(vt, 0, 0)
    v1_ref[...] = _augment_v(vt, 1, 0)
    vaug = (v0_ref, v1_ref)

    kx_ref[:, 0:PAIR] = kt_ref[0].astype(F32).T.astype(BF16)
    kx_ref[:, PAIR:2 * PAIR] = kf_ref[0]
    er = lax.broadcasted_iota(jnp.int32, (PAIR, 2 * Q_TILE), 0)
    ec = lax.broadcasted_iota(jnp.int32, (PAIR, 2 * Q_TILE), 1)
    head = 2 * pair + jnp.where(ec >= Q_TILE, 1, 0)
    piece = er - head
    coef = jnp.where((piece == 0) | (piece == N_HEADS) | (piece == 2 * N_HEADS), -1.0, 0.0).astype(BF16)
    key = lax.broadcasted_iota(jnp.int32, (Q_TILE, 2 * Q_TILE), 0)
    qry = jnp.bitwise_and(lax.broadcasted_iota(jnp.int32, (Q_TILE, 2 * Q_TILE), 1), Q_TILE - 1)
    causal = key <= qry

    n_qb = seq // Q_TILE

    def scores(qb):
        qs, ke = qb * Q_TILE, (qb + 1) * Q_TILE
        rhs = jnp.concatenate([_pair_queries(qt_ref[0, :, qs:ke]), coef], axis=0)
        return _dot(kx_ref[0:ke, :], rhs)

    def softmax(qb, s):
        qs, ke = qb * Q_TILE, (qb + 1) * Q_TILE
        cq = jnp.concatenate([c2_ref[0, 0:1, qs:ke], c2_ref[0, 1:2, qs:ke]], axis=1)
        t_diag = jnp.where(causal, s[qs:ke, :], -jnp.inf)
        m = jnp.max(t_diag, axis=0, keepdims=True)
        if qb > 0:
            t_past = s[0:qs, :]
            m = jnp.maximum(m, jnp.max(t_past, axis=0, keepdims=True))
        shift = cq - (m + cq)
        p_blocks = [jnp.exp2(t_diag + shift).astype(BF16)]
        key_ranges = [(qs, ke)]
        if qb > 0:
            p_blocks.append(jnp.exp2(t_past + shift).astype(BF16))
            key_ranges.append((0, qs))
        return p_blocks, key_ranges

    def finish(qb, p_blocks, key_ranges):
        acc0, acc1 = _weighted_values(vaug, p_blocks, key_ranges)
        o_ref[qb * Q_TILE:(qb + 1) * Q_TILE, :] = _merge_heads_out_t(acc0, acc1).astype(o_ref.dtype)

    order = list(range(n_qb - 1, -1, -1))
    s_next = scores(order[0])
    pending = None
    for i, qb in enumerate(order):
        s = s_next
        if i + 1 < n_qb:
            s_next = scores(order[i + 1])
        probs = softmax(qb, s)
        if pending is not None:
            finish(*pending)
        pending = (qb, *probs)
    finish(*pending)


def _fox(qt, kt, vt, c2, kf, *, batch, seq):
    n_tok = batch * seq
    feat = lambda: pl.BlockSpec((1, PAIR, seq), lambda b, p: (b, p, 0))
    return pl.pallas_call(
        functools.partial(_fox_kernel, seq=seq),
        grid=(batch, N_PAIRS),
        in_specs=[
            feat(), feat(), feat(),
            pl.BlockSpec((1, 2, seq), lambda b, p: (p, 0, b)),
            pl.BlockSpec((1, seq, PAIR), lambda b, p: (b, 0, 0)),
        ],
        out_specs=pl.BlockSpec((seq, PAIR), lambda b, p: (b, p)),
        out_shape=jax.ShapeDtypeStruct((n_tok, WIDTH), BF16),
        scratch_shapes=[pltpu.VMEM((PAIR, seq), BF16), pltpu.VMEM((PAIR, seq), BF16),
                        pltpu.VMEM((seq, 2 * PAIR), BF16)],
        compiler_params=_params(),
        name="fox",
    )(qt, kt, vt, c2, kf)


def _bias_kernel(rb_ref, tabm_ref, tabr_ref):
    rb = rb_ref[...]
    n_rows = rb.shape[1]
    r = lax.broadcasted_iota(jnp.int32, (n_rows, TOEPLITZ), 0)
    m = lax.broadcasted_iota(jnp.int32, (n_rows, TOEPLITZ), 1)
    e = jnp.where(m < TOEPLITZ // 2, m, m - TOEPLITZ)
    idx = jnp.clip(LEFT_REACH - Q_TILE - e, -REL_CLIP, REL_CLIP) + REL_CLIP
    onehot = jnp.where(r == idx, 1.0, 0.0).astype(BF16)
    hi = rb.astype(BF16)
    rest = rb - hi.astype(F32)
    mid = rest.astype(BF16)
    lo = (rest - mid.astype(F32)).astype(BF16)
    u = _dot(hi, onehot) + _dot(mid, onehot) + _dot(lo, onehot)

    i = lax.broadcasted_iota(jnp.int32, (Q_TILE, BAND_KEYS), 0)
    j = lax.broadcasted_iota(jnp.int32, (Q_TILE, BAND_KEYS), 1)
    qc = jnp.right_shift(i, 6)
    kc = jnp.right_shift(j, 6)
    valid = (kc >= qc) & (kc <= qc + LEFT_CHUNKS)
    n_new = tabr_ref.shape[1]
    for h in range(N_HEADS):
        base = jnp.broadcast_to(u[h:h + 1, :], (Q_TILE, TOEPLITZ))
        t = pltpu.roll(base, Q_TILE, axis=1, stride=1, stride_axis=0)[:, 0:BAND_KEYS]
        half = (h % 2) * Q_TILE
        tabm_ref[h // 2, :, half:half + Q_TILE] = jnp.where(valid, t * LOG2E, MASKED).T
        tabr_ref[h] = t[0:n_new, :]


def _bias_tables(rb_pad, *, n_new):
    return pl.pallas_call(
        _bias_kernel,
        out_shape=[
            jax.ShapeDtypeStruct((N_PAIRS, BAND_KEYS, 2 * Q_TILE), F32),
            jax.ShapeDtypeStruct((N_HEADS, n_new, BAND_KEYS), F32),
        ],
        compiler_params=_params(),
        name="bias_tables",
    )(rb_pad)


def _band_kernel(qt_ref, kt_ref, vt_ref, tab_ref, o_ref, v0_ref, v1_ref, k_ref, *, seq):
    vt = vt_ref[0]
    v0_ref[...] = _augment_v(vt, 0, 0)
    v1_ref[...] = _augment_v(vt, 1, 0)
    vaug = (v0_ref, v1_ref)
    k_ref[...] = kt_ref[0].astype(F32).T.astype(BF16)

    n_qb = seq // Q_TILE

    def scores(qb):
        qs, ke = qb * Q_TILE, (qb + 1) * Q_TILE
        ks = max(0, qs - LEFT_REACH)
        return _dot(k_ref[ks:ke, :], _pair_queries(qt_ref[0, :, qs:ke]))

    def softmax(qb, s):
        n_keys = s.shape[0]
        t = s + tab_ref[0, BAND_KEYS - n_keys:BAND_KEYS, :]
        m = jnp.max(t, axis=0, keepdims=True)
        return jnp.exp2(t - m).astype(BF16)

    def finish(qb, p):
        ke = (qb + 1) * Q_TILE
        acc0, acc1 = _weighted_values(vaug, [p], [(ke - p.shape[0], ke)])
        o_ref[qb * Q_TILE:ke, :] = _merge_heads_out_t(acc0, acc1).astype(o_ref.dtype)

    order = list(range(n_qb - 1, -1, -1))
    s_next = scores(order[0])
    pending = None
    for i, qb in enumerate(order):
        s = s_next
        if i + 1 < n_qb:
            s_next = scores(order[i + 1])
        p = softmax(qb, s)
        if pending is not None:
            finish(*pending)
        pending = (qb, p)
    finish(*pending)


def _band(qt, kt, vt, tabm, *, batch, seq):
    n_tok = batch * seq
    feat = lambda: pl.BlockSpec((1, PAIR, seq), lambda p, b: (b, p, 0))
    return pl.pallas_call(
        functools.partial(_band_kernel, seq=seq),
        grid=(N_PAIRS, batch),
        in_specs=[
            feat(), feat(), feat(),
            pl.BlockSpec((1, BAND_KEYS, 2 * Q_TILE), lambda p, b: (p, 0, 0)),
        ],
        out_specs=pl.BlockSpec((seq, PAIR), lambda p, b: (b, p)),
        out_shape=jax.ShapeDtypeStruct((n_tok, WIDTH), BF16),
        scratch_shapes=[pltpu.VMEM((PAIR, seq), BF16), pltpu.VMEM((PAIR, seq), BF16),
                        pltpu.VMEM((seq, PAIR), BF16)],
        compiler_params=_params(),
        name="band",
    )(qt, kt, vt, tabm)


def _two_part_attention(q2, kt_cache, vt_cache, kn_ref, vn_ref, bias_cache, bias_new, n_q):
    s_c = _dot(q2, kt_cache)
    s_n = _dot_nt(q2, kn_ref[...])
    accs = []
    for h in range(2):
        rows = slice(h * n_q, (h + 1) * n_q)
        t_c = s_c[rows, :] + bias_cache(h)
        t_n = s_n[rows, :] + bias_new(h)
        m = jnp.maximum(jnp.max(t_c, axis=1, keepdims=True), jnp.max(t_n, axis=1, keepdims=True))
        acc = _dot_nt(jnp.exp(t_c - m).astype(BF16), _augment_v(vt_cache, h, 0))
        acc = acc + _dot(jnp.exp(t_n - m).astype(BF16), _augment_v(vn_ref[...], h, 1))
        accs.append(acc)
    return _merge_heads_out(accs[0], accs[1])


def _sample_attn_kernel(qa_ref, ka_ref, va_ref, qb_ref, kb_ref, vb_ref,
                        cak_ref, cav_ref, cbk_ref, cbv_ref, c_ref, tab_ref,
                        oa_ref, ob_ref, kn_ref, vn_ref, *, n_q):
    row = lax.broadcasted_iota(jnp.int32, (n_q, NEW_PAD), 0)
    col = lax.broadcasted_iota(jnp.int32, (n_q, NEW_PAD), 1)
    n_cache = cak_ref.shape[2]
    n_band = cbk_ref.shape[2]

    def stage_new(slot, k_new, v_new):
        kn_ref[slot] = jnp.zeros((NEW_PAD, PAIR), BF16)
        vn_ref[slot] = jnp.zeros((NEW_PAD, PAIR), BF16)
        kn_ref[slot, 0:n_q, :] = k_new
        vn_ref[slot, 0:n_q, :] = v_new
        return kn_ref.at[slot], vn_ref.at[slot]

    for p in range(N_PAIRS):
        feats = slice(p * PAIR, (p + 1) * PAIR)

        c_cache = c_ref[0, p, :, 0:n_cache]
        c_new = c_ref[0, p, :, n_cache:n_cache + NEW_PAD]

        def fox_cache(h, c_cache=c_cache, c_new=c_new):
            cq = jnp.sum(jnp.where(col == row, c_new[h:h + 1, :], 0.0), axis=1, keepdims=True)
            return cq - c_cache[h:h + 1, :]

        def fox_new(h, c_new=c_new):
            cq = jnp.sum(jnp.where(col == row, c_new[h:h + 1, :], 0.0), axis=1, keepdims=True)
            return jnp.where(col <= row, cq - c_new[h:h + 1, :], -jnp.inf)

        kn, vn = stage_new(2 * p, ka_ref[:, feats], va_ref[:, feats])
        oa = _two_part_attention(_split_heads_q(qa_ref[:, feats]), cak_ref[0, feats, :].astype(BF16),
                                 cav_ref[0, feats, :].astype(BF16), kn, vn, fox_cache, fox_new, n_q)
        oa_ref[:, feats] = oa.astype(oa_ref.dtype)

        def band_cache(h, p=p):
            return tab_ref[2 * p + h, :, 0:n_band]

        def band_new(h, p=p):
            return jnp.where(col < n_q, tab_ref[2 * p + h, :, n_band:n_band + NEW_PAD], -jnp.inf)

        kn, vn = stage_new(2 * p + 1, kb_ref[:, feats], vb_ref[:, feats])
        ob = _two_part_attention(_split_heads_q(qb_ref[:, feats]), cbk_ref[0, feats, :].astype(BF16),
                                 cbv_ref[0, feats, :].astype(BF16), kn, vn, band_cache, band_new, n_q)
        ob_ref[:, feats] = ob.astype(ob_ref.dtype)


def _rows_cumsum_kernel(x_ref, c_ref):
    c_ref[...] = _lane_cumsum(x_ref[...])


def _rows_cumsum(x):
    return pl.pallas_call(
        _rows_cumsum_kernel,
        out_shape=jax.ShapeDtypeStruct(x.shape, F32),
        compiler_params=_params(),
        name="rows_cumsum",
    )(x)


def _sample_attn(qa, ka, va, qb, kb, vb, cakt, cavt, cbkt, cbvt, c_all, tabr, *, batch, n_q):
    n_tok = batch * n_q
    past = cakt.shape[2]
    n_band = cbkt.shape[2]
    new = lambda: pl.BlockSpec((n_q, WIDTH), lambda b: (b, 0))
    cache = lambda n: pl.BlockSpec((1, WIDTH, n), lambda b: (b, 0, 0))
    return pl.pallas_call(
        functools.partial(_sample_attn_kernel, n_q=n_q),
        grid=(batch,),
        in_specs=[
            new(), new(), new(), new(), new(), new(),
            cache(past), cache(past), cache(n_band), cache(n_band),
            pl.BlockSpec((1, N_PAIRS, 2, past + NEW_PAD), lambda b: (b, 0, 0, 0)),
            _resident(tabr.shape),
        ],
        out_specs=[new(), new()],
        out_shape=[jax.ShapeDtypeStruct((n_tok, WIDTH), BF16)] * 2,
        scratch_shapes=[pltpu.VMEM((2 * N_PAIRS, NEW_PAD, PAIR), BF16),
                        pltpu.VMEM((2 * N_PAIRS, NEW_PAD, PAIR), BF16)],
        compiler_params=_params(),
        name="sample_attn",
    )(qa, ka, va, qb, kb, vb, cakt, cavt, cbkt, cbvt, c_all, tabr)


def _post_kernel(x_ref, oa_ref, ob_ref, ga_ref, gb_ref, wo_ref, gpost_ref, gpre_ref,
                 wg_ref, wu_ref, wd_ref, gffn_ref, y_ref, *, n_sub):
    sub = x_ref.shape[0] // n_sub
    rows = [slice(i * sub, (i + 1) * sub) for i in range(n_sub)]
    st = [dict() for _ in range(n_sub)]

    def norm(i):
        st[i]["na"] = _rms(oa_ref[rows[i], :].astype(F32), ga_ref[...]).astype(BF16)
        st[i]["nb"] = _rms(ob_ref[rows[i], :].astype(F32), gb_ref[...]).astype(BF16)

    def mix(i):
        st[i]["mix"] = _dot(st[i]["na"], wo_ref[0:WIDTH, :]) + _dot(st[i]["nb"], wo_ref[WIDTH:2 * WIDTH, :])

    def mid(i):
        x1 = x_ref[rows[i], :] + _rms(st[i]["mix"], gpost_ref[...])
        st[i]["x1"] = x1
        st[i]["hf"] = _rms(x1, gpre_ref[...]).astype(BF16)

    def gate_up(i):
        st[i]["g"] = _dot(st[i]["hf"], wg_ref[...])
        st[i]["u"] = _dot(st[i]["hf"], wu_ref[...])

    def silu(i):
        g = st[i]["g"]
        st[i]["a"] = (g * jax.nn.sigmoid(g) * st[i]["u"]).astype(BF16)

    def down(i):
        st[i]["ff"] = _dot(st[i]["a"], wd_ref[...])

    def fin(i):
        y_ref[rows[i], :] = st[i]["x1"] + _rms(st[i]["ff"], gffn_ref[...])

    vpu_stages = [norm, mid, silu, fin]
    mxu_stages = [mix, gate_up, down]
    norm(0)
    for k, mxu in enumerate(mxu_stages):
        for i in range(n_sub):
            mxu(i)
            if i + 1 < n_sub:
                vpu_stages[k](i + 1)
            else:
                vpu_stages[k + 1](0)
    for i in range(1, n_sub):
        fin(i)


def _post(x, oa, ob, ga, gb, wo, gpost, gpre, wg, wu, wd, gffn, *, tm, n_sub):
    n_tok, d_model = x.shape
    row = lambda w: pl.BlockSpec((tm, w), lambda i: (i, 0))
    return pl.pallas_call(
        functools.partial(_post_kernel, n_sub=n_sub),
        grid=(n_tok // tm,),
        in_specs=[
            row(d_model), row(WIDTH), row(WIDTH),
            _resident(ga.shape), _resident(gb.shape), _resident(wo.shape),
            _resident(gpost.shape), _resident(gpre.shape),
            _resident(wg.shape), _resident(wu.shape), _resident(wd.shape), _resident(gffn.shape),
        ],
        out_specs=row(d_model),
        out_shape=jax.ShapeDtypeStruct((n_tok, d_model), F32),
        compiler_params=_params(),
        name="post",
    )(x, oa, ob, ga, gb, wo, gpost, gpre, wg, wu, wd, gffn)


def _row(v):
    return v.reshape(1, -1).astype(F32)


def _state_from_feature_major(t, batch, n):
    return t.reshape(batch, N_HEADS, HEAD_DIM, n).transpose(0, 3, 1, 2)


def _cache_feature_major(c):
    b, n = c.shape[0], c.shape[1]
    return c.transpose(0, 2, 3, 1).reshape(b, WIDTH, n)


def kernel(x_prompt, x_sample, cache_a_k, cache_a_v, cache_a_logf, cache_b_k, cache_b_v, norm_mix_pre, w_in, b_forget, rel_bias, gain_out_a, gain_out_b, w_out, norm_mix_post, norm_ffn_pre, w_gate, w_up, w_down, norm_ffn_post):
    batch, seq, d_model = x_prompt.shape
    dec_batch, dec_seq, _ = x_sample.shape
    depth = w_in.shape[0]
    past = cache_a_k.shape[2]
    n_band = cache_b_k.shape[2]
    keep = min(LEFT_REACH, seq)
    assert seq % Q_TILE == 0 and keep == LEFT_REACH and n_band == LEFT_REACH and dec_seq <= CHUNK
    n_tok_p, n_tok_s = batch * seq, dec_batch * dec_seq
    tm_p = keep

    xp = x_prompt.reshape(n_tok_p, d_model)
    xs = x_sample.reshape(n_tok_s, d_model)
    outs = [[] for _ in range(10)]
    fa0, fa1 = 3 * WIDTH, 3 * WIDTH + N_HEADS
    for l in range(depth):
        wl = w_in[l]
        qa_w, ka_w, va_w = wl[:, 0:WIDTH], wl[:, WIDTH:2 * WIDTH], wl[:, 2 * WIDTH:fa0]
        f_w = wl[:, fa0:fa1]
        qb_w, kb_w, vb_w = wl[:, fa1:fa1 + WIDTH], wl[:, fa1 + WIDTH:fa1 + 2 * WIDTH], wl[:, fa1 + 2 * WIDTH:]
        w_cat = jnp.concatenate(
            [qa_w, ka_w, va_w, qb_w, kb_w, vb_w, jnp.pad(f_w, ((0, 0), (0, PAIR - N_HEADS)))], axis=1).astype(BF16)
        w_t = jnp.concatenate(
            [ka_w, va_w, kb_w, vb_w, qa_w, qb_w, jnp.pad(f_w, ((0, 0), (0, N_HEADS)))], axis=1).T.astype(BF16)
        b_col = b_forget[l].reshape(N_HEADS, 1).astype(F32)
        rb_pad = jnp.pad(rel_bias[l].astype(F32), ((0, 16 - N_HEADS), (0, 384 - (2 * REL_CLIP + 1))))
        tail = (_row(gain_out_a[l]), _row(gain_out_b[l]), w_out[l].astype(BF16), _row(norm_mix_post[l]),
                _row(norm_ffn_pre[l]), w_gate[l].astype(BF16), w_up[l].astype(BF16), w_down[l].astype(BF16),
                _row(norm_ffn_post[l]))
        tabm, tabr = _bias_tables(rb_pad, n_new=dec_seq)

        qat, qbt, kat, vat, kbt, vbt, kat32, vat32, logf_t, kbt32, vbt32 = _inproj_t(
            xp, _row(norm_mix_pre[l]), w_t, b_col, batch=batch, seq=seq, tm=tm_p)
        c2, kf = _cumsum(logf_t, batch=batch, seq=seq)
        oa = _fox(qat, kat, vat, c2.reshape(N_PAIRS, 2, n_tok_p), kf, batch=batch, seq=seq)
        ob = _band(qbt, kbt, vbt, tabm, batch=batch, seq=seq)
        xp = _post(xp, oa, ob, *tail, tm=512, n_sub=2)
        outs[0].append(_state_from_feature_major(kat32, batch, seq))
        outs[1].append(_state_from_feature_major(vat32, batch, seq))
        outs[2].append(logf_t.T.reshape(batch, seq, N_HEADS))
        outs[3].append(_state_from_feature_major(kbt32, batch, keep))
        outs[4].append(_state_from_feature_major(vbt32, batch, keep))

        qa, ka, va, qb, kb, vb, ka32, va32, logf_t, kb32, vb32 = _inproj(xs, _row(norm_mix_pre[l]), w_cat, b_col)
        lf_all = jnp.concatenate(
            [cache_a_logf[l].astype(F32).transpose(0, 2, 1),
             logf_t.reshape(N_HEADS, dec_batch, dec_seq).transpose(1, 0, 2),
             jnp.zeros((dec_batch, N_HEADS, NEW_PAD - dec_seq), F32)], axis=2)
        c_all = _rows_cumsum(lf_all.reshape(dec_batch * N_HEADS, past + NEW_PAD))
        oa, ob = _sample_attn(
            qa, ka, va, qb, kb, vb,
            _cache_feature_major(cache_a_k[l]), _cache_feature_major(cache_a_v[l]),
            _cache_feature_major(cache_b_k[l]), _cache_feature_major(cache_b_v[l]),
            c_all.reshape(dec_batch, N_PAIRS, 2, past + NEW_PAD), tabr, batch=dec_batch, n_q=dec_seq)
        xs = _post(xs, oa, ob, *tail, tm=n_tok_s, n_sub=1)
        outs[5].append(ka32.reshape(dec_batch, dec_seq, N_HEADS, HEAD_DIM))
        outs[6].append(va32.reshape(dec_batch, dec_seq, N_HEADS, HEAD_DIM))
        outs[7].append(logf_t.T.reshape(dec_batch, dec_seq, N_HEADS))
        outs[8].append(kb32.reshape(dec_batch, dec_seq, N_HEADS, HEAD_DIM))
        outs[9].append(vb32.reshape(dec_batch, dec_seq, N_HEADS, HEAD_DIM))

    st = jnp.stack
    return (xp.reshape(batch, seq, d_model), xs.reshape(dec_batch, dec_seq, d_model),
            *[st(o) for o in outs])
```

```python
import functools

import jax
import jax.numpy as jnp
from jax import lax
from jax.experimental import pallas as pl
from jax.experimental.pallas import tpu as pltpu

F32 = jnp.float32
BF16 = jnp.bfloat16

HEAD_DIM = 64
N_HEADS = 8
WIDTH = N_HEADS * HEAD_DIM
PAIR = 2 * HEAD_DIM
N_PAIRS = N_HEADS // 2
CHUNK = 64
LEFT_CHUNKS = 8
LEFT_REACH = LEFT_CHUNKS * CHUNK
REL_CLIP = 128
EPS = 1e-6
SCALE = HEAD_DIM ** -0.5
LOG2E = 1.4426950408889634
Q_TILE = 256
BAND_KEYS = LEFT_REACH + Q_TILE
TOEPLITZ = 1024
MASKED = -1e30
NEW_PAD = 128
VMEM_LIMIT = 56 * 1024 * 1024


def _params():
    return pltpu.CompilerParams(vmem_limit_bytes=VMEM_LIMIT)


def _resident(shape):
    return pl.BlockSpec(shape, lambda *_: (0,) * len(shape), pipeline_mode=pl.Buffered(1))


def _rms(x, g):
    ms = jnp.mean(x * x, axis=-1, keepdims=True)
    return x * lax.rsqrt(ms + EPS) * g


def _log_sigmoid(x):
    return jnp.minimum(x, 0.0) - jnp.log1p(jnp.exp(-jnp.abs(x)))


def _dot(a, b):
    return jnp.dot(a, b, preferred_element_type=F32)


def _dot_nt(a, b):
    return lax.dot_general(a, b, (((1,), (1,)), ((), ())), preferred_element_type=F32)


def _inproj_kernel(x_ref, g_ref, w_ref, bf_ref,
                   qa_ref, ka_ref, va_ref, qb_ref, kb_ref, vb_ref,
                   ka32_ref, va32_ref, logf_ref, kb32_ref, vb32_ref):
    h = _rms(x_ref[...], g_ref[...]).astype(BF16)

    def proj(c):
        return _dot(h, w_ref[:, c * WIDTH:(c + 1) * WIDTH])

    qa_ref[...] = (proj(0) * SCALE).astype(BF16)
    z = proj(1)
    ka32_ref[...] = z
    ka_ref[...] = z.astype(BF16)
    z = proj(2)
    va32_ref[...] = z
    va_ref[...] = z.astype(BF16)
    qb_ref[...] = (proj(3) * SCALE).astype(BF16)
    z = proj(4)
    kb32_ref[...] = z
    kb_ref[...] = z.astype(BF16)
    z = proj(5)
    vb32_ref[...] = z
    vb_ref[...] = z.astype(BF16)
    f = _dot(h, w_ref[:, 6 * WIDTH:6 * WIDTH + PAIR])
    logf_ref[...] = _log_sigmoid(f.T[0:N_HEADS, :] + bf_ref[...])


def _inproj(x, gain, w_cat, b_col):
    n_tok, d_model = x.shape
    full = lambda w: pl.BlockSpec((n_tok, w), lambda i: (0, 0))
    out_shape = (
        [jax.ShapeDtypeStruct((n_tok, WIDTH), BF16)] * 6
        + [jax.ShapeDtypeStruct((n_tok, WIDTH), F32)] * 2
        + [jax.ShapeDtypeStruct((N_HEADS, n_tok), F32)]
        + [jax.ShapeDtypeStruct((n_tok, WIDTH), F32)] * 2
    )
    out_specs = ([full(WIDTH) for _ in range(8)]
                 + [pl.BlockSpec((N_HEADS, n_tok), lambda i: (0, 0)), full(WIDTH), full(WIDTH)])
    return pl.pallas_call(
        _inproj_kernel,
        grid=(1,),
        in_specs=[full(d_model), _resident((1, d_model)), _resident(w_cat.shape), _resident((N_HEADS, 1))],
        out_specs=out_specs,
        out_shape=out_shape,
        compiler_params=_params(),
        name="inproj_sample",
    )(x, gain, w_cat, b_col)


def _inproj_t_kernel(x_ref, g_ref, wt_ref, bf_ref,
                     qat_ref, qbt_ref, kat_ref, vat_ref, kbt_ref, vbt_ref,
                     kat32_ref, vat32_ref, logf_ref, kbt32_ref, vbt32_ref, *, tiles_per_seq):
    n_sub = 2
    sub = x_ref.shape[0] // n_sub
    hs, kept = {}, []

    def norm(i):
        hs[i] = _rms(x_ref[i * sub:(i + 1) * sub, :], g_ref[...]).astype(BF16)

    def project(i, after_first=None):
        cols = slice(i * sub, (i + 1) * sub)

        def proj_t(c, rows=WIDTH):
            return _dot_nt(wt_ref[c * WIDTH:c * WIDTH + rows, :], hs[i])

        z = proj_t(0)
        kat32_ref[0, :, cols] = z
        kat_ref[0, :, cols] = z.astype(BF16)
        if after_first is not None:
            after_first()
        z = proj_t(1)
        vat32_ref[0, :, cols] = z
        vat_ref[0, :, cols] = z.astype(BF16)
        zk = proj_t(2)
        kbt_ref[0, :, cols] = zk.astype(BF16)
        zv = proj_t(3)
        vbt_ref[0, :, cols] = zv.astype(BF16)
        qat_ref[0, :, cols] = (proj_t(4) * (SCALE * LOG2E)).astype(BF16)
        zq = proj_t(5, WIDTH + 2 * N_HEADS)
        qbt_ref[0, :, cols] = (zq[0:WIDTH] * (SCALE * LOG2E)).astype(BF16)
        logf_ref[0, :, cols] = _log_sigmoid(zq[WIDTH:WIDTH + N_HEADS] + bf_ref[...])
        kept.append((cols, zk, zv))

    norm(0)
    for i in range(n_sub):
        project(i, (lambda j=i + 1: norm(j)) if i + 1 < n_sub else None)

    @pl.when(pl.program_id(0) % tiles_per_seq == tiles_per_seq - 1)
    def _():
        for cols, zk, zv in kept:
            kbt32_ref[0, :, cols] = zk
            vbt32_ref[0, :, cols] = zv


def _inproj_t(x, gain, wt, b_col, *, batch, seq, tm):
    n_tok, d_model = x.shape
    tps = seq // tm
    feat = lambda: pl.BlockSpec((1, WIDTH, tm), lambda i: (i // tps, 0, i % tps))
    keep = lambda: pl.BlockSpec((1, WIDTH, tm), lambda i: (i // tps, 0, 0))
    out_shape = (
        [jax.ShapeDtypeStruct((batch, WIDTH, seq), BF16)] * 6
        + [jax.ShapeDtypeStruct((batch, WIDTH, seq), F32)] * 2
        + [jax.ShapeDtypeStruct((batch, N_HEADS, seq), F32)]
        + [jax.ShapeDtypeStruct((batch, WIDTH, tm), F32)] * 2
    )
    out_specs = ([feat() for _ in range(8)]
                 + [pl.BlockSpec((1, N_HEADS, tm), lambda i: (i // tps, 0, i % tps)), keep(), keep()])
    return pl.pallas_call(
        functools.partial(_inproj_t_kernel, tiles_per_seq=tps),
        grid=(n_tok // tm,),
        in_specs=[
            pl.BlockSpec((tm, d_model), lambda i: (i, 0)),
            _resident((1, d_model)),
            _resident(wt.shape),
            _resident((N_HEADS, 1)),
        ],
        out_specs=out_specs,
        out_shape=out_shape,
        compiler_params=_params(),
        name="inproj",
    )(x, gain, wt, b_col)


def _lane_cumsum(x):
    n = x.shape[-1]
    lane = lax.broadcasted_iota(jnp.int32, x.shape, x.ndim - 1)
    shift = 1
    while shift < n:
        x = x + jnp.where(lane >= shift, pltpu.roll(x, shift, axis=x.ndim - 1), 0.0)
        shift *= 2
    return x


def _split3(x):
    hi = x.astype(BF16).astype(F32)
    rest = x - hi
    mid = rest.astype(BF16).astype(F32)
    return hi, mid, rest - mid


def _cumsum_kernel(lf_ref, c2_ref, kf_ref):
    c2 = _lane_cumsum(lf_ref[...]) * LOG2E
    c2_ref[...] = c2
    seq = c2.shape[1]
    hi, mid, lo = _split3(c2)
    pad = jnp.zeros((PAIR - 3 * N_HEADS, seq), F32)
    for b in range(kf_ref.shape[0]):
        rows = slice(b * N_HEADS, (b + 1) * N_HEADS)
        feat = jnp.concatenate([hi[rows], mid[rows], lo[rows], pad], axis=0)
        kf_ref[b] = feat.T.astype(BF16)


def _cumsum(logf, *, batch, seq):
    return pl.pallas_call(
        _cumsum_kernel,
        out_shape=[
            jax.ShapeDtypeStruct((batch * N_HEADS, seq), F32),
            jax.ShapeDtypeStruct((batch, seq, PAIR), BF16),
        ],
        compiler_params=_params(),
        name="cumsum",
    )(logf)


def _split_heads_q(q):
    lane = lax.broadcasted_iota(jnp.int32, q.shape, 1)
    zero = jnp.zeros_like(q)
    return jnp.concatenate([jnp.where(lane < HEAD_DIM, q, zero),
                            jnp.where(lane >= HEAD_DIM, q, zero)], axis=0)


def _augment_v(v, h, axis):
    idx = lax.broadcasted_iota(jnp.int32, v.shape, axis)
    own = (idx < HEAD_DIM) if h == 0 else (idx >= HEAD_DIM)
    return jnp.where(own, v, jnp.ones_like(v))


def _merge_heads_out(acc0, acc1):
    lane = lax.broadcasted_iota(jnp.int32, acc0.shape, 1)
    first = lane < HEAD_DIM
    num = jnp.where(first, acc0, acc1)
    den = pltpu.roll(jnp.where(first, acc1, acc0), HEAD_DIM, axis=1)
    return num / den


def _pair_queries(qt):
    n = qt.shape[1]
    q2 = jnp.concatenate([qt, qt], axis=1)
    r = lax.broadcasted_iota(jnp.int32, q2.shape, 0)
    c = lax.broadcasted_iota(jnp.int32, q2.shape, 1)
    own = jnp.right_shift(r, 6) == jnp.where(c >= n, 1, 0)
    return jnp.where(own, q2, jnp.zeros_like(q2))


def _merge_heads_out_t(acc0, acc1):
    first = lax.broadcasted_iota(jnp.int32, acc0.shape, 0) < HEAD_DIM
    num = jnp.where(first, acc0, acc1)
    den = jnp.where(first, acc0[HEAD_DIM:HEAD_DIM + 1, :], acc1[0:1, :])
    return (num / den).T


def _weighted_values(vaug, p_blocks, key_ranges):
    accs = []
    for h in range(2):
        cols = slice(h * Q_TILE, (h + 1) * Q_TILE)
        acc = None
        for p, (k0, k1) in zip(p_blocks, key_ranges):
            part = _dot(vaug[h][:, k0:k1], p[:, cols])
            acc = part if acc is None else acc + part
        accs.append(acc)
    return accs


def _fox_kernel(qt_ref, kt_ref, vt_ref, c2_ref, kf_ref, o_ref, v0_ref, v1_ref, kx_ref, *, seq):
    pair = pl.program_id(1)
    vt = vt_ref[0]
    v0_ref[...] = _augment_v(vt, 0, 0)
    v1_ref[...] = _augment_v(vt, 1, 0)
    vaug = (v0_ref, v1_ref)

    kx_ref[:, 0:PAIR] = kt_ref[0].astype(F32).T.astype(BF16)
    kx_ref[:, PAIR:2 * PAIR] = kf_ref[0]
    er = lax.broadcasted_iota(jnp.int32, (PAIR, 2 * Q_TILE), 0)
    ec = lax.broadcasted_iota(jnp.int32, (PAIR, 2 * Q_TILE), 1)
    head = 2 * pair + jnp.where(ec >= Q_TILE, 1, 0)
    piece = er - head
    coef = jnp.where((piece == 0) | (piece == N_HEADS) | (piece == 2 * N_HEADS), -1.0, 0.0).astype(BF16)
    key = lax.broadcasted_iota(jnp.int32, (Q_TILE, 2 * Q_TILE), 0)
    qry = jnp.bitwise_and(lax.broadcasted_iota(jnp.int32, (Q_TILE, 2 * Q_TILE), 1), Q_TILE - 1)
    causal = key <= qry

    n_qb = seq // Q_TILE

    def scores(qb):
        qs, ke = qb * Q_TILE, (qb + 1) * Q_TILE
        rhs = jnp.concatenate([_pair_queries(qt_ref[0, :, qs:ke]), coef], axis=0)
        return _dot(kx_ref[0:ke, :], rhs)

    def softmax(qb, s):
        qs, ke = qb * Q_TILE, (qb + 1) * Q_TILE
        cq = jnp.concatenate([c2_ref[0, 0, 0:1, qs:ke], c2_ref[0, 0, 1:2, qs:ke]], axis=1)
        t_diag = jnp.where(causal, s[qs:ke, :], -jnp.inf)
        m = jnp.max(t_diag, axis=0, keepdims=True)
        if qb > 0:
            t_past = s[0:qs, :]
            m = jnp.maximum(m, jnp.max(t_past, axis=0, keepdims=True))
        shift = cq - (m + cq)
        p_blocks = [jnp.exp2(t_diag + shift).astype(BF16)]
        key_ranges = [(qs, ke)]
        if qb > 0:
            p_blocks.append(jnp.exp2(t_past + shift).astype(BF16))
            key_ranges.append((0, qs))
        return p_blocks, key_ranges

    def finish(qb, p_blocks, key_ranges):
        acc0, acc1 = _weighted_values(vaug, p_blocks, key_ranges)
        o_ref[qb * Q_TILE:(qb + 1) * Q_TILE, :] = _merge_heads_out_t(acc0, acc1).astype(o_ref.dtype)

    order = list(range(n_qb - 1, -1, -1))
    s_next = scores(order[0])
    pending = None
    for i, qb in enumerate(order):
        s = s_next
        if i + 1 < n_qb:
            s_next = scores(order[i + 1])
        probs = softmax(qb, s)
        if pending is not None:
            finish(*pending)
        pending = (qb, *probs)
    finish(*pending)


def _fox(qt, kt, vt, c2, kf, *, batch, seq):
    n_tok = batch * seq
    feat = lambda: pl.BlockSpec((1, PAIR, seq), lambda b, p: (b, p, 0))
    return pl.pallas_call(
        functools.partial(_fox_kernel, seq=seq),
        grid=(batch, N_PAIRS),
        in_specs=[
            feat(), feat(), feat(),
            pl.BlockSpec((1, 1, 2, seq), lambda b, p: (b, p, 0, 0)),
            pl.BlockSpec((1, seq, PAIR), lambda b, p: (b, 0, 0)),
        ],
        out_specs=pl.BlockSpec((seq, PAIR), lambda b, p: (b, p)),
        out_shape=jax.ShapeDtypeStruct((n_tok, WIDTH), BF16),
        scratch_shapes=[pltpu.VMEM((PAIR, seq), BF16), pltpu.VMEM((PAIR, seq), BF16),
                        pltpu.VMEM((seq, 2 * PAIR), BF16)],
        compiler_params=_params(),
        name="fox",
    )(qt, kt, vt, c2, kf)


def _bias_kernel(rb_ref, tabm_ref, tabr_ref):
    rb = rb_ref[...]
    n_rows = rb.shape[1]
    r = lax.broadcasted_iota(jnp.int32, (n_rows, TOEPLITZ), 0)
    m = lax.broadcasted_iota(jnp.int32, (n_rows, TOEPLITZ), 1)
    e = jnp.where(m < TOEPLITZ // 2, m, m - TOEPLITZ)
    idx = jnp.clip(LEFT_REACH - Q_TILE - e, -REL_CLIP, REL_CLIP) + REL_CLIP
    onehot = jnp.where(r == idx, 1.0, 0.0).astype(BF16)
    hi = rb.astype(BF16)
    rest = rb - hi.astype(F32)
    mid = rest.astype(BF16)
    lo = (rest - mid.astype(F32)).astype(BF16)
    u = _dot(hi, onehot) + _dot(mid, onehot) + _dot(lo, onehot)

    i = lax.broadcasted_iota(jnp.int32, (Q_TILE, BAND_KEYS), 0)
    j = lax.broadcasted_iota(jnp.int32, (Q_TILE, BAND_KEYS), 1)
    qc = jnp.right_shift(i, 6)
    kc = jnp.right_shift(j, 6)
    valid = (kc >= qc) & (kc <= qc + LEFT_CHUNKS)
    n_new = tabr_ref.shape[1]
    for h in range(N_HEADS):
        base = jnp.broadcast_to(u[h:h + 1, :], (Q_TILE, TOEPLITZ))
        t = pltpu.roll(base, Q_TILE, axis=1, stride=1, stride_axis=0)[:, 0:BAND_KEYS]
        half = (h % 2) * Q_TILE
        tabm_ref[h // 2, :, half:half + Q_TILE] = jnp.where(valid, t * LOG2E, MASKED).T
        tabr_ref[h] = t[0:n_new, :]


def _bias_tables(rb_pad, *, n_new):
    return pl.pallas_call(
        _bias_kernel,
        out_shape=[
            jax.ShapeDtypeStruct((N_PAIRS, BAND_KEYS, 2 * Q_TILE), F32),
            jax.ShapeDtypeStruct((N_HEADS, n_new, BAND_KEYS), F32),
        ],
        compiler_params=_params(),
        name="bias_tables",
    )(rb_pad)


def _band_kernel(qt_ref, kt_ref, vt_ref, tab_ref, o_ref, v0_ref, v1_ref, k_ref, *, seq):
    vt = vt_ref[0]
    v0_ref[...] = _augment_v(vt, 0, 0)
    v1_ref[...] = _augment_v(vt, 1, 0)
    vaug = (v0_ref, v1_ref)
    k_ref[...] = kt_ref[0].astype(F32).T.astype(BF16)

    n_qb = seq // Q_TILE

    def scores(qb):
        qs, ke = qb * Q_TILE, (qb + 1) * Q_TILE
        ks = max(0, qs - LEFT_REACH)
        return _dot(k_ref[ks:ke, :], _pair_queries(qt_ref[0, :, qs:ke]))

    def softmax(qb, s):
        n_keys = s.shape[0]
        t = s + tab_ref[0, BAND_KEYS - n_keys:BAND_KEYS, :]
        m = jnp.max(t, axis=0, keepdims=True)
        return jnp.exp2(t - m).astype(BF16)

    def finish(qb, p):
        ke = (qb + 1) * Q_TILE
        acc0, acc1 = _weighted_values(vaug, [p], [(ke - p.shape[0], ke)])
        o_ref[qb * Q_TILE:ke, :] = _merge_heads_out_t(acc0, acc1).astype(o_ref.dtype)

    order = list(range(n_qb - 1, -1, -1))
    s_next = scores(order[0])
    pending = None
    for i, qb in enumerate(order):
        s = s_next
        if i + 1 < n_qb:
            s_next = scores(order[i + 1])
        p = softmax(qb, s)
        if pending is not None:
            finish(*pending)
        pending = (qb, p)
    finish(*pending)


def _band(qt, kt, vt, tabm, *, batch, seq):
    n_tok = batch * seq
    feat = lambda: pl.BlockSpec((1, PAIR, seq), lambda p, b: (b, p, 0))
    return pl.pallas_call(
        functools.partial(_band_kernel, seq=seq),
        grid=(N_PAIRS, batch),
        in_specs=[
            feat(), feat(), feat(),
            pl.BlockSpec((1, BAND_KEYS, 2 * Q_TILE), lambda p, b: (p, 0, 0)),
        ],
        out_specs=pl.BlockSpec((seq, PAIR), lambda p, b: (b, p)),
        out_shape=jax.ShapeDtypeStruct((n_tok, WIDTH), BF16),
        scratch_shapes=[pltpu.VMEM((PAIR, seq), BF16), pltpu.VMEM((PAIR, seq), BF16),
                        pltpu.VMEM((seq, PAIR), BF16)],
        compiler_params=_params(),
        name="band",
    )(qt, kt, vt, tabm)


def _two_part_attention(q2, kt_cache, vt_cache, kn_ref, vn_ref, bias_cache, bias_new, n_q):
    s_c = _dot(q2, kt_cache)
    s_n = _dot_nt(q2, kn_ref[...])
    accs = []
    for h in range(2):
        rows = slice(h * n_q, (h + 1) * n_q)
        t_c = s_c[rows, :] + bias_cache(h)
        t_n = s_n[rows, :] + bias_new(h)
        m = jnp.maximum(jnp.max(t_c, axis=1, keepdims=True), jnp.max(t_n, axis=1, keepdims=True))
        acc = _dot_nt(jnp.exp(t_c - m).astype(BF16), _augment_v(vt_cache, h, 0))
        acc = acc + _dot(jnp.exp(t_n - m).astype(BF16), _augment_v(vn_ref[...], h, 1))
        accs.append(acc)
    return _merge_heads_out(accs[0], accs[1])


def _sample_attn_kernel(qa_ref, ka_ref, va_ref, qb_ref, kb_ref, vb_ref,
                        cak_ref, cav_ref, cbk_ref, cbv_ref, c_ref, tab_ref,
                        oa_ref, ob_ref, kn_ref, vn_ref, *, n_q):
    row = lax.broadcasted_iota(jnp.int32, (n_q, NEW_PAD), 0)
    col = lax.broadcasted_iota(jnp.int32, (n_q, NEW_PAD), 1)
    n_cache = cak_ref.shape[2]
    n_band = cbk_ref.shape[2]

    def stage_new(slot, k_new, v_new):
        kn_ref[slot] = jnp.zeros((NEW_PAD, PAIR), BF16)
        vn_ref[slot] = jnp.zeros((NEW_PAD, PAIR), BF16)
        kn_ref[slot, 0:n_q, :] = k_new
        vn_ref[slot, 0:n_q, :] = v_new
        return kn_ref.at[slot], vn_ref.at[slot]

    for p in range(N_PAIRS):
        feats = slice(p * PAIR, (p + 1) * PAIR)

        c_cache = c_ref[0, p, :, 0:n_cache]
        c_new = c_ref[0, p, :, n_cache:n_cache + NEW_PAD]

        def fox_cache(h, c_cache=c_cache, c_new=c_new):
            cq = jnp.sum(jnp.where(col == row, c_new[h:h + 1, :], 0.0), axis=1, keepdims=True)
            return cq - c_cache[h:h + 1, :]

        def fox_new(h, c_new=c_new):
            cq = jnp.sum(jnp.where(col == row, c_new[h:h + 1, :], 0.0), axis=1, keepdims=True)
            return jnp.where(col <= row, cq - c_new[h:h + 1, :], -jnp.inf)

        kn, vn = stage_new(2 * p, ka_ref[:, feats], va_ref[:, feats])
        oa = _two_part_attention(_split_heads_q(qa_ref[:, feats]), cak_ref[0, feats, :].astype(BF16),
                                 cav_ref[0, feats, :].astype(BF16), kn, vn, fox_cache, fox_new, n_q)
        oa_ref[:, feats] = oa.astype(oa_ref.dtype)

        def band_cache(h, p=p):
            return tab_ref[2 * p + h, :, 0:n_band]

        def band_new(h, p=p):
            return jnp.where(col < n_q, tab_ref[2 * p + h, :, n_band:n_band + NEW_PAD], -jnp.inf)

        kn, vn = stage_new(2 * p + 1, kb_ref[:, feats], vb_ref[:, feats])
        ob = _two_part_attention(_split_heads_q(qb_ref[:, feats]), cbk_ref[0, feats, :].astype(BF16),
                                 cbv_ref[0, feats, :].astype(BF16), kn, vn, band_cache, band_new, n_q)
        ob_ref[:, feats] = ob.astype(ob_ref.dtype)


def _rows_cumsum_kernel(x_ref, c_ref):
    c_ref[...] = _lane_cumsum(x_ref[...])


def _rows_cumsum(x):
    return pl.pallas_call(
        _rows_cumsum_kernel,
        out_shape=jax.ShapeDtypeStruct(x.shape, F32),
        compiler_params=_params(),
        name="rows_cumsum",
    )(x)


def _sample_attn(qa, ka, va, qb, kb, vb, cakt, cavt, cbkt, cbvt, c_all, tabr, *, batch, n_q):
    n_tok = batch * n_q
    past = cakt.shape[2]
    n_band = cbkt.shape[2]
    new = lambda: pl.BlockSpec((n_q, WIDTH), lambda b: (b, 0))
    cache = lambda n: pl.BlockSpec((1, WIDTH, n), lambda b: (b, 0, 0))
    return pl.pallas_call(
        functools.partial(_sample_attn_kernel, n_q=n_q),
        grid=(batch,),
        in_specs=[
            new(), new(), new(), new(), new(), new(),
            cache(past), cache(past), cache(n_band), cache(n_band),
            pl.BlockSpec((1, N_PAIRS, 2, past + NEW_PAD), lambda b: (b, 0, 0, 0)),
            _resident(tabr.shape),
        ],
        out_specs=[new(), new()],
        out_shape=[jax.ShapeDtypeStruct((n_tok, WIDTH), BF16)] * 2,
        scratch_shapes=[pltpu.VMEM((2 * N_PAIRS, NEW_PAD, PAIR), BF16),
                        pltpu.VMEM((2 * N_PAIRS, NEW_PAD, PAIR), BF16)],
        compiler_params=_params(),
        name="sample_attn",
    )(qa, ka, va, qb, kb, vb, cakt, cavt, cbkt, cbvt, c_all, tabr)


def _post_kernel(x_ref, oa_ref, ob_ref, ga_ref, gb_ref, wo_ref, gpost_ref, gpre_ref,
                 wg_ref, wu_ref, wd_ref, gffn_ref, y_ref, *, n_sub):
    sub = x_ref.shape[0] // n_sub
    rows = [slice(i * sub, (i + 1) * sub) for i in range(n_sub)]
    st = [dict() for _ in range(n_sub)]

    def norm(i):
        st[i]["na"] = _rms(oa_ref[rows[i], :].astype(F32), ga_ref[...]).astype(BF16)
        st[i]["nb"] = _rms(ob_ref[rows[i], :].astype(F32), gb_ref[...]).astype(BF16)

    def mix(i):
        st[i]["mix"] = _dot(st[i]["na"], wo_ref[0:WIDTH, :]) + _dot(st[i]["nb"], wo_ref[WIDTH:2 * WIDTH, :])

    def mid(i):
        x1 = x_ref[rows[i], :] + _rms(st[i]["mix"], gpost_ref[...])
        st[i]["x1"] = x1
        st[i]["hf"] = _rms(x1, gpre_ref[...]).astype(BF16)

    def gate_up(i):
        st[i]["g"] = _dot(st[i]["hf"], wg_ref[...])
        st[i]["u"] = _dot(st[i]["hf"], wu_ref[...])

    def silu(i):
        g = st[i]["g"]
        st[i]["a"] = (g * jax.nn.sigmoid(g) * st[i]["u"]).astype(BF16)

    def down(i):
        st[i]["ff"] = _dot(st[i]["a"], wd_ref[...])

    def fin(i):
        y_ref[rows[i], :] = st[i]["x1"] + _rms(st[i]["ff"], gffn_ref[...])

    vpu_stages = [norm, mid, silu, fin]
    mxu_stages = [mix, gate_up, down]
    norm(0)
    for k, mxu in enumerate(mxu_stages):
        for i in range(n_sub):
            mxu(i)
            if i + 1 < n_sub:
                vpu_stages[k](i + 1)
            else:
                vpu_stages[k + 1](0)
    for i in range(1, n_sub):
        fin(i)


def _post(x, oa, ob, ga, gb, wo, gpost, gpre, wg, wu, wd, gffn, *, tm, n_sub):
    n_tok, d_model = x.shape
    row = lambda w: pl.BlockSpec((tm, w), lambda i: (i, 0))
    return pl.pallas_call(
        functools.partial(_post_kernel, n_sub=n_sub),
        grid=(n_tok // tm,),
        in_specs=[
            row(d_model), row(WIDTH), row(WIDTH),
            _resident(ga.shape), _resident(gb.shape), _resident(wo.shape),
            _resident(gpost.shape), _resident(gpre.shape),
            _resident(wg.shape), _resident(wu.shape), _resident(wd.shape), _resident(gffn.shape),
        ],
        out_specs=row(d_model),
        out_shape=jax.ShapeDtypeStruct((n_tok, d_model), F32),
        compiler_params=_params(),
        name="post",
    )(x, oa, ob, ga, gb, wo, gpost, gpre, wg, wu, wd, gffn)


def _row(v):
    return v.reshape(1, -1).astype(F32)


def _state_from_feature_major(t, batch, n):
    return t.reshape(batch, N_HEADS, HEAD_DIM, n).transpose(0, 3, 1, 2)


def _cache_feature_major(c):
    b, n = c.shape[0], c.shape[1]
    return c.transpose(0, 2, 3, 1).reshape(b, WIDTH, n)


def kernel(x_prompt, x_sample, cache_a_k, cache_a_v, cache_a_logf, cache_b_k, cache_b_v, norm_mix_pre, w_in, b_forget, rel_bias, gain_out_a, gain_out_b, w_out, norm_mix_post, norm_ffn_pre, w_gate, w_up, w_down, norm_ffn_post):
    batch, seq, d_model = x_prompt.shape
    dec_batch, dec_seq, _ = x_sample.shape
    depth = w_in.shape[0]
    past = cache_a_k.shape[2]
    n_band = cache_b_k.shape[2]
    keep = min(LEFT_REACH, seq)
    assert seq % Q_TILE == 0 and keep == LEFT_REACH and n_band == LEFT_REACH and dec_seq <= CHUNK
    n_tok_p, n_tok_s = batch * seq, dec_batch * dec_seq
    tm_p = keep

    xp = x_prompt.reshape(n_tok_p, d_model)
    xs = x_sample.reshape(n_tok_s, d_model)
    outs = [[] for _ in range(10)]
    fa0, fa1 = 3 * WIDTH, 3 * WIDTH + N_HEADS
    for l in range(depth):
        wl = w_in[l]
        qa_w, ka_w, va_w = wl[:, 0:WIDTH], wl[:, WIDTH:2 * WIDTH], wl[:, 2 * WIDTH:fa0]
        f_w = wl[:, fa0:fa1]
        qb_w, kb_w, vb_w = wl[:, fa1:fa1 + WIDTH], wl[:, fa1 + WIDTH:fa1 + 2 * WIDTH], wl[:, fa1 + 2 * WIDTH:]
        w_cat = jnp.concatenate(
            [qa_w, ka_w, va_w, qb_w, kb_w, vb_w, jnp.pad(f_w, ((0, 0), (0, PAIR - N_HEADS)))], axis=1).astype(BF16)
        w_t = jnp.concatenate(
            [ka_w, va_w, kb_w, vb_w, qa_w, qb_w, jnp.pad(f_w, ((0, 0), (0, N_HEADS)))], axis=1).T.astype(BF16)
        b_col = b_forget[l].reshape(N_HEADS, 1).astype(F32)
        rb_pad = jnp.pad(rel_bias[l].astype(F32), ((0, 16 - N_HEADS), (0, 384 - (2 * REL_CLIP + 1))))
        tail = (_row(gain_out_a[l]), _row(gain_out_b[l]), w_out[l].astype(BF16), _row(norm_mix_post[l]),
                _row(norm_ffn_pre[l]), w_gate[l].astype(BF16), w_up[l].astype(BF16), w_down[l].astype(BF16),
                _row(norm_ffn_post[l]))
        tabm, tabr = _bias_tables(rb_pad, n_new=dec_seq)

        qat, qbt, kat, vat, kbt, vbt, kat32, vat32, logf_p, kbt32, vbt32 = _inproj_t(
            xp, _row(norm_mix_pre[l]), w_t, b_col, batch=batch, seq=seq, tm=tm_p)
        c2, kf = _cumsum(logf_p.reshape(batch * N_HEADS, seq), batch=batch, seq=seq)
        oa = _fox(qat, kat, vat, c2.reshape(batch, N_PAIRS, 2, seq), kf, batch=batch, seq=seq)
        ob = _band(qbt, kbt, vbt, tabm, batch=batch, seq=seq)
        xp = _post(xp, oa, ob, *tail, tm=512, n_sub=2)
        outs[0].append(_state_from_feature_major(kat32, batch, seq))
        outs[1].append(_state_from_feature_major(vat32, batch, seq))
        outs[2].append(logf_p.transpose(0, 2, 1))
        outs[3].append(_state_from_feature_major(kbt32, batch, keep))
        outs[4].append(_state_from_feature_major(vbt32, batch, keep))

        qa, ka, va, qb, kb, vb, ka32, va32, logf_t, kb32, vb32 = _inproj(xs, _row(norm_mix_pre[l]), w_cat, b_col)
        lf_all = jnp.concatenate(
            [cache_a_logf[l].astype(F32).transpose(0, 2, 1),
             logf_t.reshape(N_HEADS, dec_batch, dec_seq).transpose(1, 0, 2),
             jnp.zeros((dec_batch, N_HEADS, NEW_PAD - dec_seq), F32)], axis=2)
        c_all = _rows_cumsum(lf_all.reshape(dec_batch * N_HEADS, past + NEW_PAD))
        oa, ob = _sample_attn(
            qa, ka, va, qb, kb, vb,
            _cache_feature_major(cache_a_k[l]), _cache_feature_major(cache_a_v[l]),
            _cache_feature_major(cache_b_k[l]), _cache_feature_major(cache_b_v[l]),
            c_all.reshape(dec_batch, N_PAIRS, 2, past + NEW_PAD), tabr, batch=dec_batch, n_q=dec_seq)
        xs = _post(xs, oa, ob, *tail, tm=n_tok_s, n_sub=1)
        outs[5].append(ka32.reshape(dec_batch, dec_seq, N_HEADS, HEAD_DIM))
        outs[6].append(va32.reshape(dec_batch, dec_seq, N_HEADS, HEAD_DIM))
        outs[7].append(logf_t.T.reshape(dec_batch, dec_seq, N_HEADS))
        outs[8].append(kb32.reshape(dec_batch, dec_seq, N_HEADS, HEAD_DIM))
        outs[9].append(vb32.reshape(dec_batch, dec_seq, N_HEADS, HEAD_DIM))

    st = jnp.stack
    return (xp.reshape(batch, seq, d_model), xs.reshape(dec_batch, dec_seq, d_model),
            *[st(o) for o in outs])
```

```python
import functools

import jax
import jax.numpy as jnp
from jax import lax
from jax.experimental import pallas as pl
from jax.experimental.pallas import tpu as pltpu

F32 = jnp.float32
BF16 = jnp.bfloat16

HEAD_DIM = 64
N_HEADS = 8
WIDTH = N_HEADS * HEAD_DIM
PAIR = 2 * HEAD_DIM
N_PAIRS = N_HEADS // 2
CHUNK = 64
LEFT_CHUNKS = 8
LEFT_REACH = LEFT_CHUNKS * CHUNK
REL_CLIP = 128
EPS = 1e-6
SCALE = HEAD_DIM ** -0.5
LOG2E = 1.4426950408889634
Q_TILE = 256
BAND_KEYS = LEFT_REACH + Q_TILE
TOEPLITZ = 1024
MASKED = -1e30
NEW_PAD = 128
VMEM_LIMIT = 56 * 1024 * 1024
W_IN_COLS = {"qa": 0, "ka": WIDTH, "va": 2 * WIDTH, "forget": 3 * WIDTH,
             "qb": 3 * WIDTH + N_HEADS, "kb": 4 * WIDTH + N_HEADS, "vb": 5 * WIDTH + N_HEADS}


def _params():
    return pltpu.CompilerParams(vmem_limit_bytes=VMEM_LIMIT)


def _resident(shape):
    return pl.BlockSpec(shape, lambda *_: (0,) * len(shape), pipeline_mode=pl.Buffered(1))


def _rms(x, g):
    ms = jnp.mean(x * x, axis=-1, keepdims=True)
    return x * lax.rsqrt(ms + EPS) * g


def _log_sigmoid(x):
    return jnp.minimum(x, 0.0) - jnp.log1p(jnp.exp(-jnp.abs(x)))


def _dot(a, b):
    return jnp.dot(a, b, preferred_element_type=F32)


def _dot_nt(a, b):
    return lax.dot_general(a, b, (((1,), (1,)), ((), ())), preferred_element_type=F32)


def _inproj_kernel(x_ref, g_ref, w_ref, bf_ref,
                   qa_ref, ka_ref, va_ref, qb_ref, kb_ref, vb_ref,
                   ka32_ref, va32_ref, logf_ref, kb32_ref, vb32_ref):
    h = _rms(x_ref[...], g_ref[...]).astype(BF16)

    def proj(name, width=WIDTH):
        c0 = W_IN_COLS[name]
        return _dot(h, w_ref[0, :, c0:c0 + width].astype(BF16))

    qa_ref[...] = (proj("qa") * SCALE).astype(BF16)
    z = proj("ka")
    ka32_ref[...] = z
    ka_ref[...] = z.astype(BF16)
    z = proj("va")
    va32_ref[...] = z
    va_ref[...] = z.astype(BF16)
    qb_ref[...] = (proj("qb") * SCALE).astype(BF16)
    z = proj("kb")
    kb32_ref[...] = z
    kb_ref[...] = z.astype(BF16)
    z = proj("vb")
    vb32_ref[...] = z
    vb_ref[...] = z.astype(BF16)
    f = proj("forget", PAIR)
    logf_ref[...] = _log_sigmoid(f.T[0:N_HEADS, :] + bf_ref[...])


def _inproj(x, gain, w_in, layer, b_col):
    n_tok, d_model = x.shape
    full = lambda w: pl.BlockSpec((n_tok, w), lambda i: (0, 0))
    w_spec = pl.BlockSpec((1,) + w_in.shape[1:], lambda i: (layer, 0, 0), pipeline_mode=pl.Buffered(1))
    out_shape = (
        [jax.ShapeDtypeStruct((n_tok, WIDTH), BF16)] * 6
        + [jax.ShapeDtypeStruct((n_tok, WIDTH), F32)] * 2
        + [jax.ShapeDtypeStruct((N_HEADS, n_tok), F32)]
        + [jax.ShapeDtypeStruct((n_tok, WIDTH), F32)] * 2
    )
    out_specs = ([full(WIDTH) for _ in range(8)]
                 + [pl.BlockSpec((N_HEADS, n_tok), lambda i: (0, 0)), full(WIDTH), full(WIDTH)])
    return pl.pallas_call(
        _inproj_kernel,
        grid=(1,),
        in_specs=[full(d_model), _resident((1, d_model)), w_spec, _resident((N_HEADS, 1))],
        out_specs=out_specs,
        out_shape=out_shape,
        compiler_params=_params(),
        name="inproj_sample",
    )(x, gain, w_in, b_col)


def _inproj_t_kernel(x_ref, g_ref, w_ref, bf_ref,
                     qat_ref, qbt_ref, kat_ref, vat_ref, kbt_ref, vbt_ref,
                     kat32_ref, vat32_ref, logf_ref, kbt32_ref, vbt32_ref, wt_ref, *, tiles_per_seq):
    @pl.when(pl.program_id(0) == 0)
    def _():
        for g, name in enumerate(("ka", "va", "kb", "vb", "qa", "qb")):
            c0 = W_IN_COLS[name]
            wt_ref[g * WIDTH:(g + 1) * WIDTH, :] = w_ref[0, :, c0:c0 + WIDTH].T.astype(BF16)
        c0 = W_IN_COLS["forget"]
        ft = w_ref[0, :, c0:c0 + PAIR].T
        live = lax.broadcasted_iota(jnp.int32, (2 * N_HEADS, ft.shape[1]), 0) < N_HEADS
        wt_ref[6 * WIDTH:6 * WIDTH + 2 * N_HEADS, :] = jnp.where(live, ft[0:2 * N_HEADS, :], 0.0).astype(BF16)

    n_sub = 2
    sub = x_ref.shape[0] // n_sub
    hs, kept = {}, []

    def norm(i):
        hs[i] = _rms(x_ref[i * sub:(i + 1) * sub, :], g_ref[...]).astype(BF16)

    def project(i, after_first=None):
        cols = slice(i * sub, (i + 1) * sub)

        def proj_t(c, rows=WIDTH):
            return _dot_nt(wt_ref[c * WIDTH:c * WIDTH + rows, :], hs[i])

        z = proj_t(0)
        kat32_ref[0, :, cols] = z
        kat_ref[0, :, cols] = z.astype(BF16)
        if after_first is not None:
            after_first()
        z = proj_t(1)
        vat32_ref[0, :, cols] = z
        vat_ref[0, :, cols] = z.astype(BF16)
        zk = proj_t(2)
        kbt_ref[0, :, cols] = zk.astype(BF16)
        zv = proj_t(3)
        vbt_ref[0, :, cols] = zv.astype(BF16)
        qat_ref[0, :, cols] = (proj_t(4) * (SCALE * LOG2E)).astype(BF16)
        zq = proj_t(5, WIDTH + 2 * N_HEADS)
        qbt_ref[0, :, cols] = (zq[0:WIDTH] * (SCALE * LOG2E)).astype(BF16)
        logf_ref[0, :, cols] = _log_sigmoid(zq[WIDTH:WIDTH + N_HEADS] + bf_ref[...])
        kept.append((cols, zk, zv))

    norm(0)
    for i in range(n_sub):
        project(i, (lambda j=i + 1: norm(j)) if i + 1 < n_sub else None)

    @pl.when(pl.program_id(0) % tiles_per_seq == tiles_per_seq - 1)
    def _():
        for cols, zk, zv in kept:
            kbt32_ref[0, :, cols] = zk
            vbt32_ref[0, :, cols] = zv


def _inproj_t(x, gain, w_in, layer, b_col, *, batch, seq, tm):
    n_tok, d_model = x.shape
    tps = seq // tm
    w_spec = pl.BlockSpec((1,) + w_in.shape[1:], lambda i: (layer, 0, 0), pipeline_mode=pl.Buffered(1))
    feat = lambda: pl.BlockSpec((1, WIDTH, tm), lambda i: (i // tps, 0, i % tps))
    keep = lambda: pl.BlockSpec((1, WIDTH, tm), lambda i: (i // tps, 0, 0))
    out_shape = (
        [jax.ShapeDtypeStruct((batch, WIDTH, seq), BF16)] * 6
        + [jax.ShapeDtypeStruct((batch, WIDTH, seq), F32)] * 2
        + [jax.ShapeDtypeStruct((batch, N_HEADS, seq), F32)]
        + [jax.ShapeDtypeStruct((batch, WIDTH, tm), F32)] * 2
    )
    out_specs = ([feat() for _ in range(8)]
                 + [pl.BlockSpec((1, N_HEADS, tm), lambda i: (i // tps, 0, i % tps)), keep(), keep()])
    return pl.pallas_call(
        functools.partial(_inproj_t_kernel, tiles_per_seq=tps),
        grid=(n_tok // tm,),
        in_specs=[
            pl.BlockSpec((tm, d_model), lambda i: (i, 0)),
            _resident((1, d_model)),
            w_spec,
            _resident((N_HEADS, 1)),
        ],
        out_specs=out_specs,
        out_shape=out_shape,
        scratch_shapes=[pltpu.VMEM((6 * WIDTH + 2 * N_HEADS, d_model), BF16)],
        compiler_params=_params(),
        name="inproj",
    )(x, gain, w_in, b_col)


def _lane_cumsum(x):
    n = x.shape[-1]
    lane = lax.broadcasted_iota(jnp.int32, x.shape, x.ndim - 1)
    shift = 1
    while shift < n:
        x = x + jnp.where(lane >= shift, pltpu.roll(x, shift, axis=x.ndim - 1), 0.0)
        shift *= 2
    return x


def _split3(x):
    hi = x.astype(BF16).astype(F32)
    rest = x - hi
    mid = rest.astype(BF16).astype(F32)
    return hi, mid, rest - mid


def _cumsum_kernel(lf_ref, c2_ref, kf_ref):
    c2 = _lane_cumsum(lf_ref[...]) * LOG2E
    c2_ref[...] = c2
    seq = c2.shape[1]
    hi, mid, lo = _split3(c2)
    pad = jnp.zeros((PAIR - 3 * N_HEADS, seq), F32)
    for b in range(kf_ref.shape[0]):
        rows = slice(b * N_HEADS, (b + 1) * N_HEADS)
        feat = jnp.concatenate([hi[rows], mid[rows], lo[rows], pad], axis=0)
        kf_ref[b] = feat.T.astype(BF16)


def _cumsum(logf, *, batch, seq):
    return pl.pallas_call(
        _cumsum_kernel,
        out_shape=[
            jax.ShapeDtypeStruct((batch * N_HEADS, seq), F32),
            jax.ShapeDtypeStruct((batch, seq, PAIR), BF16),
        ],
        compiler_params=_params(),
        name="cumsum",
    )(logf)


def _split_heads_q(q):
    lane = lax.broadcasted_iota(jnp.int32, q.shape, 1)
    zero = jnp.zeros_like(q)
    return jnp.concatenate([jnp.where(lane < HEAD_DIM, q, zero),
                            jnp.where(lane >= HEAD_DIM, q, zero)], axis=0)


def _augment_v(v, h, axis):
    idx = lax.broadcasted_iota(jnp.int32, v.shape, axis)
    own = (idx < HEAD_DIM) if h == 0 else (idx >= HEAD_DIM)
    return jnp.where(own, v, jnp.ones_like(v))


def _merge_heads_out(acc0, acc1):
    lane = lax.broadcasted_iota(jnp.int32, acc0.shape, 1)
    first = lane < HEAD_DIM
    num = jnp.where(first, acc0, acc1)
    den = pltpu.roll(jnp.where(first, acc1, acc0), HEAD_DIM, axis=1)
    return num / den


def _pair_queries(qt):
    n = qt.shape[1]
    q2 = jnp.concatenate([qt, qt], axis=1)
    r = lax.broadcasted_iota(jnp.int32, q2.shape, 0)
    c = lax.broadcasted_iota(jnp.int32, q2.shape, 1)
    own = jnp.right_shift(r, 6) == jnp.where(c >= n, 1, 0)
    return jnp.where(own, q2, jnp.zeros_like(q2))


def _merge_heads_out_t(acc0, acc1):
    first = lax.broadcasted_iota(jnp.int32, acc0.shape, 0) < HEAD_DIM
    num = jnp.where(first, acc0, acc1)
    den = jnp.where(first, acc0[HEAD_DIM:HEAD_DIM + 1, :], acc1[0:1, :])
    return (num / den).T


def _weighted_values(vaug, p_blocks, key_ranges):
    accs = []
    for h in range(2):
        cols = slice(h * Q_TILE, (h + 1) * Q_TILE)
        acc = None
        for p, (k0, k1) in zip(p_blocks, key_ranges):
            part = _dot(vaug[h][:, k0:k1], p[:, cols])
            acc = part if acc is None else acc + part
        accs.append(acc)
    return accs


def _fox_kernel(qt_ref, kt_ref, vt_ref, c2_ref, kf_ref, o_ref, v0_ref, v1_ref, kx_ref, *, seq):
    pair = pl.program_id(1)
    vt = vt_ref[0]
    v0_ref[...] = _augment_v(vt, 0, 0)
    v1_ref[...] = _augment_v(vt, 1, 0)
    vaug = (v0_ref, v1_ref)

    kx_ref[:, 0:PAIR] = kt_ref[0].astype(F32).T.astype(BF16)
    kx_ref[:, PAIR:2 * PAIR] = kf_ref[0]
    er = lax.broadcasted_iota(jnp.int32, (PAIR, 2 * Q_TILE), 0)
    ec = lax.broadcasted_iota(jnp.int32, (PAIR, 2 * Q_TILE), 1)
    head = 2 * pair + jnp.where(ec >= Q_TILE, 1, 0)
    piece = er - head
    coef = jnp.where((piece == 0) | (piece == N_HEADS) | (piece == 2 * N_HEADS), -1.0, 0.0).astype(BF16)
    key = lax.broadcasted_iota(jnp.int32, (Q_TILE, 2 * Q_TILE), 0)
    qry = jnp.bitwise_and(lax.broadcasted_iota(jnp.int32, (Q_TILE, 2 * Q_TILE), 1), Q_TILE - 1)
    causal = key <= qry

    n_qb = seq // Q_TILE

    def scores(qb):
        qs, ke = qb * Q_TILE, (qb + 1) * Q_TILE
        rhs = jnp.concatenate([_pair_queries(qt_ref[0, :, qs:ke]), coef], axis=0)
        return _dot(kx_ref[0:ke, :], rhs)

    def softmax(qb, s):
        qs, ke = qb * Q_TILE, (qb + 1) * Q_TILE
        cq = jnp.concatenate([c2_ref[0, 0, 0:1, qs:ke], c2_ref[0, 0, 1:2, qs:ke]], axis=1)
        t_diag = jnp.where(causal, s[qs:ke, :], -jnp.inf)
        m = jnp.max(t_diag, axis=0, keepdims=True)
        if qb > 0:
            t_past = s[0:qs, :]
            m = jnp.maximum(m, jnp.max(t_past, axis=0, keepdims=True))
        shift = cq - (m + cq)
        p_blocks = [jnp.exp2(t_diag + shift).astype(BF16)]
        key_ranges = [(qs, ke)]
        if qb > 0:
            p_blocks.append(jnp.exp2(t_past + shift).astype(BF16))
            key_ranges.append((0, qs))
        return p_blocks, key_ranges

    def finish(qb, p_blocks, key_ranges):
        acc0, acc1 = _weighted_values(vaug, p_blocks, key_ranges)
        o_ref[qb * Q_TILE:(qb + 1) * Q_TILE, :] = _merge_heads_out_t(acc0, acc1).astype(o_ref.dtype)

    order = list(range(n_qb - 1, -1, -1))
    s_next = scores(order[0])
    pending = None
    for i, qb in enumerate(order):
        s = s_next
        if i + 1 < n_qb:
            s_next = scores(order[i + 1])
        probs = softmax(qb, s)
        if pending is not None:
            finish(*pending)
        pending = (qb, *probs)
    finish(*pending)


def _fox(qt, kt, vt, c2, kf, *, batch, seq):
    n_tok = batch * seq
    feat = lambda: pl.BlockSpec((1, PAIR, seq), lambda b, p: (b, p, 0))
    return pl.pallas_call(
        functools.partial(_fox_kernel, seq=seq),
        grid=(batch, N_PAIRS),
        in_specs=[
            feat(), feat(), feat(),
            pl.BlockSpec((1, 1, 2, seq), lambda b, p: (b, p, 0, 0)),
            pl.BlockSpec((1, seq, PAIR), lambda b, p: (b, 0, 0)),
        ],
        out_specs=pl.BlockSpec((seq, PAIR), lambda b, p: (b, p)),
        out_shape=jax.ShapeDtypeStruct((n_tok, WIDTH), BF16),
        scratch_shapes=[pltpu.VMEM((PAIR, seq), BF16), pltpu.VMEM((PAIR, seq), BF16),
                        pltpu.VMEM((seq, 2 * PAIR), BF16)],
        compiler_params=_params(),
        name="fox",
    )(qt, kt, vt, c2, kf)


def _bias_kernel(rb_ref, tabm_ref, tabr_ref):
    rb = rb_ref[...]
    n_rows = rb.shape[1]
    r = lax.broadcasted_iota(jnp.int32, (n_rows, TOEPLITZ), 0)
    m = lax.broadcasted_iota(jnp.int32, (n_rows, TOEPLITZ), 1)
    e = jnp.where(m < TOEPLITZ // 2, m, m - TOEPLITZ)
    idx = jnp.clip(LEFT_REACH - Q_TILE - e, -REL_CLIP, REL_CLIP) + REL_CLIP
    onehot = jnp.where(r == idx, 1.0, 0.0).astype(BF16)
    hi = rb.astype(BF16)
    rest = rb - hi.astype(F32)
    mid = rest.astype(BF16)
    lo = (rest - mid.astype(F32)).astype(BF16)
    u = _dot(hi, onehot) + _dot(mid, onehot) + _dot(lo, onehot)

    i = lax.broadcasted_iota(jnp.int32, (Q_TILE, BAND_KEYS), 0)
    j = lax.broadcasted_iota(jnp.int32, (Q_TILE, BAND_KEYS), 1)
    qc = jnp.right_shift(i, 6)
    kc = jnp.right_shift(j, 6)
    valid = (kc >= qc) & (kc <= qc + LEFT_CHUNKS)
    n_new = tabr_ref.shape[1]
    for h in range(N_HEADS):
        base = jnp.broadcast_to(u[h:h + 1, :], (Q_TILE, TOEPLITZ))
        t = pltpu.roll(base, Q_TILE, axis=1, stride=1, stride_axis=0)[:, 0:BAND_KEYS]
        half = (h % 2) * Q_TILE
        tabm_ref[h // 2, :, half:half + Q_TILE] = jnp.where(valid, t * LOG2E, MASKED).T
        tabr_ref[h] = t[0:n_new, :]


def _bias_tables(rb_pad, *, n_new):
    return pl.pallas_call(
        _bias_kernel,
        out_shape=[
            jax.ShapeDtypeStruct((N_PAIRS, BAND_KEYS, 2 * Q_TILE), F32),
            jax.ShapeDtypeStruct((N_HEADS, n_new, BAND_KEYS), F32),
        ],
        compiler_params=_params(),
        name="bias_tables",
    )(rb_pad)


def _band_kernel(qt_ref, kt_ref, vt_ref, tab_ref, o_ref, v0_ref, v1_ref, k_ref, *, seq):
    vt = vt_ref[0]
    v0_ref[...] = _augment_v(vt, 0, 0)
    v1_ref[...] = _augment_v(vt, 1, 0)
    vaug = (v0_ref, v1_ref)
    k_ref[...] = kt_ref[0].astype(F32).T.astype(BF16)

    n_qb = seq // Q_TILE

    def scores(qb):
        qs, ke = qb * Q_TILE, (qb + 1) * Q_TILE
        ks = max(0, qs - LEFT_REACH)
        return _dot(k_ref[ks:ke, :], _pair_queries(qt_ref[0, :, qs:ke]))

    def softmax(qb, s):
        n_keys = s.shape[0]
        t = s + tab_ref[0, BAND_KEYS - n_keys:BAND_KEYS, :]
        m = jnp.max(t, axis=0, keepdims=True)
        return jnp.exp2(t - m).astype(BF16)

    def finish(qb, p):
        ke = (qb + 1) * Q_TILE
        acc0, acc1 = _weighted_values(vaug, [p], [(ke - p.shape[0], ke)])
        o_ref[qb * Q_TILE:ke, :] = _merge_heads_out_t(acc0, acc1).astype(o_ref.dtype)

    order = list(range(n_qb - 1, -1, -1))
    s_next = scores(order[0])
    pending = None
    for i, qb in enumerate(order):
        s = s_next
        if i + 1 < n_qb:
            s_next = scores(order[i + 1])
        p = softmax(qb, s)
        if pending is not None:
            finish(*pending)
        pending = (qb, p)
    finish(*pending)


def _band(qt, kt, vt, tabm, *, batch, seq):
    n_tok = batch * seq
    feat = lambda: pl.BlockSpec((1, PAIR, seq), lambda p, b: (b, p, 0))
    return pl.pallas_call(
        functools.partial(_band_kernel, seq=seq),
        grid=(N_PAIRS, batch),
        in_specs=[
            feat(), feat(), feat(),
            pl.BlockSpec((1, BAND_KEYS, 2 * Q_TILE), lambda p, b: (p, 0, 0)),
        ],
        out_specs=pl.BlockSpec((seq, PAIR), lambda p, b: (b, p)),
        out_shape=jax.ShapeDtypeStruct((n_tok, WIDTH), BF16),
        scratch_shapes=[pltpu.VMEM((PAIR, seq), BF16), pltpu.VMEM((PAIR, seq), BF16),
                        pltpu.VMEM((seq, PAIR), BF16)],
        compiler_params=_params(),
        name="band",
    )(qt, kt, vt, tabm)


def _two_part_attention(q2, kt_cache, vt_cache, kn_ref, vn_ref, bias_cache, bias_new, n_q):
    s_c = _dot(q2, kt_cache)
    s_n = _dot_nt(q2, kn_ref[...])
    accs = []
    for h in range(2):
        rows = slice(h * n_q, (h + 1) * n_q)
        t_c = s_c[rows, :] + bias_cache(h)
        t_n = s_n[rows, :] + bias_new(h)
        m = jnp.maximum(jnp.max(t_c, axis=1, keepdims=True), jnp.max(t_n, axis=1, keepdims=True))
        acc = _dot_nt(jnp.exp(t_c - m).astype(BF16), _augment_v(vt_cache, h, 0))
        acc = acc + _dot(jnp.exp(t_n - m).astype(BF16), _augment_v(vn_ref[...], h, 1))
        accs.append(acc)
    return _merge_heads_out(accs[0], accs[1])


def _sample_attn_kernel(qa_ref, ka_ref, va_ref, qb_ref, kb_ref, vb_ref,
                        cak_ref, cav_ref, cbk_ref, cbv_ref, c_ref, tab_ref,
                        oa_ref, ob_ref, kn_ref, vn_ref, *, n_q):
    row = lax.broadcasted_iota(jnp.int32, (n_q, NEW_PAD), 0)
    col = lax.broadcasted_iota(jnp.int32, (n_q, NEW_PAD), 1)
    n_cache = cak_ref.shape[2]
    n_band = cbk_ref.shape[2]

    def stage_new(slot, k_new, v_new):
        kn_ref[slot] = jnp.zeros((NEW_PAD, PAIR), BF16)
        vn_ref[slot] = jnp.zeros((NEW_PAD, PAIR), BF16)
        kn_ref[slot, 0:n_q, :] = k_new
        vn_ref[slot, 0:n_q, :] = v_new
        return kn_ref.at[slot], vn_ref.at[slot]

    for p in range(N_PAIRS):
        feats = slice(p * PAIR, (p + 1) * PAIR)

        c_cache = c_ref[0, p, :, 0:n_cache]
        c_new = c_ref[0, p, :, n_cache:n_cache + NEW_PAD]

        def fox_cache(h, c_cache=c_cache, c_new=c_new):
            cq = jnp.sum(jnp.where(col == row, c_new[h:h + 1, :], 0.0), axis=1, keepdims=True)
            return cq - c_cache[h:h + 1, :]

        def fox_new(h, c_new=c_new):
            cq = jnp.sum(jnp.where(col == row, c_new[h:h + 1, :], 0.0), axis=1, keepdims=True)
            return jnp.where(col <= row, cq - c_new[h:h + 1, :], -jnp.inf)

        kn, vn = stage_new(2 * p, ka_ref[:, feats], va_ref[:, feats])
        oa = _two_part_attention(_split_heads_q(qa_ref[:, feats]), cak_ref[0, feats, :].astype(BF16),
                                 cav_ref[0, feats, :].astype(BF16), kn, vn, fox_cache, fox_new, n_q)
        oa_ref[:, feats] = oa.astype(oa_ref.dtype)

        def band_cache(h, p=p):
            return tab_ref[2 * p + h, :, 0:n_band]

        def band_new(h, p=p):
            return jnp.where(col < n_q, tab_ref[2 * p + h, :, n_band:n_band + NEW_PAD], -jnp.inf)

        kn, vn = stage_new(2 * p + 1, kb_ref[:, feats], vb_ref[:, feats])
        ob = _two_part_attention(_split_heads_q(qb_ref[:, feats]), cbk_ref[0, feats, :].astype(BF16),
                                 cbv_ref[0, feats, :].astype(BF16), kn, vn, band_cache, band_new, n_q)
        ob_ref[:, feats] = ob.astype(ob_ref.dtype)


def _rows_cumsum_kernel(x_ref, c_ref):
    c_ref[...] = _lane_cumsum(x_ref[...])


def _rows_cumsum(x):
    return pl.pallas_call(
        _rows_cumsum_kernel,
        out_shape=jax.ShapeDtypeStruct(x.shape, F32),
        compiler_params=_params(),
        name="rows_cumsum",
    )(x)


def _sample_attn(qa, ka, va, qb, kb, vb, cakt, cavt, cbkt, cbvt, c_all, tabr, *, batch, n_q):
    n_tok = batch * n_q
    past = cakt.shape[2]
    n_band = cbkt.shape[2]
    new = lambda: pl.BlockSpec((n_q, WIDTH), lambda b: (b, 0))
    cache = lambda n: pl.BlockSpec((1, WIDTH, n), lambda b: (b, 0, 0))
    return pl.pallas_call(
        functools.partial(_sample_attn_kernel, n_q=n_q),
        grid=(batch,),
        in_specs=[
            new(), new(), new(), new(), new(), new(),
            cache(past), cache(past), cache(n_band), cache(n_band),
            pl.BlockSpec((1, N_PAIRS, 2, past + NEW_PAD), lambda b: (b, 0, 0, 0)),
            _resident(tabr.shape),
        ],
        out_specs=[new(), new()],
        out_shape=[jax.ShapeDtypeStruct((n_tok, WIDTH), BF16)] * 2,
        scratch_shapes=[pltpu.VMEM((2 * N_PAIRS, NEW_PAD, PAIR), BF16),
                        pltpu.VMEM((2 * N_PAIRS, NEW_PAD, PAIR), BF16)],
        compiler_params=_params(),
        name="sample_attn",
    )(qa, ka, va, qb, kb, vb, cakt, cavt, cbkt, cbvt, c_all, tabr)


def _post_kernel(x_ref, oa_ref, ob_ref, ga_ref, gb_ref, wo_ref, gpost_ref, gpre_ref,
                 wg_ref, wu_ref, wd_ref, gffn_ref, y_ref, *, n_sub):
    sub = x_ref.shape[0] // n_sub
    rows = [slice(i * sub, (i + 1) * sub) for i in range(n_sub)]
    st = [dict() for _ in range(n_sub)]

    def norm(i):
        st[i]["na"] = _rms(oa_ref[rows[i], :].astype(F32), ga_ref[...]).astype(BF16)
        st[i]["nb"] = _rms(ob_ref[rows[i], :].astype(F32), gb_ref[...]).astype(BF16)

    def mix(i):
        st[i]["mix"] = _dot(st[i]["na"], wo_ref[0:WIDTH, :]) + _dot(st[i]["nb"], wo_ref[WIDTH:2 * WIDTH, :])

    def mid(i):
        x1 = x_ref[rows[i], :] + _rms(st[i]["mix"], gpost_ref[...])
        st[i]["x1"] = x1
        st[i]["hf"] = _rms(x1, gpre_ref[...]).astype(BF16)

    def gate_up(i):
        st[i]["g"] = _dot(st[i]["hf"], wg_ref[...])
        st[i]["u"] = _dot(st[i]["hf"], wu_ref[...])

    def silu(i):
        g = st[i]["g"]
        st[i]["a"] = (g * jax.nn.sigmoid(g) * st[i]["u"]).astype(BF16)

    def down(i):
        st[i]["ff"] = _dot(st[i]["a"], wd_ref[...])

    def fin(i):
        y_ref[rows[i], :] = st[i]["x1"] + _rms(st[i]["ff"], gffn_ref[...])

    vpu_stages = [norm, mid, silu, fin]
    mxu_stages = [mix, gate_up, down]
    norm(0)
    for k, mxu in enumerate(mxu_stages):
        for i in range(n_sub):
            mxu(i)
            if i + 1 < n_sub:
                vpu_stages[k](i + 1)
            else:
                vpu_stages[k + 1](0)
    for i in range(1, n_sub):
        fin(i)


def _post(x, oa, ob, ga, gb, wo, gpost, gpre, wg, wu, wd, gffn, *, tm, n_sub):
    n_tok, d_model = x.shape
    row = lambda w: pl.BlockSpec((tm, w), lambda i: (i, 0))
    return pl.pallas_call(
        functools.partial(_post_kernel, n_sub=n_sub),
        grid=(n_tok // tm,),
        in_specs=[
            row(d_model), row(WIDTH), row(WIDTH),
            _resident(ga.shape), _resident(gb.shape), _resident(wo.shape),
            _resident(gpost.shape), _resident(gpre.shape),
            _resident(wg.shape), _resident(wu.shape), _resident(wd.shape), _resident(gffn.shape),
        ],
        out_specs=row(d_model),
        out_shape=jax.ShapeDtypeStruct((n_tok, d_model), F32),
        compiler_params=_params(),
        name="post",
    )(x, oa, ob, ga, gb, wo, gpost, gpre, wg, wu, wd, gffn)


def _row(v):
    return v.reshape(1, -1).astype(F32)


def _state_from_feature_major(t, batch, n):
    return t.reshape(batch, N_HEADS, HEAD_DIM, n).transpose(0, 3, 1, 2)


def _cache_feature_major(c):
    b, n = c.shape[0], c.shape[1]
    return c.transpose(0, 2, 3, 1).reshape(b, WIDTH, n)


def kernel(x_prompt, x_sample, cache_a_k, cache_a_v, cache_a_logf, cache_b_k, cache_b_v, norm_mix_pre, w_in, b_forget, rel_bias, gain_out_a, gain_out_b, w_out, norm_mix_post, norm_ffn_pre, w_gate, w_up, w_down, norm_ffn_post):
    batch, seq, d_model = x_prompt.shape
    dec_batch, dec_seq, _ = x_sample.shape
    depth = w_in.shape[0]
    past = cache_a_k.shape[2]
    n_band = cache_b_k.shape[2]
    keep = min(LEFT_REACH, seq)
    assert seq % Q_TILE == 0 and keep == LEFT_REACH and n_band == LEFT_REACH and dec_seq <= CHUNK
    n_tok_p, n_tok_s = batch * seq, dec_batch * dec_seq
    tm_p = keep

    xp = x_prompt.reshape(n_tok_p, d_model)
    xs = x_sample.reshape(n_tok_s, d_model)
    outs = [[] for _ in range(10)]
    for l in range(depth):
        b_col = b_forget[l].reshape(N_HEADS, 1).astype(F32)
        rb_pad = jnp.pad(rel_bias[l].astype(F32), ((0, 16 - N_HEADS), (0, 384 - (2 * REL_CLIP + 1))))
        tail = (_row(gain_out_a[l]), _row(gain_out_b[l]), w_out[l].astype(BF16), _row(norm_mix_post[l]),
                _row(norm_ffn_pre[l]), w_gate[l].astype(BF16), w_up[l].astype(BF16), w_down[l].astype(BF16),
                _row(norm_ffn_post[l]))
        tabm, tabr = _bias_tables(rb_pad, n_new=dec_seq)

        qat, qbt, kat, vat, kbt, vbt, kat32, vat32, logf_p, kbt32, vbt32 = _inproj_t(
            xp, _row(norm_mix_pre[l]), w_in, l, b_col, batch=batch, seq=seq, tm=tm_p)
        c2, kf = _cumsum(logf_p.reshape(batch * N_HEADS, seq), batch=batch, seq=seq)
        oa = _fox(qat, kat, vat, c2.reshape(batch, N_PAIRS, 2, seq), kf, batch=batch, seq=seq)
        ob = _band(qbt, kbt, vbt, tabm, batch=batch, seq=seq)
        xp = _post(xp, oa, ob, *tail, tm=512, n_sub=2)
        outs[0].append(_state_from_feature_major(kat32, batch, seq))
        outs[1].append(_state_from_feature_major(vat32, batch, seq))
        outs[2].append(logf_p.transpose(0, 2, 1))
        outs[3].append(_state_from_feature_major(kbt32, batch, keep))
        outs[4].append(_state_from_feature_major(vbt32, batch, keep))

        qa, ka, va, qb, kb, vb, ka32, va32, logf_t, kb32, vb32 = _inproj(xs, _row(norm_mix_pre[l]), w_in, l, b_col)
        lf_all = jnp.concatenate(
            [cache_a_logf[l].astype(F32).transpose(0, 2, 1),
             logf_t.reshape(N_HEADS, dec_batch, dec_seq).transpose(1, 0, 2),
             jnp.zeros((dec_batch, N_HEADS, NEW_PAD - dec_seq), F32)], axis=2)
        c_all = _rows_cumsum(lf_all.reshape(dec_batch * N_HEADS, past + NEW_PAD))
        oa, ob = _sample_attn(
            qa, ka, va, qb, kb, vb,
            _cache_feature_major(cache_a_k[l]), _cache_feature_major(cache_a_v[l]),
            _cache_feature_major(cache_b_k[l]), _cache_feature_major(cache_b_v[l]),
            c_all.reshape(dec_batch, N_PAIRS, 2, past + NEW_PAD), tabr, batch=dec_batch, n_q=dec_seq)
        xs = _post(xs, oa, ob, *tail, tm=n_tok_s, n_sub=1)
        outs[5].append(ka32.reshape(dec_batch, dec_seq, N_HEADS, HEAD_DIM))
        outs[6].append(va32.reshape(dec_batch, dec_seq, N_HEADS, HEAD_DIM))
        outs[7].append(logf_t.T.reshape(dec_batch, dec_seq, N_HEADS))
        outs[8].append(kb32.reshape(dec_batch, dec_seq, N_HEADS, HEAD_DIM))
        outs[9].append(vb32.reshape(dec_batch, dec_seq, N_HEADS, HEAD_DIM))

    st = jnp.stack
    return (xp.reshape(batch, seq, d_model), xs.reshape(dec_batch, dec_seq, d_model),
            *[st(o) for o in outs])
```

```python
import functools

import jax
import jax.numpy as jnp
from jax import lax
from jax.experimental import pallas as pl
from jax.experimental.pallas import tpu as pltpu

F32 = jnp.float32
BF16 = jnp.bfloat16

HEAD_DIM = 64
N_HEADS = 8
WIDTH = N_HEADS * HEAD_DIM
PAIR = 2 * HEAD_DIM
N_PAIRS = N_HEADS // 2
CHUNK = 64
LEFT_CHUNKS = 8
LEFT_REACH = LEFT_CHUNKS * CHUNK
REL_CLIP = 128
EPS = 1e-6
SCALE = HEAD_DIM ** -0.5
LOG2E = 1.4426950408889634
Q_TILE = 256
BAND_KEYS = LEFT_REACH + Q_TILE
TOEPLITZ = 1024
MASKED = -1e30
NEW_PAD = 128
VMEM_LIMIT = 56 * 1024 * 1024
W_IN_COLS = {"qa": 0, "ka": WIDTH, "va": 2 * WIDTH, "forget": 3 * WIDTH,
             "qb": 3 * WIDTH + N_HEADS, "kb": 4 * WIDTH + N_HEADS, "vb": 5 * WIDTH + N_HEADS}


def _params():
    return pltpu.CompilerParams(vmem_limit_bytes=VMEM_LIMIT)


def _resident(shape):
    return pl.BlockSpec(shape, lambda *_: (0,) * len(shape), pipeline_mode=pl.Buffered(1))


def _rms(x, g):
    ms = jnp.mean(x * x, axis=-1, keepdims=True)
    return x * lax.rsqrt(ms + EPS) * g


def _log_sigmoid(x):
    return jnp.minimum(x, 0.0) - jnp.log1p(jnp.exp(-jnp.abs(x)))


def _dot(a, b):
    return jnp.dot(a, b, preferred_element_type=F32)


def _dot_nt(a, b):
    return lax.dot_general(a, b, (((1,), (1,)), ((), ())), preferred_element_type=F32)


def _inproj_kernel(x_ref, g_ref, w_ref, bf_ref,
                   qa_ref, ka_ref, va_ref, qb_ref, kb_ref, vb_ref,
                   ka32_ref, va32_ref, logf_ref, kb32_ref, vb32_ref):
    h = _rms(x_ref[...], g_ref[...]).astype(BF16)

    def proj(name):
        r0 = W_IN_COLS[name]
        return _dot_nt(h, w_ref[0, r0:r0 + WIDTH, :].astype(BF16))

    qa_ref[...] = (proj("qa") * SCALE).astype(BF16)
    z = proj("ka")
    ka32_ref[...] = z
    ka_ref[...] = z.astype(BF16)
    z = proj("va")
    va32_ref[...] = z
    va_ref[...] = z.astype(BF16)
    qb_ref[...] = (proj("qb") * SCALE).astype(BF16)
    z = proj("kb")
    kb32_ref[...] = z
    kb_ref[...] = z.astype(BF16)
    z = proj("vb")
    vb32_ref[...] = z
    vb_ref[...] = z.astype(BF16)
    r0 = W_IN_COLS["forget"]
    ft = _dot_nt(w_ref[0, r0:r0 + 2 * N_HEADS, :].astype(BF16), h)
    logf_ref[...] = _log_sigmoid(ft[0:N_HEADS, :] + bf_ref[...])


def _inproj(x, gain, w_in, layer, b_col):
    n_tok, d_model = x.shape
    full = lambda w: pl.BlockSpec((n_tok, w), lambda i: (0, 0))
    w_spec = pl.BlockSpec((1,) + w_in.shape[1:], lambda i: (layer, 0, 0), pipeline_mode=pl.Buffered(1))
    out_shape = (
        [jax.ShapeDtypeStruct((n_tok, WIDTH), BF16)] * 6
        + [jax.ShapeDtypeStruct((n_tok, WIDTH), F32)] * 2
        + [jax.ShapeDtypeStruct((N_HEADS, n_tok), F32)]
        + [jax.ShapeDtypeStruct((n_tok, WIDTH), F32)] * 2
    )
    out_specs = ([full(WIDTH) for _ in range(8)]
                 + [pl.BlockSpec((N_HEADS, n_tok), lambda i: (0, 0)), full(WIDTH), full(WIDTH)])
    return pl.pallas_call(
        _inproj_kernel,
        grid=(1,),
        in_specs=[full(d_model), _resident((1, d_model)), w_spec, _resident((N_HEADS, 1))],
        out_specs=out_specs,
        out_shape=out_shape,
        compiler_params=_params(),
        name="inproj_sample",
    )(x, gain, w_in, b_col)


def _inproj_t_kernel(x_ref, g_ref, w_ref, bf_ref,
                     qat_ref, qbt_ref, kat_ref, vat_ref, kbt_ref, vbt_ref,
                     kat32_ref, vat32_ref, logf_ref, kbt32_ref, vbt32_ref, wt_ref, *, tiles_per_seq):
    @pl.when(pl.program_id(0) == 0)
    def _():
        for g, name in enumerate(("ka", "va", "kb", "vb", "qa", "qb")):
            r0 = W_IN_COLS[name]
            wt_ref[g * WIDTH:(g + 1) * WIDTH, :] = w_ref[0, r0:r0 + WIDTH, :].astype(BF16)
        r0 = W_IN_COLS["forget"]
        ft = w_ref[0, r0:r0 + 2 * N_HEADS, :]
        live = lax.broadcasted_iota(jnp.int32, ft.shape, 0) < N_HEADS
        wt_ref[6 * WIDTH:6 * WIDTH + 2 * N_HEADS, :] = jnp.where(live, ft, 0.0).astype(BF16)

    n_sub = 2
    sub = x_ref.shape[0] // n_sub
    hs, kept = {}, []

    def norm(i):
        hs[i] = _rms(x_ref[i * sub:(i + 1) * sub, :], g_ref[...]).astype(BF16)

    def project(i, after_first=None):
        cols = slice(i * sub, (i + 1) * sub)

        def proj_t(c, rows=WIDTH):
            return _dot_nt(wt_ref[c * WIDTH:c * WIDTH + rows, :], hs[i])

        z = proj_t(0)
        kat32_ref[0, :, cols] = z
        kat_ref[0, :, cols] = z.astype(BF16)
        if after_first is not None:
            after_first()
        z = proj_t(1)
        vat32_ref[0, :, cols] = z
        vat_ref[0, :, cols] = z.astype(BF16)
        zk = proj_t(2)
        kbt_ref[0, :, cols] = zk.astype(BF16)
        zv = proj_t(3)
        vbt_ref[0, :, cols] = zv.astype(BF16)
        qat_ref[0, :, cols] = (proj_t(4) * (SCALE * LOG2E)).astype(BF16)
        zq = proj_t(5, WIDTH + 2 * N_HEADS)
        qbt_ref[0, :, cols] = (zq[0:WIDTH] * (SCALE * LOG2E)).astype(BF16)
        logf_ref[0, :, cols] = _log_sigmoid(zq[WIDTH:WIDTH + N_HEADS] + bf_ref[...])
        kept.append((cols, zk, zv))

    norm(0)
    for i in range(n_sub):
        project(i, (lambda j=i + 1: norm(j)) if i + 1 < n_sub else None)

    @pl.when(pl.program_id(0) % tiles_per_seq == tiles_per_seq - 1)
    def _():
        for cols, zk, zv in kept:
            kbt32_ref[0, :, cols] = zk
            vbt32_ref[0, :, cols] = zv


def _inproj_t(x, gain, w_in, layer, b_col, *, batch, seq, tm):
    n_tok, d_model = x.shape
    tps = seq // tm
    w_spec = pl.BlockSpec((1,) + w_in.shape[1:], lambda i: (layer, 0, 0), pipeline_mode=pl.Buffered(1))
    feat = lambda: pl.BlockSpec((1, WIDTH, tm), lambda i: (i // tps, 0, i % tps))
    keep = lambda: pl.BlockSpec((1, WIDTH, tm), lambda i: (i // tps, 0, 0))
    out_shape = (
        [jax.ShapeDtypeStruct((batch, WIDTH, seq), BF16)] * 6
        + [jax.ShapeDtypeStruct((batch, WIDTH, seq), F32)] * 2
        + [jax.ShapeDtypeStruct((batch, N_HEADS, seq), F32)]
        + [jax.ShapeDtypeStruct((batch, WIDTH, tm), F32)] * 2
    )
    out_specs = ([feat() for _ in range(8)]
                 + [pl.BlockSpec((1, N_HEADS, tm), lambda i: (i // tps, 0, i % tps)), keep(), keep()])
    return pl.pallas_call(
        functools.partial(_inproj_t_kernel, tiles_per_seq=tps),
        grid=(n_tok // tm,),
        in_specs=[
            pl.BlockSpec((tm, d_model), lambda i: (i, 0)),
            _resident((1, d_model)),
            w_spec,
            _resident((N_HEADS, 1)),
        ],
        out_specs=out_specs,
        out_shape=out_shape,
        scratch_shapes=[pltpu.VMEM((6 * WIDTH + 2 * N_HEADS, d_model), BF16)],
        compiler_params=_params(),
        name="inproj",
    )(x, gain, w_in, b_col)


def _lane_cumsum(x):
    n = x.shape[-1]
    lane = lax.broadcasted_iota(jnp.int32, x.shape, x.ndim - 1)
    shift = 1
    while shift < n:
        x = x + jnp.where(lane >= shift, pltpu.roll(x, shift, axis=x.ndim - 1), 0.0)
        shift *= 2
    return x


def _split3(x):
    hi = x.astype(BF16).astype(F32)
    rest = x - hi
    mid = rest.astype(BF16).astype(F32)
    return hi, mid, rest - mid


def _cumsum_kernel(lf_ref, c2_ref, kf_ref):
    c2 = _lane_cumsum(lf_ref[...]) * LOG2E
    c2_ref[...] = c2
    seq = c2.shape[1]
    hi, mid, lo = _split3(c2)
    pad = jnp.zeros((PAIR - 3 * N_HEADS, seq), F32)
    for b in range(kf_ref.shape[0]):
        rows = slice(b * N_HEADS, (b + 1) * N_HEADS)
        feat = jnp.concatenate([hi[rows], mid[rows], lo[rows], pad], axis=0)
        kf_ref[b] = feat.T.astype(BF16)


def _cumsum(logf, *, batch, seq):
    return pl.pallas_call(
        _cumsum_kernel,
        out_shape=[
            jax.ShapeDtypeStruct((batch * N_HEADS, seq), F32),
            jax.ShapeDtypeStruct((batch, seq, PAIR), BF16),
        ],
        compiler_params=_params(),
        name="cumsum",
    )(logf)


def _split_heads_q(q):
    lane = lax.broadcasted_iota(jnp.int32, q.shape, 1)
    zero = jnp.zeros_like(q)
    return jnp.concatenate([jnp.where(lane < HEAD_DIM, q, zero),
                            jnp.where(lane >= HEAD_DIM, q, zero)], axis=0)


def _augment_v(v, h, axis):
    idx = lax.broadcasted_iota(jnp.int32, v.shape, axis)
    own = (idx < HEAD_DIM) if h == 0 else (idx >= HEAD_DIM)
    return jnp.where(own, v, jnp.ones_like(v))


def _merge_heads_out(acc0, acc1):
    lane = lax.broadcasted_iota(jnp.int32, acc0.shape, 1)
    first = lane < HEAD_DIM
    num = jnp.where(first, acc0, acc1)
    den = pltpu.roll(jnp.where(first, acc1, acc0), HEAD_DIM, axis=1)
    return num / den


def _pair_queries(qt):
    n = qt.shape[1]
    q2 = jnp.concatenate([qt, qt], axis=1)
    r = lax.broadcasted_iota(jnp.int32, q2.shape, 0)
    c = lax.broadcasted_iota(jnp.int32, q2.shape, 1)
    own = jnp.right_shift(r, 6) == jnp.where(c >= n, 1, 0)
    return jnp.where(own, q2, jnp.zeros_like(q2))


def _merge_heads_out_t(acc0, acc1):
    first = lax.broadcasted_iota(jnp.int32, acc0.shape, 0) < HEAD_DIM
    num = jnp.where(first, acc0, acc1)
    den = jnp.where(first, acc0[HEAD_DIM:HEAD_DIM + 1, :], acc1[0:1, :])
    return (num / den).T


def _weighted_values(vaug, p_blocks, key_ranges):
    accs = []
    for h in range(2):
        cols = slice(h * Q_TILE, (h + 1) * Q_TILE)
        acc = None
        for p, (k0, k1) in zip(p_blocks, key_ranges):
            part = _dot(vaug[h][:, k0:k1], p[:, cols])
            acc = part if acc is None else acc + part
        accs.append(acc)
    return accs


def _fox_kernel(qt_ref, kt_ref, vt_ref, c2_ref, kf_ref, o_ref, v0_ref, v1_ref, kx_ref, *, seq):
    pair = pl.program_id(1)
    vt = vt_ref[0]
    v0_ref[...] = _augment_v(vt, 0, 0)
    v1_ref[...] = _augment_v(vt, 1, 0)
    vaug = (v0_ref, v1_ref)

    kx_ref[:, 0:PAIR] = kt_ref[0].astype(F32).T.astype(BF16)
    kx_ref[:, PAIR:2 * PAIR] = kf_ref[0]
    er = lax.broadcasted_iota(jnp.int32, (PAIR, 2 * Q_TILE), 0)
    ec = lax.broadcasted_iota(jnp.int32, (PAIR, 2 * Q_TILE), 1)
    head = 2 * pair + jnp.where(ec >= Q_TILE, 1, 0)
    piece = er - head
    coef = jnp.where((piece == 0) | (piece == N_HEADS) | (piece == 2 * N_HEADS), -1.0, 0.0).astype(BF16)
    key = lax.broadcasted_iota(jnp.int32, (Q_TILE, 2 * Q_TILE), 0)
    qry = jnp.bitwise_and(lax.broadcasted_iota(jnp.int32, (Q_TILE, 2 * Q_TILE), 1), Q_TILE - 1)
    causal = key <= qry

    n_qb = seq // Q_TILE

    def scores(qb):
        qs, ke = qb * Q_TILE, (qb + 1) * Q_TILE
        rhs = jnp.concatenate([_pair_queries(qt_ref[0, :, qs:ke]), coef], axis=0)
        return _dot(kx_ref[0:ke, :], rhs)

    def softmax(qb, s):
        qs, ke = qb * Q_TILE, (qb + 1) * Q_TILE
        cq = jnp.concatenate([c2_ref[0, 0, 0:1, qs:ke], c2_ref[0, 0, 1:2, qs:ke]], axis=1)
        t_diag = jnp.where(causal, s[qs:ke, :], -jnp.inf)
        m = jnp.max(t_diag, axis=0, keepdims=True)
        if qb > 0:
            t_past = s[0:qs, :]
            m = jnp.maximum(m, jnp.max(t_past, axis=0, keepdims=True))
        shift = cq - (m + cq)
        p_blocks = [jnp.exp2(t_diag + shift).astype(BF16)]
        key_ranges = [(qs, ke)]
        if qb > 0:
            p_blocks.append(jnp.exp2(t_past + shift).astype(BF16))
            key_ranges.append((0, qs))
        return p_blocks, key_ranges

    def finish(qb, p_blocks, key_ranges):
        acc0, acc1 = _weighted_values(vaug, p_blocks, key_ranges)
        o_ref[qb * Q_TILE:(qb + 1) * Q_TILE, :] = _merge_heads_out_t(acc0, acc1).astype(o_ref.dtype)

    order = list(range(n_qb - 1, -1, -1))
    s_next = scores(order[0])
    pending = None
    for i, qb in enumerate(order):
        s = s_next
        if i + 1 < n_qb:
            s_next = scores(order[i + 1])
        probs = softmax(qb, s)
        if pending is not None:
            finish(*pending)
        pending = (qb, *probs)
    finish(*pending)


def _fox(qt, kt, vt, c2, kf, *, batch, seq):
    n_tok = batch * seq
    feat = lambda: pl.BlockSpec((1, PAIR, seq), lambda b, p: (b, p, 0))
    return pl.pallas_call(
        functools.partial(_fox_kernel, seq=seq),
        grid=(batch, N_PAIRS),
        in_specs=[
            feat(), feat(), feat(),
            pl.BlockSpec((1, 1, 2, seq), lambda b, p: (b, p, 0, 0)),
            pl.BlockSpec((1, seq, PAIR), lambda b, p: (b, 0, 0)),
        ],
        out_specs=pl.BlockSpec((seq, PAIR), lambda b, p: (b, p)),
        out_shape=jax.ShapeDtypeStruct((n_tok, WIDTH), BF16),
        scratch_shapes=[pltpu.VMEM((PAIR, seq), BF16), pltpu.VMEM((PAIR, seq), BF16),
                        pltpu.VMEM((seq, 2 * PAIR), BF16)],
        compiler_params=_params(),
        name="fox",
    )(qt, kt, vt, c2, kf)


def _bias_kernel(rb_ref, tabm_ref, tabr_ref):
    rb = rb_ref[...]
    n_rows = rb.shape[1]
    r = lax.broadcasted_iota(jnp.int32, (n_rows, TOEPLITZ), 0)
    m = lax.broadcasted_iota(jnp.int32, (n_rows, TOEPLITZ), 1)
    e = jnp.where(m < TOEPLITZ // 2, m, m - TOEPLITZ)
    idx = jnp.clip(LEFT_REACH - Q_TILE - e, -REL_CLIP, REL_CLIP) + REL_CLIP
    onehot = jnp.where(r == idx, 1.0, 0.0).astype(BF16)
    hi = rb.astype(BF16)
    rest = rb - hi.astype(F32)
    mid = rest.astype(BF16)
    lo = (rest - mid.astype(F32)).astype(BF16)
    u = _dot(hi, onehot) + _dot(mid, onehot) + _dot(lo, onehot)

    i = lax.broadcasted_iota(jnp.int32, (Q_TILE, BAND_KEYS), 0)
    j = lax.broadcasted_iota(jnp.int32, (Q_TILE, BAND_KEYS), 1)
    qc = jnp.right_shift(i, 6)
    kc = jnp.right_shift(j, 6)
    valid = (kc >= qc) & (kc <= qc + LEFT_CHUNKS)
    n_new = tabr_ref.shape[1]
    for h in range(N_HEADS):
        base = jnp.broadcast_to(u[h:h + 1, :], (Q_TILE, TOEPLITZ))
        t = pltpu.roll(base, Q_TILE, axis=1, stride=1, stride_axis=0)[:, 0:BAND_KEYS]
        half = (h % 2) * Q_TILE
        tabm_ref[h // 2, :, half:half + Q_TILE] = jnp.where(valid, t * LOG2E, MASKED).T
        tabr_ref[h] = t[0:n_new, :]


def _bias_tables(rb_pad, *, n_new):
    return pl.pallas_call(
        _bias_kernel,
        out_shape=[
            jax.ShapeDtypeStruct((N_PAIRS, BAND_KEYS, 2 * Q_TILE), F32),
            jax.ShapeDtypeStruct((N_HEADS, n_new, BAND_KEYS), F32),
        ],
        compiler_params=_params(),
        name="bias_tables",
    )(rb_pad)


def _band_kernel(qt_ref, kt_ref, vt_ref, tab_ref, o_ref, v0_ref, v1_ref, k_ref, *, seq):
    vt = vt_ref[0]
    v0_ref[...] = _augment_v(vt, 0, 0)
    v1_ref[...] = _augment_v(vt, 1, 0)
    vaug = (v0_ref, v1_ref)
    k_ref[...] = kt_ref[0].astype(F32).T.astype(BF16)

    n_qb = seq // Q_TILE

    def scores(qb):
        qs, ke = qb * Q_TILE, (qb + 1) * Q_TILE
        ks = max(0, qs - LEFT_REACH)
        return _dot(k_ref[ks:ke, :], _pair_queries(qt_ref[0, :, qs:ke]))

    def softmax(qb, s):
        n_keys = s.shape[0]
        t = s + tab_ref[0, BAND_KEYS - n_keys:BAND_KEYS, :]
        m = jnp.max(t, axis=0, keepdims=True)
        return jnp.exp2(t - m).astype(BF16)

    def finish(qb, p):
        ke = (qb + 1) * Q_TILE
        acc0, acc1 = _weighted_values(vaug, [p], [(ke - p.shape[0], ke)])
        o_ref[qb * Q_TILE:ke, :] = _merge_heads_out_t(acc0, acc1).astype(o_ref.dtype)

    order = list(range(n_qb - 1, -1, -1))
    s_next = scores(order[0])
    pending = None
    for i, qb in enumerate(order):
        s = s_next
        if i + 1 < n_qb:
            s_next = scores(order[i + 1])
        p = softmax(qb, s)
        if pending is not None:
            finish(*pending)
        pending = (qb, p)
    finish(*pending)


def _band(qt, kt, vt, tabm, *, batch, seq):
    n_tok = batch * seq
    feat = lambda: pl.BlockSpec((1, PAIR, seq), lambda p, b: (b, p, 0))
    return pl.pallas_call(
        functools.partial(_band_kernel, seq=seq),
        grid=(N_PAIRS, batch),
        in_specs=[
            feat(), feat(), feat(),
            pl.BlockSpec((1, BAND_KEYS, 2 * Q_TILE), lambda p, b: (p, 0, 0)),
        ],
        out_specs=pl.BlockSpec((seq, PAIR), lambda p, b: (b, p)),
        out_shape=jax.ShapeDtypeStruct((n_tok, WIDTH), BF16),
        scratch_shapes=[pltpu.VMEM((PAIR, seq), BF16), pltpu.VMEM((PAIR, seq), BF16),
                        pltpu.VMEM((seq, PAIR), BF16)],
        compiler_params=_params(),
        name="band",
    )(qt, kt, vt, tabm)


def _two_part_attention(q2, kt_cache, vt_cache, kn_ref, vn_ref, bias_cache, bias_new, n_q):
    s_c = _dot(q2, kt_cache)
    s_n = _dot_nt(q2, kn_ref[...])
    accs = []
    for h in range(2):
        rows = slice(h * n_q, (h + 1) * n_q)
        t_c = s_c[rows, :] + bias_cache(h)
        t_n = s_n[rows, :] + bias_new(h)
        m = jnp.maximum(jnp.max(t_c, axis=1, keepdims=True), jnp.max(t_n, axis=1, keepdims=True))
        acc = _dot_nt(jnp.exp(t_c - m).astype(BF16), _augment_v(vt_cache, h, 0))
        acc = acc + _dot(jnp.exp(t_n - m).astype(BF16), _augment_v(vn_ref[...], h, 1))
        accs.append(acc)
    return _merge_heads_out(accs[0], accs[1])


def _sample_attn_kernel(qa_ref, ka_ref, va_ref, qb_ref, kb_ref, vb_ref,
                        cak_ref, cav_ref, cbk_ref, cbv_ref, c_ref, tab_ref,
                        oa_ref, ob_ref, kn_ref, vn_ref, *, n_q):
    row = lax.broadcasted_iota(jnp.int32, (n_q, NEW_PAD), 0)
    col = lax.broadcasted_iota(jnp.int32, (n_q, NEW_PAD), 1)
    n_cache = cak_ref.shape[2]
    n_band = cbk_ref.shape[2]

    def stage_new(slot, k_new, v_new):
        kn_ref[slot] = jnp.zeros((NEW_PAD, PAIR), BF16)
        vn_ref[slot] = jnp.zeros((NEW_PAD, PAIR), BF16)
        kn_ref[slot, 0:n_q, :] = k_new
        vn_ref[slot, 0:n_q, :] = v_new
        return kn_ref.at[slot], vn_ref.at[slot]

    for p in range(N_PAIRS):
        feats = slice(p * PAIR, (p + 1) * PAIR)

        c_cache = c_ref[0, p, :, 0:n_cache]
        c_new = c_ref[0, p, :, n_cache:n_cache + NEW_PAD]

        def fox_cache(h, c_cache=c_cache, c_new=c_new):
            cq = jnp.sum(jnp.where(col == row, c_new[h:h + 1, :], 0.0), axis=1, keepdims=True)
            return cq - c_cache[h:h + 1, :]

        def fox_new(h, c_new=c_new):
            cq = jnp.sum(jnp.where(col == row, c_new[h:h + 1, :], 0.0), axis=1, keepdims=True)
            return jnp.where(col <= row, cq - c_new[h:h + 1, :], -jnp.inf)

        kn, vn = stage_new(2 * p, ka_ref[:, feats], va_ref[:, feats])
        oa = _two_part_attention(_split_heads_q(qa_ref[:, feats]), cak_ref[0, feats, :].astype(BF16),
                                 cav_ref[0, feats, :].astype(BF16), kn, vn, fox_cache, fox_new, n_q)
        oa_ref[:, feats] = oa.astype(oa_ref.dtype)

        def band_cache(h, p=p):
            return tab_ref[2 * p + h, :, 0:n_band]

        def band_new(h, p=p):
            return jnp.where(col < n_q, tab_ref[2 * p + h, :, n_band:n_band + NEW_PAD], -jnp.inf)

        kn, vn = stage_new(2 * p + 1, kb_ref[:, feats], vb_ref[:, feats])
        ob = _two_part_attention(_split_heads_q(qb_ref[:, feats]), cbk_ref[0, feats, :].astype(BF16),
                                 cbv_ref[0, feats, :].astype(BF16), kn, vn, band_cache, band_new, n_q)
        ob_ref[:, feats] = ob.astype(ob_ref.dtype)


def _rows_cumsum_kernel(x_ref, c_ref):
    c_ref[...] = _lane_cumsum(x_ref[...])


def _rows_cumsum(x):
    return pl.pallas_call(
        _rows_cumsum_kernel,
        out_shape=jax.ShapeDtypeStruct(x.shape, F32),
        compiler_params=_params(),
        name="rows_cumsum",
    )(x)


def _sample_attn(qa, ka, va, qb, kb, vb, cakt, cavt, cbkt, cbvt, c_all, tabr, *, batch, n_q):
    n_tok = batch * n_q
    past = cakt.shape[2]
    n_band = cbkt.shape[2]
    new = lambda: pl.BlockSpec((n_q, WIDTH), lambda b: (b, 0))
    cache = lambda n: pl.BlockSpec((1, WIDTH, n), lambda b: (b, 0, 0))
    return pl.pallas_call(
        functools.partial(_sample_attn_kernel, n_q=n_q),
        grid=(batch,),
        in_specs=[
            new(), new(), new(), new(), new(), new(),
            cache(past), cache(past), cache(n_band), cache(n_band),
            pl.BlockSpec((1, N_PAIRS, 2, past + NEW_PAD), lambda b: (b, 0, 0, 0)),
            _resident(tabr.shape),
        ],
        out_specs=[new(), new()],
        out_shape=[jax.ShapeDtypeStruct((n_tok, WIDTH), BF16)] * 2,
        scratch_shapes=[pltpu.VMEM((2 * N_PAIRS, NEW_PAD, PAIR), BF16),
                        pltpu.VMEM((2 * N_PAIRS, NEW_PAD, PAIR), BF16)],
        compiler_params=_params(),
        name="sample_attn",
    )(qa, ka, va, qb, kb, vb, cakt, cavt, cbkt, cbvt, c_all, tabr)


def _post_kernel(x_ref, oa_ref, ob_ref, ga_ref, gb_ref, wo_ref, gpost_ref, gpre_ref,
                 wg_ref, wu_ref, wd_ref, gffn_ref, y_ref, *, n_sub):
    sub = x_ref.shape[0] // n_sub
    rows = [slice(i * sub, (i + 1) * sub) for i in range(n_sub)]
    st = [dict() for _ in range(n_sub)]

    def norm(i):
        st[i]["na"] = _rms(oa_ref[rows[i], :].astype(F32), ga_ref[...]).astype(BF16)
        st[i]["nb"] = _rms(ob_ref[rows[i], :].astype(F32), gb_ref[...]).astype(BF16)

    def mix(i):
        st[i]["mix"] = _dot(st[i]["na"], wo_ref[0:WIDTH, :]) + _dot(st[i]["nb"], wo_ref[WIDTH:2 * WIDTH, :])

    def mid(i):
        x1 = x_ref[rows[i], :] + _rms(st[i]["mix"], gpost_ref[...])
        st[i]["x1"] = x1
        st[i]["hf"] = _rms(x1, gpre_ref[...]).astype(BF16)

    def gate_up(i):
        st[i]["g"] = _dot(st[i]["hf"], wg_ref[...])
        st[i]["u"] = _dot(st[i]["hf"], wu_ref[...])

    def silu(i):
        g = st[i]["g"]
        st[i]["a"] = (g * jax.nn.sigmoid(g) * st[i]["u"]).astype(BF16)

    def down(i):
        st[i]["ff"] = _dot(st[i]["a"], wd_ref[...])

    def fin(i):
        y_ref[rows[i], :] = st[i]["x1"] + _rms(st[i]["ff"], gffn_ref[...])

    vpu_stages = [norm, mid, silu, fin]
    mxu_stages = [mix, gate_up, down]
    norm(0)
    for k, mxu in enumerate(mxu_stages):
        for i in range(n_sub):
            mxu(i)
            if i + 1 < n_sub:
                vpu_stages[k](i + 1)
            else:
                vpu_stages[k + 1](0)
    for i in range(1, n_sub):
        fin(i)


def _post(x, oa, ob, ga, gb, wo, gpost, gpre, wg, wu, wd, gffn, *, tm, n_sub):
    n_tok, d_model = x.shape
    row = lambda w: pl.BlockSpec((tm, w), lambda i: (i, 0))
    return pl.pallas_call(
        functools.partial(_post_kernel, n_sub=n_sub),
        grid=(n_tok // tm,),
        in_specs=[
            row(d_model), row(WIDTH), row(WIDTH),
            _resident(ga.shape), _resident(gb.shape), _resident(wo.shape),
            _resident(gpost.shape), _resident(gpre.shape),
            _resident(wg.shape), _resident(wu.shape), _resident(wd.shape), _resident(gffn.shape),
        ],
        out_specs=row(d_model),
        out_shape=jax.ShapeDtypeStruct((n_tok, d_model), F32),
        compiler_params=_params(),
        name="post",
    )(x, oa, ob, ga, gb, wo, gpost, gpre, wg, wu, wd, gffn)


def _row(v):
    return v.reshape(1, -1).astype(F32)


def _state_from_feature_major(t, batch, n):
    return t.reshape(batch, N_HEADS, HEAD_DIM, n).transpose(0, 3, 1, 2)


def _cache_feature_major(c):
    b, n = c.shape[0], c.shape[1]
    return c.transpose(0, 2, 3, 1).reshape(b, WIDTH, n)


def kernel(x_prompt, x_sample, cache_a_k, cache_a_v, cache_a_logf, cache_b_k, cache_b_v, norm_mix_pre, w_in, b_forget, rel_bias, gain_out_a, gain_out_b, w_out, norm_mix_post, norm_ffn_pre, w_gate, w_up, w_down, norm_ffn_post):
    batch, seq, d_model = x_prompt.shape
    dec_batch, dec_seq, _ = x_sample.shape
    depth = w_in.shape[0]
    past = cache_a_k.shape[2]
    n_band = cache_b_k.shape[2]
    keep = min(LEFT_REACH, seq)
    assert seq % Q_TILE == 0 and keep == LEFT_REACH and n_band == LEFT_REACH and dec_seq <= CHUNK
    n_tok_p, n_tok_s = batch * seq, dec_batch * dec_seq
    tm_p = keep

    xp = x_prompt.reshape(n_tok_p, d_model)
    xs = x_sample.reshape(n_tok_s, d_model)
    outs = [[] for _ in range(10)]
    w_in_t = jnp.swapaxes(w_in, 1, 2)
    for l in range(depth):
        b_col = b_forget[l].reshape(N_HEADS, 1).astype(F32)
        rb_pad = jnp.pad(rel_bias[l].astype(F32), ((0, 16 - N_HEADS), (0, 384 - (2 * REL_CLIP + 1))))
        tail = (_row(gain_out_a[l]), _row(gain_out_b[l]), w_out[l].astype(BF16), _row(norm_mix_post[l]),
                _row(norm_ffn_pre[l]), w_gate[l].astype(BF16), w_up[l].astype(BF16), w_down[l].astype(BF16),
                _row(norm_ffn_post[l]))
        tabm, tabr = _bias_tables(rb_pad, n_new=dec_seq)

        qat, qbt, kat, vat, kbt, vbt, kat32, vat32, logf_p, kbt32, vbt32 = _inproj_t(
            xp, _row(norm_mix_pre[l]), w_in_t, l, b_col, batch=batch, seq=seq, tm=tm_p)
        c2, kf = _cumsum(logf_p.reshape(batch * N_HEADS, seq), batch=batch, seq=seq)
        oa = _fox(qat, kat, vat, c2.reshape(batch, N_PAIRS, 2, seq), kf, batch=batch, seq=seq)
        ob = _band(qbt, kbt, vbt, tabm, batch=batch, seq=seq)
        xp = _post(xp, oa, ob, *tail, tm=512, n_sub=2)
        outs[0].append(_state_from_feature_major(kat32, batch, seq))
        outs[1].append(_state_from_feature_major(vat32, batch, seq))
        outs[2].append(logf_p.transpose(0, 2, 1))
        outs[3].append(_state_from_feature_major(kbt32, batch, keep))
        outs[4].append(_state_from_feature_major(vbt32, batch, keep))

        qa, ka, va, qb, kb, vb, ka32, va32, logf_t, kb32, vb32 = _inproj(xs, _row(norm_mix_pre[l]), w_in_t, l, b_col)
        lf_all = jnp.concatenate(
            [cache_a_logf[l].astype(F32).transpose(0, 2, 1),
             logf_t.reshape(N_HEADS, dec_batch, dec_seq).transpose(1, 0, 2),
             jnp.zeros((dec_batch, N_HEADS, NEW_PAD - dec_seq), F32)], axis=2)
        c_all = _rows_cumsum(lf_all.reshape(dec_batch * N_HEADS, past + NEW_PAD))
        oa, ob = _sample_attn(
            qa, ka, va, qb, kb, vb,
            _cache_feature_major(cache_a_k[l]), _cache_feature_major(cache_a_v[l]),
            _cache_feature_major(cache_b_k[l]), _cache_feature_major(cache_b_v[l]),
            c_all.reshape(dec_batch, N_PAIRS, 2, past + NEW_PAD), tabr, batch=dec_batch, n_q=dec_seq)
        xs = _post(xs, oa, ob, *tail, tm=n_tok_s, n_sub=1)
        outs[5].append(ka32.reshape(dec_batch, dec_seq, N_HEADS, HEAD_DIM))
        outs[6].append(va32.reshape(dec_batch, dec_seq, N_HEADS, HEAD_DIM))
        outs[7].append(logf_t.T.reshape(dec_batch, dec_seq, N_HEADS))
        outs[8].append(kb32.reshape(dec_batch, dec_seq, N_HEADS, HEAD_DIM))
        outs[9].append(vb32.reshape(dec_batch, dec_seq, N_HEADS, HEAD_DIM))

    st = jnp.stack
    return (xp.reshape(batch, seq, d_model), xs.reshape(dec_batch, dec_seq, d_model),
            *[st(o) for o in outs])
```

```python
import functools

import jax
import jax.numpy as jnp
from jax import lax
from jax.experimental import pallas as pl
from jax.experimental.pallas import tpu as pltpu

F32 = jnp.float32
BF16 = jnp.bfloat16

HEAD_DIM = 64
N_HEADS = 8
WIDTH = N_HEADS * HEAD_DIM
PAIR = 2 * HEAD_DIM
N_PAIRS = N_HEADS // 2
CHUNK = 64
LEFT_CHUNKS = 8
LEFT_REACH = LEFT_CHUNKS * CHUNK
REL_CLIP = 128
EPS = 1e-6
SCALE = HEAD_DIM ** -0.5
LOG2E = 1.4426950408889634
Q_TILE = 256
PAIRS_PER_STEP = 4
BAND_KEYS = LEFT_REACH + Q_TILE
TOEPLITZ = 1024
MASKED = -1e30
NEW_PAD = 128
VMEM_LIMIT = 56 * 1024 * 1024
W_IN_COLS = {"qa": 0, "ka": WIDTH, "va": 2 * WIDTH, "forget": 3 * WIDTH,
             "qb": 3 * WIDTH + N_HEADS, "kb": 4 * WIDTH + N_HEADS, "vb": 5 * WIDTH + N_HEADS}


def _params():
    return pltpu.CompilerParams(vmem_limit_bytes=VMEM_LIMIT)


def _resident(shape):
    return pl.BlockSpec(shape, lambda *_: (0,) * len(shape), pipeline_mode=pl.Buffered(1))


def _rms(x, g):
    ms = jnp.mean(x * x, axis=-1, keepdims=True)
    return x * lax.rsqrt(ms + EPS) * g


def _log_sigmoid(x):
    return jnp.minimum(x, 0.0) - jnp.log1p(jnp.exp(-jnp.abs(x)))


def _dot(a, b):
    return jnp.dot(a, b, preferred_element_type=F32)


def _dot_nt(a, b):
    return lax.dot_general(a, b, (((1,), (1,)), ((), ())), preferred_element_type=F32)


def _inproj_kernel(x_ref, g_ref, w_ref, bf_ref,
                   qa_ref, ka_ref, va_ref, qb_ref, kb_ref, vb_ref,
                   ka32_ref, va32_ref, logf_ref, kb32_ref, vb32_ref):
    h = _rms(x_ref[...], g_ref[...]).astype(BF16)

    def proj(name):
        r0 = W_IN_COLS[name]
        return _dot_nt(h, w_ref[0, r0:r0 + WIDTH, :].astype(BF16))

    qa_ref[...] = (proj("qa") * SCALE).astype(BF16)
    z = proj("ka")
    ka32_ref[...] = z
    ka_ref[...] = z.astype(BF16)
    z = proj("va")
    va32_ref[...] = z
    va_ref[...] = z.astype(BF16)
    qb_ref[...] = (proj("qb") * SCALE).astype(BF16)
    z = proj("kb")
    kb32_ref[...] = z
    kb_ref[...] = z.astype(BF16)
    z = proj("vb")
    vb32_ref[...] = z
    vb_ref[...] = z.astype(BF16)
    r0 = W_IN_COLS["forget"]
    ft = _dot_nt(w_ref[0, r0:r0 + 2 * N_HEADS, :].astype(BF16), h)
    logf_ref[...] = _log_sigmoid(ft[0:N_HEADS, :] + bf_ref[...])


def _inproj(x, gain, w_in, layer, b_col):
    n_tok, d_model = x.shape
    full = lambda w: pl.BlockSpec((n_tok, w), lambda i: (0, 0))
    w_spec = pl.BlockSpec((1,) + w_in.shape[1:], lambda i: (layer, 0, 0), pipeline_mode=pl.Buffered(1))
    out_shape = (
        [jax.ShapeDtypeStruct((n_tok, WIDTH), BF16)] * 6
        + [jax.ShapeDtypeStruct((n_tok, WIDTH), F32)] * 2
        + [jax.ShapeDtypeStruct((N_HEADS, n_tok), F32)]
        + [jax.ShapeDtypeStruct((n_tok, WIDTH), F32)] * 2
    )
    out_specs = ([full(WIDTH) for _ in range(8)]
                 + [pl.BlockSpec((N_HEADS, n_tok), lambda i: (0, 0)), full(WIDTH), full(WIDTH)])
    return pl.pallas_call(
        _inproj_kernel,
        grid=(1,),
        in_specs=[full(d_model), _resident((1, d_model)), w_spec, _resident((N_HEADS, 1))],
        out_specs=out_specs,
        out_shape=out_shape,
        compiler_params=_params(),
        name="inproj_sample",
    )(x, gain, w_in, b_col)


def _inproj_t_kernel(x_ref, g_ref, w_ref, bf_ref,
                     qat_ref, qbt_ref, kat_ref, vat_ref, kbt_ref, vbt_ref,
                     kat32_ref, vat32_ref, logf_ref, kbt32_ref, vbt32_ref, wt_ref, *, tiles_per_seq):
    @pl.when(pl.program_id(0) == 0)
    def _():
        for g, name in enumerate(("ka", "va", "kb", "vb", "qa", "qb")):
            r0 = W_IN_COLS[name]
            wt_ref[g * WIDTH:(g + 1) * WIDTH, :] = w_ref[0, r0:r0 + WIDTH, :].astype(BF16)
        r0 = W_IN_COLS["forget"]
        ft = w_ref[0, r0:r0 + 2 * N_HEADS, :]
        live = lax.broadcasted_iota(jnp.int32, ft.shape, 0) < N_HEADS
        wt_ref[6 * WIDTH:6 * WIDTH + 2 * N_HEADS, :] = jnp.where(live, ft, 0.0).astype(BF16)

    n_sub = 2
    sub = x_ref.shape[0] // n_sub
    hs, kept = {}, []

    def norm(i):
        hs[i] = _rms(x_ref[i * sub:(i + 1) * sub, :], g_ref[...]).astype(BF16)

    def project(i, after_first=None):
        cols = slice(i * sub, (i + 1) * sub)

        def proj_t(c, rows=WIDTH):
            return _dot_nt(wt_ref[c * WIDTH:c * WIDTH + rows, :], hs[i])

        z = proj_t(0)
        kat32_ref[0, :, cols] = z
        kat_ref[0, :, cols] = z.astype(BF16)
        if after_first is not None:
            after_first()
        z = proj_t(1)
        vat32_ref[0, :, cols] = z
        vat_ref[0, :, cols] = z.astype(BF16)
        zk = proj_t(2)
        kbt_ref[0, :, cols] = zk.astype(BF16)
        zv = proj_t(3)
        vbt_ref[0, :, cols] = zv.astype(BF16)
        qat_ref[0, :, cols] = (proj_t(4) * (SCALE * LOG2E)).astype(BF16)
        zq = proj_t(5, WIDTH + 2 * N_HEADS)
        qbt_ref[0, :, cols] = (zq[0:WIDTH] * (SCALE * LOG2E)).astype(BF16)
        logf_ref[0, :, cols] = _log_sigmoid(zq[WIDTH:WIDTH + N_HEADS] + bf_ref[...])
        kept.append((cols, zk, zv))

    norm(0)
    for i in range(n_sub):
        project(i, (lambda j=i + 1: norm(j)) if i + 1 < n_sub else None)

    @pl.when(pl.program_id(0) % tiles_per_seq == tiles_per_seq - 1)
    def _():
        for cols, zk, zv in kept:
            kbt32_ref[0, :, cols] = zk
            vbt32_ref[0, :, cols] = zv


def _inproj_t(x, gain, w_in, layer, b_col, *, batch, seq, tm):
    n_tok, d_model = x.shape
    tps = seq // tm
    w_spec = pl.BlockSpec((1,) + w_in.shape[1:], lambda i: (layer, 0, 0), pipeline_mode=pl.Buffered(1))
    feat = lambda: pl.BlockSpec((1, WIDTH, tm), lambda i: (i // tps, 0, i % tps))
    keep = lambda: pl.BlockSpec((1, WIDTH, tm), lambda i: (i // tps, 0, 0))
    out_shape = (
        [jax.ShapeDtypeStruct((batch, WIDTH, seq), BF16)] * 6
        + [jax.ShapeDtypeStruct((batch, WIDTH, seq), F32)] * 2
        + [jax.ShapeDtypeStruct((batch, N_HEADS, seq), F32)]
        + [jax.ShapeDtypeStruct((batch, WIDTH, tm), F32)] * 2
    )
    out_specs = ([feat() for _ in range(8)]
                 + [pl.BlockSpec((1, N_HEADS, tm), lambda i: (i // tps, 0, i % tps)), keep(), keep()])
    return pl.pallas_call(
        functools.partial(_inproj_t_kernel, tiles_per_seq=tps),
        grid=(n_tok // tm,),
        in_specs=[
            pl.BlockSpec((tm, d_model), lambda i: (i, 0)),
            _resident((1, d_model)),
            w_spec,
            _resident((N_HEADS, 1)),
        ],
        out_specs=out_specs,
        out_shape=out_shape,
        scratch_shapes=[pltpu.VMEM((6 * WIDTH + 2 * N_HEADS, d_model), BF16)],
        compiler_params=_params(),
        name="inproj",
    )(x, gain, w_in, b_col)


def _lane_cumsum(x):
    n = x.shape[-1]
    lane = lax.broadcasted_iota(jnp.int32, x.shape, x.ndim - 1)
    shift = 1
    while shift < n:
        x = x + jnp.where(lane >= shift, pltpu.roll(x, shift, axis=x.ndim - 1), 0.0)
        shift *= 2
    return x


def _split3(x):
    hi = x.astype(BF16).astype(F32)
    rest = x - hi
    mid = rest.astype(BF16).astype(F32)
    return hi, mid, rest - mid


def _cumsum_kernel(lf_ref, c2_ref, kf_ref):
    c2 = _lane_cumsum(lf_ref[...]) * LOG2E
    c2_ref[...] = c2
    seq = c2.shape[1]
    hi, mid, lo = _split3(c2)
    pad = jnp.zeros((PAIR - 3 * N_HEADS, seq), F32)
    for b in range(kf_ref.shape[0]):
        rows = slice(b * N_HEADS, (b + 1) * N_HEADS)
        feat = jnp.concatenate([hi[rows], mid[rows], lo[rows], pad], axis=0)
        kf_ref[b] = feat.T.astype(BF16)


def _cumsum(logf, *, batch, seq):
    return pl.pallas_call(
        _cumsum_kernel,
        out_shape=[
            jax.ShapeDtypeStruct((batch * N_HEADS, seq), F32),
            jax.ShapeDtypeStruct((batch, seq, PAIR), BF16),
        ],
        compiler_params=_params(),
        name="cumsum",
    )(logf)


def _split_heads_q(q):
    lane = lax.broadcasted_iota(jnp.int32, q.shape, 1)
    zero = jnp.zeros_like(q)
    return jnp.concatenate([jnp.where(lane < HEAD_DIM, q, zero),
                            jnp.where(lane >= HEAD_DIM, q, zero)], axis=0)


def _augment_v(v, h, axis):
    idx = lax.broadcasted_iota(jnp.int32, v.shape, axis)
    own = (idx < HEAD_DIM) if h == 0 else (idx >= HEAD_DIM)
    return jnp.where(own, v, jnp.ones_like(v))


def _merge_heads_out(acc0, acc1):
    lane = lax.broadcasted_iota(jnp.int32, acc0.shape, 1)
    first = lane < HEAD_DIM
    num = jnp.where(first, acc0, acc1)
    den = pltpu.roll(jnp.where(first, acc1, acc0), HEAD_DIM, axis=1)
    return num / den


def _pair_queries(qt):
    n = qt.shape[1]
    q2 = jnp.concatenate([qt, qt], axis=1)
    r = lax.broadcasted_iota(jnp.int32, q2.shape, 0)
    c = lax.broadcasted_iota(jnp.int32, q2.shape, 1)
    own = jnp.right_shift(r, 6) == jnp.where(c >= n, 1, 0)
    return jnp.where(own, q2, jnp.zeros_like(q2))


def _merge_heads_out_t(acc0, acc1):
    first = lax.broadcasted_iota(jnp.int32, acc0.shape, 0) < HEAD_DIM
    num = jnp.where(first, acc0, acc1)
    den = jnp.where(first, acc0[HEAD_DIM:HEAD_DIM + 1, :], acc1[0:1, :])
    return (num / den).T


def _weighted_values(vaug, p_blocks, key_ranges):
    accs = []
    for h in range(2):
        cols = slice(h * Q_TILE, (h + 1) * Q_TILE)
        acc = None
        for p, (k0, k1) in zip(p_blocks, key_ranges):
            part = _dot(vaug[h][:, k0:k1], p[:, cols])
            acc = part if acc is None else acc + part
        accs.append(acc)
    return accs


def _fox_kernel(qt_ref, kt_ref, vt_ref, c2_ref, kf_ref, o_ref, v0_ref, v1_ref, kx_ref, *, seq):
    for pp in range(PAIRS_PER_STEP):
        feats = slice(pp * PAIR, (pp + 1) * PAIR)
        _fox_pair(PAIRS_PER_STEP * pl.program_id(1) + pp,
                  qt_ref.at[:, feats, :], kt_ref.at[:, feats, :], vt_ref.at[:, feats, :],
                  c2_ref.at[:, pp:pp + 1], kf_ref, o_ref.at[:, feats],
                  v0_ref.at[pp], v1_ref.at[pp], kx_ref.at[pp], seq=seq)


def _fox_pair(pair, qt_ref, kt_ref, vt_ref, c2_ref, kf_ref, o_ref, v0_ref, v1_ref, kx_ref, *, seq):
    vt = vt_ref[0]
    v0_ref[...] = _augment_v(vt, 0, 0)
    v1_ref[...] = _augment_v(vt, 1, 0)
    vaug = (v0_ref, v1_ref)

    kx_ref[:, 0:PAIR] = kt_ref[0].astype(F32).T.astype(BF16)
    kx_ref[:, PAIR:2 * PAIR] = kf_ref[0]
    er = lax.broadcasted_iota(jnp.int32, (PAIR, 2 * Q_TILE), 0)
    ec = lax.broadcasted_iota(jnp.int32, (PAIR, 2 * Q_TILE), 1)
    head = 2 * pair + jnp.where(ec >= Q_TILE, 1, 0)
    piece = er - head
    coef = jnp.where((piece == 0) | (piece == N_HEADS) | (piece == 2 * N_HEADS), -1.0, 0.0).astype(BF16)
    key = lax.broadcasted_iota(jnp.int32, (Q_TILE, 2 * Q_TILE), 0)
    qry = jnp.bitwise_and(lax.broadcasted_iota(jnp.int32, (Q_TILE, 2 * Q_TILE), 1), Q_TILE - 1)
    causal = key <= qry

    n_qb = seq // Q_TILE

    def scores(qb):
        qs, ke = qb * Q_TILE, (qb + 1) * Q_TILE
        rhs = jnp.concatenate([_pair_queries(qt_ref[0, :, qs:ke]), coef], axis=0)
        return _dot(kx_ref[0:ke, :], rhs)

    def softmax(qb, s):
        qs, ke = qb * Q_TILE, (qb + 1) * Q_TILE
        cq = jnp.concatenate([c2_ref[0, 0, 0:1, qs:ke], c2_ref[0, 0, 1:2, qs:ke]], axis=1)
        t_diag = jnp.where(causal, s[qs:ke, :], -jnp.inf)
        m = jnp.max(t_diag, axis=0, keepdims=True)
        if qb > 0:
            t_past = s[0:qs, :]
            m = jnp.maximum(m, jnp.max(t_past, axis=0, keepdims=True))
        shift = cq - (m + cq)
        p_blocks = [jnp.exp2(t_diag + shift).astype(BF16)]
        key_ranges = [(qs, ke)]
        if qb > 0:
            p_blocks.append(jnp.exp2(t_past + shift).astype(BF16))
            key_ranges.append((0, qs))
        return p_blocks, key_ranges

    def finish(qb, p_blocks, key_ranges):
        acc0, acc1 = _weighted_values(vaug, p_blocks, key_ranges)
        o_ref[qb * Q_TILE:(qb + 1) * Q_TILE, :] = _merge_heads_out_t(acc0, acc1).astype(o_ref.dtype)

    order = list(range(n_qb - 1, -1, -1))
    s_next = scores(order[0])
    pending = None
    for i, qb in enumerate(order):
        s = s_next
        if i + 1 < n_qb:
            s_next = scores(order[i + 1])
        probs = softmax(qb, s)
        if pending is not None:
            finish(*pending)
        pending = (qb, *probs)
    finish(*pending)


def _fox(qt, kt, vt, c2, kf, *, batch, seq):
    n_tok = batch * seq
    pps = PAIRS_PER_STEP
    feat = lambda: pl.BlockSpec((1, pps * PAIR, seq), lambda b, p: (b, p, 0))
    return pl.pallas_call(
        functools.partial(_fox_kernel, seq=seq),
        grid=(batch, N_PAIRS // pps),
        in_specs=[
            feat(), feat(), feat(),
            pl.BlockSpec((1, pps, 2, seq), lambda b, p: (b, p, 0, 0)),
            pl.BlockSpec((1, seq, PAIR), lambda b, p: (b, 0, 0)),
        ],
        out_specs=pl.BlockSpec((seq, pps * PAIR), lambda b, p: (b, p)),
        out_shape=jax.ShapeDtypeStruct((n_tok, WIDTH), BF16),
        scratch_shapes=[pltpu.VMEM((pps, PAIR, seq), BF16), pltpu.VMEM((pps, PAIR, seq), BF16),
                        pltpu.VMEM((pps, seq, 2 * PAIR), BF16)],
        compiler_params=_params(),
        name="fox",
    )(qt, kt, vt, c2, kf)


def _bias_kernel(rb_ref, tabm_ref, tabr_ref):
    rb = rb_ref[...]
    n_rows = rb.shape[1]
    r = lax.broadcasted_iota(jnp.int32, (n_rows, TOEPLITZ), 0)
    m = lax.broadcasted_iota(jnp.int32, (n_rows, TOEPLITZ), 1)
    e = jnp.where(m < TOEPLITZ // 2, m, m - TOEPLITZ)
    idx = jnp.clip(LEFT_REACH - Q_TILE - e, -REL_CLIP, REL_CLIP) + REL_CLIP
    onehot = jnp.where(r == idx, 1.0, 0.0).astype(BF16)
    hi = rb.astype(BF16)
    rest = rb - hi.astype(F32)
    mid = rest.astype(BF16)
    lo = (rest - mid.astype(F32)).astype(BF16)
    u = _dot(hi, onehot) + _dot(mid, onehot) + _dot(lo, onehot)

    i = lax.broadcasted_iota(jnp.int32, (Q_TILE, BAND_KEYS), 0)
    j = lax.broadcasted_iota(jnp.int32, (Q_TILE, BAND_KEYS), 1)
    qc = jnp.right_shift(i, 6)
    kc = jnp.right_shift(j, 6)
    valid = (kc >= qc) & (kc <= qc + LEFT_CHUNKS)
    n_new = tabr_ref.shape[1]
    for h in range(N_HEADS):
        base = jnp.broadcast_to(u[h:h + 1, :], (Q_TILE, TOEPLITZ))
        t = pltpu.roll(base, Q_TILE, axis=1, stride=1, stride_axis=0)[:, 0:BAND_KEYS]
        half = (h % 2) * Q_TILE
        tabm_ref[h // 2, :, half:half + Q_TILE] = jnp.where(valid, t * LOG2E, MASKED).T
        tabr_ref[h] = t[0:n_new, :]


def _bias_tables(rb_pad, *, n_new):
    return pl.pallas_call(
        _bias_kernel,
        out_shape=[
            jax.ShapeDtypeStruct((N_PAIRS, BAND_KEYS, 2 * Q_TILE), F32),
            jax.ShapeDtypeStruct((N_HEADS, n_new, BAND_KEYS), F32),
        ],
        compiler_params=_params(),
        name="bias_tables",
    )(rb_pad)


def _band_kernel(qt_ref, kt_ref, vt_ref, tab_ref, o_ref, v0_ref, v1_ref, k_ref, *, seq):
    for pp in range(PAIRS_PER_STEP):
        feats = slice(pp * PAIR, (pp + 1) * PAIR)
        _band_pair(qt_ref.at[:, feats, :], kt_ref.at[:, feats, :], vt_ref.at[:, feats, :],
                   tab_ref.at[pp:pp + 1], o_ref.at[:, feats],
                   v0_ref.at[pp], v1_ref.at[pp], k_ref.at[pp], seq=seq)


def _band_pair(qt_ref, kt_ref, vt_ref, tab_ref, o_ref, v0_ref, v1_ref, k_ref, *, seq):
    vt = vt_ref[0]
    v0_ref[...] = _augment_v(vt, 0, 0)
    v1_ref[...] = _augment_v(vt, 1, 0)
    vaug = (v0_ref, v1_ref)
    k_ref[...] = kt_ref[0].astype(F32).T.astype(BF16)

    n_qb = seq // Q_TILE

    def scores(qb):
        qs, ke = qb * Q_TILE, (qb + 1) * Q_TILE
        ks = max(0, qs - LEFT_REACH)
        return _dot(k_ref[ks:ke, :], _pair_queries(qt_ref[0, :, qs:ke]))

    def softmax(qb, s):
        n_keys = s.shape[0]
        t = s + tab_ref[0, BAND_KEYS - n_keys:BAND_KEYS, :]
        m = jnp.max(t, axis=0, keepdims=True)
        return jnp.exp2(t - m).astype(BF16)

    def finish(qb, p):
        ke = (qb + 1) * Q_TILE
        acc0, acc1 = _weighted_values(vaug, [p], [(ke - p.shape[0], ke)])
        o_ref[qb * Q_TILE:ke, :] = _merge_heads_out_t(acc0, acc1).astype(o_ref.dtype)

    order = list(range(n_qb - 1, -1, -1))
    s_next = scores(order[0])
    pending = None
    for i, qb in enumerate(order):
        s = s_next
        if i + 1 < n_qb:
            s_next = scores(order[i + 1])
        p = softmax(qb, s)
        if pending is not None:
            finish(*pending)
        pending = (qb, p)
    finish(*pending)


def _band(qt, kt, vt, tabm, *, batch, seq):
    n_tok = batch * seq
    pps = PAIRS_PER_STEP
    feat = lambda: pl.BlockSpec((1, pps * PAIR, seq), lambda p, b: (b, p, 0))
    return pl.pallas_call(
        functools.partial(_band_kernel, seq=seq),
        grid=(N_PAIRS // pps, batch),
        in_specs=[
            feat(), feat(), feat(),
            pl.BlockSpec((pps, BAND_KEYS, 2 * Q_TILE), lambda p, b: (p, 0, 0)),
        ],
        out_specs=pl.BlockSpec((seq, pps * PAIR), lambda p, b: (b, p)),
        out_shape=jax.ShapeDtypeStruct((n_tok, WIDTH), BF16),
        scratch_shapes=[pltpu.VMEM((pps, PAIR, seq), BF16), pltpu.VMEM((pps, PAIR, seq), BF16),
                        pltpu.VMEM((pps, seq, PAIR), BF16)],
        compiler_params=_params(),
        name="band",
    )(qt, kt, vt, tabm)


def _two_part_attention(q2, kt_cache, vt_cache, kn_ref, vn_ref, bias_cache, bias_new, n_q):
    s_c = _dot(q2, kt_cache)
    s_n = _dot_nt(q2, kn_ref[...])
    accs = []
    for h in range(2):
        rows = slice(h * n_q, (h + 1) * n_q)
        t_c = s_c[rows, :] + bias_cache(h)
        t_n = s_n[rows, :] + bias_new(h)
        m = jnp.maximum(jnp.max(t_c, axis=1, keepdims=True), jnp.max(t_n, axis=1, keepdims=True))
        acc = _dot_nt(jnp.exp(t_c - m).astype(BF16), _augment_v(vt_cache, h, 0))
        acc = acc + _dot(jnp.exp(t_n - m).astype(BF16), _augment_v(vn_ref[...], h, 1))
        accs.append(acc)
    return _merge_heads_out(accs[0], accs[1])


def _sample_attn_kernel(qa_ref, ka_ref, va_ref, qb_ref, kb_ref, vb_ref,
                        cak_ref, cav_ref, cbk_ref, cbv_ref, c_ref, tab_ref,
                        oa_ref, ob_ref, kn_ref, vn_ref, *, n_q):
    row = lax.broadcasted_iota(jnp.int32, (n_q, NEW_PAD), 0)
    col = lax.broadcasted_iota(jnp.int32, (n_q, NEW_PAD), 1)
    n_cache = cak_ref.shape[2]
    n_band = cbk_ref.shape[2]

    def stage_new(slot, k_new, v_new):
        kn_ref[slot] = jnp.zeros((NEW_PAD, PAIR), BF16)
        vn_ref[slot] = jnp.zeros((NEW_PAD, PAIR), BF16)
        kn_ref[slot, 0:n_q, :] = k_new
        vn_ref[slot, 0:n_q, :] = v_new
        return kn_ref.at[slot], vn_ref.at[slot]

    for p in range(N_PAIRS):
        feats = slice(p * PAIR, (p + 1) * PAIR)

        c_cache = c_ref[0, p, :, 0:n_cache]
        c_new = c_ref[0, p, :, n_cache:n_cache + NEW_PAD]

        def fox_cache(h, c_cache=c_cache, c_new=c_new):
            cq = jnp.sum(jnp.where(col == row, c_new[h:h + 1, :], 0.0), axis=1, keepdims=True)
            return cq - c_cache[h:h + 1, :]

        def fox_new(h, c_new=c_new):
            cq = jnp.sum(jnp.where(col == row, c_new[h:h + 1, :], 0.0), axis=1, keepdims=True)
            return jnp.where(col <= row, cq - c_new[h:h + 1, :], -jnp.inf)

        kn, vn = stage_new(2 * p, ka_ref[:, feats], va_ref[:, feats])
        oa = _two_part_attention(_split_heads_q(qa_ref[:, feats]), cak_ref[0, feats, :].astype(BF16),
                                 cav_ref[0, feats, :].astype(BF16), kn, vn, fox_cache, fox_new, n_q)
        oa_ref[:, feats] = oa.astype(oa_ref.dtype)

        def band_cache(h, p=p):
            return tab_ref[2 * p + h, :, 0:n_band]

        def band_new(h, p=p):
            return jnp.where(col < n_q, tab_ref[2 * p + h, :, n_band:n_band + NEW_PAD], -jnp.inf)

        kn, vn = stage_new(2 * p + 1, kb_ref[:, feats], vb_ref[:, feats])
        ob = _two_part_attention(_split_heads_q(qb_ref[:, feats]), cbk_ref[0, feats, :].astype(BF16),
                                 cbv_ref[0, feats, :].astype(BF16), kn, vn, band_cache, band_new, n_q)
        ob_ref[:, feats] = ob.astype(ob_ref.dtype)


def _rows_cumsum_kernel(x_ref, c_ref):
    c_ref[...] = _lane_cumsum(x_ref[...])


def _rows_cumsum(x):
    return pl.pallas_call(
        _rows_cumsum_kernel,
        out_shape=jax.ShapeDtypeStruct(x.shape, F32),
        compiler_params=_params(),
        name="rows_cumsum",
    )(x)


def _sample_attn(qa, ka, va, qb, kb, vb, cakt, cavt, cbkt, cbvt, c_all, tabr, *, batch, n_q):
    n_tok = batch * n_q
    past = cakt.shape[2]
    n_band = cbkt.shape[2]
    new = lambda: pl.BlockSpec((n_q, WIDTH), lambda b: (b, 0))
    cache = lambda n: pl.BlockSpec((1, WIDTH, n), lambda b: (b, 0, 0))
    return pl.pallas_call(
        functools.partial(_sample_attn_kernel, n_q=n_q),
        grid=(batch,),
        in_specs=[
            new(), new(), new(), new(), new(), new(),
            cache(past), cache(past), cache(n_band), cache(n_band),
            pl.BlockSpec((1, N_PAIRS, 2, past + NEW_PAD), lambda b: (b, 0, 0, 0)),
            _resident(tabr.shape),
        ],
        out_specs=[new(), new()],
        out_shape=[jax.ShapeDtypeStruct((n_tok, WIDTH), BF16)] * 2,
        scratch_shapes=[pltpu.VMEM((2 * N_PAIRS, NEW_PAD, PAIR), BF16),
                        pltpu.VMEM((2 * N_PAIRS, NEW_PAD, PAIR), BF16)],
        compiler_params=_params(),
        name="sample_attn",
    )(qa, ka, va, qb, kb, vb, cakt, cavt, cbkt, cbvt, c_all, tabr)


def _post_kernel(x_ref, oa_ref, ob_ref, ga_ref, gb_ref, wo_ref, gpost_ref, gpre_ref,
                 wg_ref, wu_ref, wd_ref, gffn_ref, y_ref, *, n_sub):
    sub = x_ref.shape[0] // n_sub
    rows = [slice(i * sub, (i + 1) * sub) for i in range(n_sub)]
    st = [dict() for _ in range(n_sub)]

    def norm(i):
        st[i]["na"] = _rms(oa_ref[rows[i], :].astype(F32), ga_ref[...]).astype(BF16)
        st[i]["nb"] = _rms(ob_ref[rows[i], :].astype(F32), gb_ref[...]).astype(BF16)

    def mix(i):
        st[i]["mix"] = _dot(st[i]["na"], wo_ref[0:WIDTH, :]) + _dot(st[i]["nb"], wo_ref[WIDTH:2 * WIDTH, :])

    def mid(i):
        x1 = x_ref[rows[i], :] + _rms(st[i]["mix"], gpost_ref[...])
        st[i]["x1"] = x1
        st[i]["hf"] = _rms(x1, gpre_ref[...]).astype(BF16)

    def gate_up(i):
        st[i]["g"] = _dot(st[i]["hf"], wg_ref[...])
        st[i]["u"] = _dot(st[i]["hf"], wu_ref[...])

    def silu(i):
        g = st[i]["g"]
        st[i]["a"] = (g * jax.nn.sigmoid(g) * st[i]["u"]).astype(BF16)

    def down(i):
        st[i]["ff"] = _dot(st[i]["a"], wd_ref[...])

    def fin(i):
        y_ref[rows[i], :] = st[i]["x1"] + _rms(st[i]["ff"], gffn_ref[...])

    vpu_stages = [norm, mid, silu, fin]
    mxu_stages = [mix, gate_up, down]
    norm(0)
    for k, mxu in enumerate(mxu_stages):
        for i in range(n_sub):
            mxu(i)
            if i + 1 < n_sub:
                vpu_stages[k](i + 1)
            else:
                vpu_stages[k + 1](0)
    for i in range(1, n_sub):
        fin(i)


def _post(x, oa, ob, ga, gb, wo, gpost, gpre, wg, wu, wd, gffn, *, tm, n_sub):
    n_tok, d_model = x.shape
    row = lambda w: pl.BlockSpec((tm, w), lambda i: (i, 0))
    return pl.pallas_call(
        functools.partial(_post_kernel, n_sub=n_sub),
        grid=(n_tok // tm,),
        in_specs=[
            row(d_model), row(WIDTH), row(WIDTH),
            _resident(ga.shape), _resident(gb.shape), _resident(wo.shape),
            _resident(gpost.shape), _resident(gpre.shape),
            _resident(wg.shape), _resident(wu.shape), _resident(wd.shape), _resident(gffn.shape),
        ],
        out_specs=row(d_model),
        out_shape=jax.ShapeDtypeStruct((n_tok, d_model), F32),
        compiler_params=_params(),
        name="post",
    )(x, oa, ob, ga, gb, wo, gpost, gpre, wg, wu, wd, gffn)


def _row(v):
    return v.reshape(1, -1).astype(F32)


def _state_from_feature_major(t, batch, n):
    return t.reshape(batch, N_HEADS, HEAD_DIM, n).transpose(0, 3, 1, 2)


def _cache_feature_major(c):
    b, n = c.shape[0], c.shape[1]
    return c.transpose(0, 2, 3, 1).reshape(b, WIDTH, n)


def kernel(x_prompt, x_sample, cache_a_k, cache_a_v, cache_a_logf, cache_b_k, cache_b_v, norm_mix_pre, w_in, b_forget, rel_bias, gain_out_a, gain_out_b, w_out, norm_mix_post, norm_ffn_pre, w_gate, w_up, w_down, norm_ffn_post):
    batch, seq, d_model = x_prompt.shape
    dec_batch, dec_seq, _ = x_sample.shape
    depth = w_in.shape[0]
    past = cache_a_k.shape[2]
    n_band = cache_b_k.shape[2]
    keep = min(LEFT_REACH, seq)
    assert seq % Q_TILE == 0 and keep == LEFT_REACH and n_band == LEFT_REACH and dec_seq <= CHUNK
    n_tok_p, n_tok_s = batch * seq, dec_batch * dec_seq
    tm_p = keep

    xp = x_prompt.reshape(n_tok_p, d_model)
    xs = x_sample.reshape(n_tok_s, d_model)
    outs = [[] for _ in range(10)]
    w_in_t = jnp.swapaxes(w_in, 1, 2)
    for l in range(depth):
        b_col = b_forget[l].reshape(N_HEADS, 1).astype(F32)
        rb_pad = jnp.pad(rel_bias[l].astype(F32), ((0, 16 - N_HEADS), (0, 384 - (2 * REL_CLIP + 1))))
        tail = (_row(gain_out_a[l]), _row(gain_out_b[l]), w_out[l].astype(BF16), _row(norm_mix_post[l]),
                _row(norm_ffn_pre[l]), w_gate[l].astype(BF16), w_up[l].astype(BF16), w_down[l].astype(BF16),
                _row(norm_ffn_post[l]))
        tabm, tabr = _bias_tables(rb_pad, n_new=dec_seq)

        qat, qbt, kat, vat, kbt, vbt, kat32, vat32, logf_p, kbt32, vbt32 = _inproj_t(
            xp, _row(norm_mix_pre[l]), w_in_t, l, b_col, batch=batch, seq=seq, tm=tm_p)
        c2, kf = _cumsum(logf_p.reshape(batch * N_HEADS, seq), batch=batch, seq=seq)
        oa = _fox(qat, kat, vat, c2.reshape(batch, N_PAIRS, 2, seq), kf, batch=batch, seq=seq)
        ob = _band(qbt, kbt, vbt, tabm, batch=batch, seq=seq)
        xp = _post(xp, oa, ob, *tail, tm=512, n_sub=2)
        outs[0].append(_state_from_feature_major(kat32, batch, seq))
        outs[1].append(_state_from_feature_major(vat32, batch, seq))
        outs[2].append(logf_p.transpose(0, 2, 1))
        outs[3].append(_state_from_feature_major(kbt32, batch, keep))
        outs[4].append(_state_from_feature_major(vbt32, batch, keep))

        qa, ka, va, qb, kb, vb, ka32, va32, logf_t, kb32, vb32 = _inproj(xs, _row(norm_mix_pre[l]), w_in_t, l, b_col)
        lf_all = jnp.concatenate(
            [cache_a_logf[l].astype(F32).transpose(0, 2, 1),
             logf_t.reshape(N_HEADS, dec_batch, dec_seq).transpose(1, 0, 2),
             jnp.zeros((dec_batch, N_HEADS, NEW_PAD - dec_seq), F32)], axis=2)
        c_all = _rows_cumsum(lf_all.reshape(dec_batch * N_HEADS, past + NEW_PAD))
        oa, ob = _sample_attn(
            qa, ka, va, qb, kb, vb,
            _cache_feature_major(cache_a_k[l]), _cache_feature_major(cache_a_v[l]),
            _cache_feature_major(cache_b_k[l]), _cache_feature_major(cache_b_v[l]),
            c_all.reshape(dec_batch, N_PAIRS, 2, past + NEW_PAD), tabr, batch=dec_batch, n_q=dec_seq)
        xs = _post(xs, oa, ob, *tail, tm=n_tok_s, n_sub=1)
        outs[5].append(ka32.reshape(dec_batch, dec_seq, N_HEADS, HEAD_DIM))
        outs[6].append(va32.reshape(dec_batch, dec_seq, N_HEADS, HEAD_DIM))
        outs[7].append(logf_t.T.reshape(dec_batch, dec_seq, N_HEADS))
        outs[8].append(kb32.reshape(dec_batch, dec_seq, N_HEADS, HEAD_DIM))
        outs[9].append(vb32.reshape(dec_batch, dec_seq, N_HEADS, HEAD_DIM))

    st = jnp.stack
    return (xp.reshape(batch, seq, d_model), xs.reshape(dec_batch, dec_seq, d_model),
            *[st(o) for o in outs])
```

```python
import functools

import jax
import jax.numpy as jnp
from jax import lax
from jax.experimental import pallas as pl
from jax.experimental.pallas import tpu as pltpu

F32 = jnp.float32
BF16 = jnp.bfloat16

HEAD_DIM = 64
N_HEADS = 8
WIDTH = N_HEADS * HEAD_DIM
PAIR = 2 * HEAD_DIM
N_PAIRS = N_HEADS // 2
CHUNK = 64
LEFT_CHUNKS = 8
LEFT_REACH = LEFT_CHUNKS * CHUNK
REL_CLIP = 128
EPS = 1e-6
SCALE = HEAD_DIM ** -0.5
LOG2E = 1.4426950408889634
Q_TILE = 256
PAIRS_PER_STEP = 4
BAND_KEYS = LEFT_REACH + Q_TILE
TOEPLITZ = 1024
MASKED = -1e30
NEW_PAD = 128
VMEM_LIMIT = 56 * 1024 * 1024
W_IN_COLS = {"qa": 0, "ka": WIDTH, "va": 2 * WIDTH, "forget": 3 * WIDTH,
             "qb": 3 * WIDTH + N_HEADS, "kb": 4 * WIDTH + N_HEADS, "vb": 5 * WIDTH + N_HEADS}


def _params():
    return pltpu.CompilerParams(vmem_limit_bytes=VMEM_LIMIT)


def _resident(shape):
    return pl.BlockSpec(shape, lambda *_: (0,) * len(shape), pipeline_mode=pl.Buffered(1))


def _rms(x, g):
    ms = jnp.mean(x * x, axis=-1, keepdims=True)
    return x * lax.rsqrt(ms + EPS) * g


def _log_sigmoid(x):
    return jnp.minimum(x, 0.0) - jnp.log1p(jnp.exp(-jnp.abs(x)))


def _dot(a, b):
    return jnp.dot(a, b, preferred_element_type=F32)


def _dot_nt(a, b):
    return lax.dot_general(a, b, (((1,), (1,)), ((), ())), preferred_element_type=F32)


def _inproj_kernel(x_ref, g_ref, w_ref, bf_ref,
                   qa_ref, ka_ref, va_ref, qb_ref, kb_ref, vb_ref,
                   ka32_ref, va32_ref, logf_ref, kb32_ref, vb32_ref):
    h = _rms(x_ref[...], g_ref[...]).astype(BF16)

    def proj(name):
        r0 = W_IN_COLS[name]
        return _dot_nt(h, w_ref[0, r0:r0 + WIDTH, :].astype(BF16))

    qa_ref[...] = (proj("qa") * SCALE).astype(BF16)
    z = proj("ka")
    ka32_ref[...] = z
    ka_ref[...] = z.astype(BF16)
    z = proj("va")
    va32_ref[...] = z
    va_ref[...] = z.astype(BF16)
    qb_ref[...] = (proj("qb") * SCALE).astype(BF16)
    z = proj("kb")
    kb32_ref[...] = z
    kb_ref[...] = z.astype(BF16)
    z = proj("vb")
    vb32_ref[...] = z
    vb_ref[...] = z.astype(BF16)
    r0 = W_IN_COLS["forget"]
    ft = _dot_nt(w_ref[0, r0:r0 + 2 * N_HEADS, :].astype(BF16), h)
    logf_ref[...] = _log_sigmoid(ft[0:N_HEADS, :] + bf_ref[...])


def _inproj(x, gain, w_in, layer, b_col):
    n_tok, d_model = x.shape
    full = lambda w: pl.BlockSpec((n_tok, w), lambda i: (0, 0))
    w_spec = pl.BlockSpec((1,) + w_in.shape[1:], lambda i: (layer, 0, 0), pipeline_mode=pl.Buffered(1))
    out_shape = (
        [jax.ShapeDtypeStruct((n_tok, WIDTH), BF16)] * 6
        + [jax.ShapeDtypeStruct((n_tok, WIDTH), F32)] * 2
        + [jax.ShapeDtypeStruct((N_HEADS, n_tok), F32)]
        + [jax.ShapeDtypeStruct((n_tok, WIDTH), F32)] * 2
    )
    out_specs = ([full(WIDTH) for _ in range(8)]
                 + [pl.BlockSpec((N_HEADS, n_tok), lambda i: (0, 0)), full(WIDTH), full(WIDTH)])
    return pl.pallas_call(
        _inproj_kernel,
        grid=(1,),
        in_specs=[full(d_model), _resident((1, d_model)), w_spec, _resident((N_HEADS, 1))],
        out_specs=out_specs,
        out_shape=out_shape,
        compiler_params=_params(),
        name="inproj_sample",
    )(x, gain, w_in, b_col)


def _inproj_t_kernel(x_ref, g_ref, w_ref, bf_ref,
                     qat_ref, qbt_ref, kat_ref, vat_ref, kbt_ref, vbt_ref,
                     kat32_ref, vat32_ref, logf_ref, kbt32_ref, vbt32_ref, wt_ref, *, tiles_per_seq):
    @pl.when(pl.program_id(0) == 0)
    def _():
        for g, name in enumerate(("ka", "va", "kb", "vb", "qa", "qb")):
            r0 = W_IN_COLS[name]
            wt_ref[g * WIDTH:(g + 1) * WIDTH, :] = w_ref[0, r0:r0 + WIDTH, :].astype(BF16)
        r0 = W_IN_COLS["forget"]
        ft = w_ref[0, r0:r0 + 2 * N_HEADS, :]
        live = lax.broadcasted_iota(jnp.int32, ft.shape, 0) < N_HEADS
        wt_ref[6 * WIDTH:6 * WIDTH + 2 * N_HEADS, :] = jnp.where(live, ft, 0.0).astype(BF16)

    sub = Q_TILE
    n_sub = x_ref.shape[0] // sub
    keep_from = x_ref.shape[0] - kbt32_ref.shape[2]
    hs, kept = {}, []

    def norm(i):
        hs[i] = _rms(x_ref[i * sub:(i + 1) * sub, :], g_ref[...]).astype(BF16)

    def project(i, after_first=None):
        cols = slice(i * sub, (i + 1) * sub)

        def proj_t(c, rows=WIDTH):
            return _dot_nt(wt_ref[c * WIDTH:c * WIDTH + rows, :], hs[i])

        z = proj_t(0)
        kat32_ref[0, :, cols] = z
        kat_ref[0, :, cols] = z.astype(BF16)
        if after_first is not None:
            after_first()
        z = proj_t(1)
        vat32_ref[0, :, cols] = z
        vat_ref[0, :, cols] = z.astype(BF16)
        zk = proj_t(2)
        kbt_ref[0, :, cols] = zk.astype(BF16)
        zv = proj_t(3)
        vbt_ref[0, :, cols] = zv.astype(BF16)
        qat_ref[0, :, cols] = (proj_t(4) * (SCALE * LOG2E)).astype(BF16)
        zq = proj_t(5, WIDTH + 2 * N_HEADS)
        qbt_ref[0, :, cols] = (zq[0:WIDTH] * (SCALE * LOG2E)).astype(BF16)
        logf_ref[0, :, cols] = _log_sigmoid(zq[WIDTH:WIDTH + N_HEADS] + bf_ref[...])
        if i * sub >= keep_from:
            kept.append((slice(i * sub - keep_from, (i + 1) * sub - keep_from), zk, zv))

    norm(0)
    for i in range(n_sub):
        project(i, (lambda j=i + 1: norm(j)) if i + 1 < n_sub else None)

    @pl.when(pl.program_id(0) % tiles_per_seq == tiles_per_seq - 1)
    def _():
        for cols, zk, zv in kept:
            kbt32_ref[0, :, cols] = zk
            vbt32_ref[0, :, cols] = zv


def _inproj_t(x, gain, w_in, layer, b_col, *, batch, seq, tm, keep_rows):
    n_tok, d_model = x.shape
    tps = seq // tm
    w_spec = pl.BlockSpec((1,) + w_in.shape[1:], lambda i: (layer, 0, 0), pipeline_mode=pl.Buffered(1))
    feat = lambda: pl.BlockSpec((1, WIDTH, tm), lambda i: (i // tps, 0, i % tps))
    keep = lambda: pl.BlockSpec((1, WIDTH, keep_rows), lambda i: (i // tps, 0, 0))
    out_shape = (
        [jax.ShapeDtypeStruct((batch, WIDTH, seq), BF16)] * 6
        + [jax.ShapeDtypeStruct((batch, WIDTH, seq), F32)] * 2
        + [jax.ShapeDtypeStruct((batch, N_HEADS, seq), F32)]
        + [jax.ShapeDtypeStruct((batch, WIDTH, keep_rows), F32)] * 2
    )
    out_specs = ([feat() for _ in range(8)]
                 + [pl.BlockSpec((1, N_HEADS, tm), lambda i: (i // tps, 0, i % tps)), keep(), keep()])
    return pl.pallas_call(
        functools.partial(_inproj_t_kernel, tiles_per_seq=tps),
        grid=(n_tok // tm,),
        in_specs=[
            pl.BlockSpec((tm, d_model), lambda i: (i, 0)),
            _resident((1, d_model)),
            w_spec,
            _resident((N_HEADS, 1)),
        ],
        out_specs=out_specs,
        out_shape=out_shape,
        scratch_shapes=[pltpu.VMEM((6 * WIDTH + 2 * N_HEADS, d_model), BF16)],
        compiler_params=_params(),
        name="inproj",
    )(x, gain, w_in, b_col)


def _lane_cumsum(x):
    n = x.shape[-1]
    lane = lax.broadcasted_iota(jnp.int32, x.shape, x.ndim - 1)
    shift = 1
    while shift < n:
        x = x + jnp.where(lane >= shift, pltpu.roll(x, shift, axis=x.ndim - 1), 0.0)
        shift *= 2
    return x


def _split3(x):
    hi = x.astype(BF16).astype(F32)
    rest = x - hi
    mid = rest.astype(BF16).astype(F32)
    return hi, mid, rest - mid


def _cumsum_kernel(lf_ref, c2_ref, kf_ref):
    c2 = _lane_cumsum(lf_ref[...]) * LOG2E
    c2_ref[...] = c2
    seq = c2.shape[1]
    hi, mid, lo = _split3(c2)
    pad = jnp.zeros((PAIR - 3 * N_HEADS, seq), F32)
    for b in range(kf_ref.shape[0]):
        rows = slice(b * N_HEADS, (b + 1) * N_HEADS)
        feat = jnp.concatenate([hi[rows], mid[rows], lo[rows], pad], axis=0)
        kf_ref[b] = feat.T.astype(BF16)


def _cumsum(logf, *, batch, seq):
    return pl.pallas_call(
        _cumsum_kernel,
        out_shape=[
            jax.ShapeDtypeStruct((batch * N_HEADS, seq), F32),
            jax.ShapeDtypeStruct((batch, seq, PAIR), BF16),
        ],
        compiler_params=_params(),
        name="cumsum",
    )(logf)


def _split_heads_q(q):
    lane = lax.broadcasted_iota(jnp.int32, q.shape, 1)
    zero = jnp.zeros_like(q)
    return jnp.concatenate([jnp.where(lane < HEAD_DIM, q, zero),
                            jnp.where(lane >= HEAD_DIM, q, zero)], axis=0)


def _augment_v(v, h, axis):
    idx = lax.broadcasted_iota(jnp.int32, v.shape, axis)
    own = (idx < HEAD_DIM) if h == 0 else (idx >= HEAD_DIM)
    return jnp.where(own, v, jnp.ones_like(v))


def _merge_heads_out(acc0, acc1):
    lane = lax.broadcasted_iota(jnp.int32, acc0.shape, 1)
    first = lane < HEAD_DIM
    num = jnp.where(first, acc0, acc1)
    den = pltpu.roll(jnp.where(first, acc1, acc0), HEAD_DIM, axis=1)
    return num / den


def _pair_queries(qt):
    n = qt.shape[1]
    q2 = jnp.concatenate([qt, qt], axis=1)
    r = lax.broadcasted_iota(jnp.int32, q2.shape, 0)
    c = lax.broadcasted_iota(jnp.int32, q2.shape, 1)
    own = jnp.right_shift(r, 6) == jnp.where(c >= n, 1, 0)
    return jnp.where(own, q2, jnp.zeros_like(q2))


def _merge_heads_out_t(acc0, acc1):
    first = lax.broadcasted_iota(jnp.int32, acc0.shape, 0) < HEAD_DIM
    num = jnp.where(first, acc0, acc1)
    den = jnp.where(first, acc0[HEAD_DIM:HEAD_DIM + 1, :], acc1[0:1, :])
    return (num / den).T


def _weighted_values(vaug, p_blocks, key_ranges):
    accs = []
    for h in range(2):
        cols = slice(h * Q_TILE, (h + 1) * Q_TILE)
        acc = None
        for p, (k0, k1) in zip(p_blocks, key_ranges):
            part = _dot(vaug[h][:, k0:k1], p[:, cols])
            acc = part if acc is None else acc + part
        accs.append(acc)
    return accs


def _fox_kernel(qt_ref, kt_ref, vt_ref, c2_ref, kf_ref, o_ref, v0_ref, v1_ref, kx_ref, *, seq):
    for pp in range(PAIRS_PER_STEP):
        feats = slice(pp * PAIR, (pp + 1) * PAIR)
        _fox_pair(PAIRS_PER_STEP * pl.program_id(1) + pp,
                  qt_ref.at[:, feats, :], kt_ref.at[:, feats, :], vt_ref.at[:, feats, :],
                  c2_ref.at[:, pp:pp + 1], kf_ref, o_ref.at[:, feats],
                  v0_ref.at[pp], v1_ref.at[pp], kx_ref.at[pp], seq=seq)


def _fox_pair(pair, qt_ref, kt_ref, vt_ref, c2_ref, kf_ref, o_ref, v0_ref, v1_ref, kx_ref, *, seq):
    vt = vt_ref[0]
    v0_ref[...] = _augment_v(vt, 0, 0)
    v1_ref[...] = _augment_v(vt, 1, 0)
    vaug = (v0_ref, v1_ref)

    kx_ref[:, 0:PAIR] = kt_ref[0].astype(F32).T.astype(BF16)
    kx_ref[:, PAIR:2 * PAIR] = kf_ref[0]
    er = lax.broadcasted_iota(jnp.int32, (PAIR, 2 * Q_TILE), 0)
    ec = lax.broadcasted_iota(jnp.int32, (PAIR, 2 * Q_TILE), 1)
    head = 2 * pair + jnp.where(ec >= Q_TILE, 1, 0)
    piece = er - head
    coef = jnp.where((piece == 0) | (piece == N_HEADS) | (piece == 2 * N_HEADS), -1.0, 0.0).astype(BF16)
    key = lax.broadcasted_iota(jnp.int32, (Q_TILE, 2 * Q_TILE), 0)
    qry = jnp.bitwise_and(lax.broadcasted_iota(jnp.int32, (Q_TILE, 2 * Q_TILE), 1), Q_TILE - 1)
    causal = key <= qry

    n_qb = seq // Q_TILE

    def scores(qb):
        qs, ke = qb * Q_TILE, (qb + 1) * Q_TILE
        rhs = jnp.concatenate([_pair_queries(qt_ref[0, :, qs:ke]), coef], axis=0)
        return _dot(kx_ref[0:ke, :], rhs)

    def softmax(qb, s):
        qs, ke = qb * Q_TILE, (qb + 1) * Q_TILE
        cq = jnp.concatenate([c2_ref[0, 0, 0:1, qs:ke], c2_ref[0, 0, 1:2, qs:ke]], axis=1)
        t_diag = jnp.where(causal, s[qs:ke, :], -jnp.inf)
        m = jnp.max(t_diag, axis=0, keepdims=True)
        if qb > 0:
            t_past = s[0:qs, :]
            m = jnp.maximum(m, jnp.max(t_past, axis=0, keepdims=True))
        shift = cq - (m + cq)
        p_blocks = [jnp.exp2(t_diag + shift).astype(BF16)]
        key_ranges = [(qs, ke)]
        if qb > 0:
            p_blocks.append(jnp.exp2(t_past + shift).astype(BF16))
            key_ranges.append((0, qs))
        return p_blocks, key_ranges

    def finish(qb, p_blocks, key_ranges):
        acc0, acc1 = _weighted_values(vaug, p_blocks, key_ranges)
        o_ref[qb * Q_TILE:(qb + 1) * Q_TILE, :] = _merge_heads_out_t(acc0, acc1).astype(o_ref.dtype)

    order = list(range(n_qb - 1, -1, -1))
    s_next = scores(order[0])
    pending = None
    for i, qb in enumerate(order):
        s = s_next
        if i + 1 < n_qb:
            s_next = scores(order[i + 1])
        probs = softmax(qb, s)
        if pending is not None:
            finish(*pending)
        pending = (qb, *probs)
    finish(*pending)


def _fox(qt, kt, vt, c2, kf, *, batch, seq):
    n_tok = batch * seq
    pps = PAIRS_PER_STEP
    feat = lambda: pl.BlockSpec((1, pps * PAIR, seq), lambda b, p: (b, p, 0))
    return pl.pallas_call(
        functools.partial(_fox_kernel, seq=seq),
        grid=(batch, N_PAIRS // pps),
        in_specs=[
            feat(), feat(), feat(),
            pl.BlockSpec((1, pps, 2, seq), lambda b, p: (b, p, 0, 0)),
            pl.BlockSpec((1, seq, PAIR), lambda b, p: (b, 0, 0)),
        ],
        out_specs=pl.BlockSpec((seq, pps * PAIR), lambda b, p: (b, p)),
        out_shape=jax.ShapeDtypeStruct((n_tok, WIDTH), BF16),
        scratch_shapes=[pltpu.VMEM((pps, PAIR, seq), BF16), pltpu.VMEM((pps, PAIR, seq), BF16),
                        pltpu.VMEM((pps, seq, 2 * PAIR), BF16)],
        compiler_params=_params(),
        name="fox",
    )(qt, kt, vt, c2, kf)


def _bias_kernel(rb_ref, tabm_ref, tabr_ref):
    rb = rb_ref[...]
    n_rows = rb.shape[1]
    r = lax.broadcasted_iota(jnp.int32, (n_rows, TOEPLITZ), 0)
    m = lax.broadcasted_iota(jnp.int32, (n_rows, TOEPLITZ), 1)
    e = jnp.where(m < TOEPLITZ // 2, m, m - TOEPLITZ)
    idx = jnp.clip(LEFT_REACH - Q_TILE - e, -REL_CLIP, REL_CLIP) + REL_CLIP
    onehot = jnp.where(r == idx, 1.0, 0.0).astype(BF16)
    hi = rb.astype(BF16)
    rest = rb - hi.astype(F32)
    mid = rest.astype(BF16)
    lo = (rest - mid.astype(F32)).astype(BF16)
    u = _dot(hi, onehot) + _dot(mid, onehot) + _dot(lo, onehot)

    i = lax.broadcasted_iota(jnp.int32, (Q_TILE, BAND_KEYS), 0)
    j = lax.broadcasted_iota(jnp.int32, (Q_TILE, BAND_KEYS), 1)
    qc = jnp.right_shift(i, 6)
    kc = jnp.right_shift(j, 6)
    valid = (kc >= qc) & (kc <= qc + LEFT_CHUNKS)
    n_new = tabr_ref.shape[1]
    for h in range(N_HEADS):
        base = jnp.broadcast_to(u[h:h + 1, :], (Q_TILE, TOEPLITZ))
        t = pltpu.roll(base, Q_TILE, axis=1, stride=1, stride_axis=0)[:, 0:BAND_KEYS]
        half = (h % 2) * Q_TILE
        tabm_ref[h // 2, :, half:half + Q_TILE] = jnp.where(valid, t * LOG2E, MASKED).T
        tabr_ref[h] = t[0:n_new, :]


def _bias_tables(rb_pad, *, n_new):
    return pl.pallas_call(
        _bias_kernel,
        out_shape=[
            jax.ShapeDtypeStruct((N_PAIRS, BAND_KEYS, 2 * Q_TILE), F32),
            jax.ShapeDtypeStruct((N_HEADS, n_new, BAND_KEYS), F32),
        ],
        compiler_params=_params(),
        name="bias_tables",
    )(rb_pad)


def _band_kernel(qt_ref, kt_ref, vt_ref, tab_ref, o_ref, v0_ref, v1_ref, k_ref, *, seq):
    for pp in range(PAIRS_PER_STEP):
        feats = slice(pp * PAIR, (pp + 1) * PAIR)
        _band_pair(qt_ref.at[:, feats, :], kt_ref.at[:, feats, :], vt_ref.at[:, feats, :],
                   tab_ref.at[pp:pp + 1], o_ref.at[:, feats],
                   v0_ref.at[pp], v1_ref.at[pp], k_ref.at[pp], seq=seq)


def _band_pair(qt_ref, kt_ref, vt_ref, tab_ref, o_ref, v0_ref, v1_ref, k_ref, *, seq):
    vt = vt_ref[0]
    v0_ref[...] = _augment_v(vt, 0, 0)
    v1_ref[...] = _augment_v(vt, 1, 0)
    vaug = (v0_ref, v1_ref)
    k_ref[...] = kt_ref[0].astype(F32).T.astype(BF16)

    n_qb = seq // Q_TILE

    def scores(qb):
        qs, ke = qb * Q_TILE, (qb + 1) * Q_TILE
        ks = max(0, qs - LEFT_REACH)
        return _dot(k_ref[ks:ke, :], _pair_queries(qt_ref[0, :, qs:ke]))

    def softmax(qb, s):
        n_keys = s.shape[0]
        t = s + tab_ref[0, BAND_KEYS - n_keys:BAND_KEYS, :]
        m = jnp.max(t, axis=0, keepdims=True)
        return jnp.exp2(t - m).astype(BF16)

    def finish(qb, p):
        ke = (qb + 1) * Q_TILE
        acc0, acc1 = _weighted_values(vaug, [p], [(ke - p.shape[0], ke)])
        o_ref[qb * Q_TILE:ke, :] = _merge_heads_out_t(acc0, acc1).astype(o_ref.dtype)

    order = list(range(n_qb - 1, -1, -1))
    s_next = scores(order[0])
    pending = None
    for i, qb in enumerate(order):
        s = s_next
        if i + 1 < n_qb:
            s_next = scores(order[i + 1])
        p = softmax(qb, s)
        if pending is not None:
            finish(*pending)
        pending = (qb, p)
    finish(*pending)


def _band(qt, kt, vt, tabm, *, batch, seq):
    n_tok = batch * seq
    pps = PAIRS_PER_STEP
    feat = lambda: pl.BlockSpec((1, pps * PAIR, seq), lambda p, b: (b, p, 0))
    return pl.pallas_call(
        functools.partial(_band_kernel, seq=seq),
        grid=(N_PAIRS // pps, batch),
        in_specs=[
            feat(), feat(), feat(),
            pl.BlockSpec((pps, BAND_KEYS, 2 * Q_TILE), lambda p, b: (p, 0, 0)),
        ],
        out_specs=pl.BlockSpec((seq, pps * PAIR), lambda p, b: (b, p)),
        out_shape=jax.ShapeDtypeStruct((n_tok, WIDTH), BF16),
        scratch_shapes=[pltpu.VMEM((pps, PAIR, seq), BF16), pltpu.VMEM((pps, PAIR, seq), BF16),
                        pltpu.VMEM((pps, seq, PAIR), BF16)],
        compiler_params=_params(),
        name="band",
    )(qt, kt, vt, tabm)


def _two_part_attention(q2, kt_cache, vt_cache, kn_ref, vn_ref, bias_cache, bias_new, n_q):
    s_c = _dot(q2, kt_cache)
    s_n = _dot_nt(q2, kn_ref[...])
    accs = []
    for h in range(2):
        rows = slice(h * n_q, (h + 1) * n_q)
        t_c = s_c[rows, :] + bias_cache(h)
        t_n = s_n[rows, :] + bias_new(h)
        m = jnp.maximum(jnp.max(t_c, axis=1, keepdims=True), jnp.max(t_n, axis=1, keepdims=True))
        acc = _dot_nt(jnp.exp(t_c - m).astype(BF16), _augment_v(vt_cache, h, 0))
        acc = acc + _dot(jnp.exp(t_n - m).astype(BF16), _augment_v(vn_ref[...], h, 1))
        accs.append(acc)
    return _merge_heads_out(accs[0], accs[1])


def _sample_attn_kernel(qa_ref, ka_ref, va_ref, qb_ref, kb_ref, vb_ref,
                        cak_ref, cav_ref, cbk_ref, cbv_ref, c_ref, tab_ref,
                        oa_ref, ob_ref, kn_ref, vn_ref, *, n_q):
    row = lax.broadcasted_iota(jnp.int32, (n_q, NEW_PAD), 0)
    col = lax.broadcasted_iota(jnp.int32, (n_q, NEW_PAD), 1)
    n_cache = cak_ref.shape[2]
    n_band = cbk_ref.shape[2]

    def stage_new(slot, k_new, v_new):
        kn_ref[slot] = jnp.zeros((NEW_PAD, PAIR), BF16)
        vn_ref[slot] = jnp.zeros((NEW_PAD, PAIR), BF16)
        kn_ref[slot, 0:n_q, :] = k_new
        vn_ref[slot, 0:n_q, :] = v_new
        return kn_ref.at[slot], vn_ref.at[slot]

    for p in range(N_PAIRS):
        feats = slice(p * PAIR, (p + 1) * PAIR)

        c_cache = c_ref[0, p, :, 0:n_cache]
        c_new = c_ref[0, p, :, n_cache:n_cache + NEW_PAD]

        def fox_cache(h, c_cache=c_cache, c_new=c_new):
            cq = jnp.sum(jnp.where(col == row, c_new[h:h + 1, :], 0.0), axis=1, keepdims=True)
            return cq - c_cache[h:h + 1, :]

        def fox_new(h, c_new=c_new):
            cq = jnp.sum(jnp.where(col == row, c_new[h:h + 1, :], 0.0), axis=1, keepdims=True)
            return jnp.where(col <= row, cq - c_new[h:h + 1, :], -jnp.inf)

        kn, vn = stage_new(2 * p, ka_ref[:, feats], va_ref[:, feats])
        oa = _two_part_attention(_split_heads_q(qa_ref[:, feats]), cak_ref[0, feats, :].astype(BF16),
                                 cav_ref[0, feats, :].astype(BF16), kn, vn, fox_cache, fox_new, n_q)
        oa_ref[:, feats] = oa.astype(oa_ref.dtype)

        def band_cache(h, p=p):
            return tab_ref[2 * p + h, :, 0:n_band]

        def band_new(h, p=p):
            return jnp.where(col < n_q, tab_ref[2 * p + h, :, n_band:n_band + NEW_PAD], -jnp.inf)

        kn, vn = stage_new(2 * p + 1, kb_ref[:, feats], vb_ref[:, feats])
        ob = _two_part_attention(_split_heads_q(qb_ref[:, feats]), cbk_ref[0, feats, :].astype(BF16),
                                 cbv_ref[0, feats, :].astype(BF16), kn, vn, band_cache, band_new, n_q)
        ob_ref[:, feats] = ob.astype(ob_ref.dtype)


def _rows_cumsum_kernel(x_ref, c_ref):
    c_ref[...] = _lane_cumsum(x_ref[...])


def _rows_cumsum(x):
    return pl.pallas_call(
        _rows_cumsum_kernel,
        out_shape=jax.ShapeDtypeStruct(x.shape, F32),
        compiler_params=_params(),
        name="rows_cumsum",
    )(x)


def _sample_attn(qa, ka, va, qb, kb, vb, cakt, cavt, cbkt, cbvt, c_all, tabr, *, batch, n_q):
    n_tok = batch * n_q
    past = cakt.shape[2]
    n_band = cbkt.shape[2]
    new = lambda: pl.BlockSpec((n_q, WIDTH), lambda b: (b, 0))
    cache = lambda n: pl.BlockSpec((1, WIDTH, n), lambda b: (b, 0, 0))
    return pl.pallas_call(
        functools.partial(_sample_attn_kernel, n_q=n_q),
        grid=(batch,),
        in_specs=[
            new(), new(), new(), new(), new(), new(),
            cache(past), cache(past), cache(n_band), cache(n_band),
            pl.BlockSpec((1, N_PAIRS, 2, past + NEW_PAD), lambda b: (b, 0, 0, 0)),
            _resident(tabr.shape),
        ],
        out_specs=[new(), new()],
        out_shape=[jax.ShapeDtypeStruct((n_tok, WIDTH), BF16)] * 2,
        scratch_shapes=[pltpu.VMEM((2 * N_PAIRS, NEW_PAD, PAIR), BF16),
                        pltpu.VMEM((2 * N_PAIRS, NEW_PAD, PAIR), BF16)],
        compiler_params=_params(),
        name="sample_attn",
    )(qa, ka, va, qb, kb, vb, cakt, cavt, cbkt, cbvt, c_all, tabr)


def _post_kernel(x_ref, oa_ref, ob_ref, ga_ref, gb_ref, wo_ref, gpost_ref, gpre_ref,
                 wg_ref, wu_ref, wd_ref, gffn_ref, y_ref, *, n_sub):
    sub = x_ref.shape[0] // n_sub
    rows = [slice(i * sub, (i + 1) * sub) for i in range(n_sub)]
    st = [dict() for _ in range(n_sub)]

    def norm(i):
        st[i]["na"] = _rms(oa_ref[rows[i], :].astype(F32), ga_ref[...]).astype(BF16)
        st[i]["nb"] = _rms(ob_ref[rows[i], :].astype(F32), gb_ref[...]).astype(BF16)

    def mix(i):
        st[i]["mix"] = _dot(st[i]["na"], wo_ref[0:WIDTH, :]) + _dot(st[i]["nb"], wo_ref[WIDTH:2 * WIDTH, :])

    def mid(i):
        x1 = x_ref[rows[i], :] + _rms(st[i]["mix"], gpost_ref[...])
        st[i]["x1"] = x1
        st[i]["hf"] = _rms(x1, gpre_ref[...]).astype(BF16)

    def gate_up(i):
        st[i]["g"] = _dot(st[i]["hf"], wg_ref[...])
        st[i]["u"] = _dot(st[i]["hf"], wu_ref[...])

    def silu(i):
        g = st[i]["g"]
        st[i]["a"] = (g * jax.nn.sigmoid(g) * st[i]["u"]).astype(BF16)

    def down(i):
        st[i]["ff"] = _dot(st[i]["a"], wd_ref[...])

    def fin(i):
        y_ref[rows[i], :] = st[i]["x1"] + _rms(st[i]["ff"], gffn_ref[...])

    vpu_stages = [norm, mid, silu, fin]
    mxu_stages = [mix, gate_up, down]
    norm(0)
    for k, mxu in enumerate(mxu_stages):
        for i in range(n_sub):
            mxu(i)
            if i + 1 < n_sub:
                vpu_stages[k](i + 1)
            else:
                vpu_stages[k + 1](0)
    for i in range(1, n_sub):
        fin(i)


def _post(x, oa, ob, ga, gb, wo, gpost, gpre, wg, wu, wd, gffn, *, tm, n_sub):
    n_tok, d_model = x.shape
    row = lambda w: pl.BlockSpec((tm, w), lambda i: (i, 0))
    return pl.pallas_call(
        functools.partial(_post_kernel, n_sub=n_sub),
        grid=(n_tok // tm,),
        in_specs=[
            row(d_model), row(WIDTH), row(WIDTH),
            _resident(ga.shape), _resident(gb.shape), _resident(wo.shape),
            _resident(gpost.shape), _resident(gpre.shape),
            _resident(wg.shape), _resident(wu.shape), _resident(wd.shape), _resident(gffn.shape),
        ],
        out_specs=row(d_model),
        out_shape=jax.ShapeDtypeStruct((n_tok, d_model), F32),
        compiler_params=_params(),
        name="post",
    )(x, oa, ob, ga, gb, wo, gpost, gpre, wg, wu, wd, gffn)


def _row(v):
    return v.reshape(1, -1).astype(F32)


def _state_from_feature_major(t, batch, n):
    return t.reshape(batch, N_HEADS, HEAD_DIM, n).transpose(0, 3, 1, 2)


def _cache_feature_major(c):
    b, n = c.shape[0], c.shape[1]
    return c.transpose(0, 2, 3, 1).reshape(b, WIDTH, n)


def kernel(x_prompt, x_sample, cache_a_k, cache_a_v, cache_a_logf, cache_b_k, cache_b_v, norm_mix_pre, w_in, b_forget, rel_bias, gain_out_a, gain_out_b, w_out, norm_mix_post, norm_ffn_pre, w_gate, w_up, w_down, norm_ffn_post):
    batch, seq, d_model = x_prompt.shape
    dec_batch, dec_seq, _ = x_sample.shape
    depth = w_in.shape[0]
    past = cache_a_k.shape[2]
    n_band = cache_b_k.shape[2]
    keep = min(LEFT_REACH, seq)
    assert seq % Q_TILE == 0 and keep == LEFT_REACH and n_band == LEFT_REACH and dec_seq <= CHUNK
    n_tok_p, n_tok_s = batch * seq, dec_batch * dec_seq
    tm_p = 2 * keep
    assert seq % tm_p == 0

    xp = x_prompt.reshape(n_tok_p, d_model)
    xs = x_sample.reshape(n_tok_s, d_model)
    outs = [[] for _ in range(10)]
    w_in_t = jnp.swapaxes(w_in, 1, 2)
    for l in range(depth):
        b_col = b_forget[l].reshape(N_HEADS, 1).astype(F32)
        rb_pad = jnp.pad(rel_bias[l].astype(F32), ((0, 16 - N_HEADS), (0, 384 - (2 * REL_CLIP + 1))))
        tail = (_row(gain_out_a[l]), _row(gain_out_b[l]), w_out[l].astype(BF16), _row(norm_mix_post[l]),
                _row(norm_ffn_pre[l]), w_gate[l].astype(BF16), w_up[l].astype(BF16), w_down[l].astype(BF16),
                _row(norm_ffn_post[l]))
        tabm, tabr = _bias_tables(rb_pad, n_new=dec_seq)

        qat, qbt, kat, vat, kbt, vbt, kat32, vat32, logf_p, kbt32, vbt32 = _inproj_t(
            xp, _row(norm_mix_pre[l]), w_in_t, l, b_col, batch=batch, seq=seq, tm=tm_p, keep_rows=keep)
        c2, kf = _cumsum(logf_p.reshape(batch * N_HEADS, seq), batch=batch, seq=seq)
        oa = _fox(qat, kat, vat, c2.reshape(batch, N_PAIRS, 2, seq), kf, batch=batch, seq=seq)
        ob = _band(qbt, kbt, vbt, tabm, batch=batch, seq=seq)
        xp = _post(xp, oa, ob, *tail, tm=512, n_sub=2)
        outs[0].append(_state_from_feature_major(kat32, batch, seq))
        outs[1].append(_state_from_feature_major(vat32, batch, seq))
        outs[2].append(logf_p.transpose(0, 2, 1))
        outs[3].append(_state_from_feature_major(kbt32, batch, keep))
        outs[4].append(_state_from_feature_major(vbt32, batch, keep))

        qa, ka, va, qb, kb, vb, ka32, va32, logf_t, kb32, vb32 = _inproj(xs, _row(norm_mix_pre[l]), w_in_t, l, b_col)
        lf_all = jnp.concatenate(
            [cache_a_logf[l].astype(F32).transpose(0, 2, 1),
             logf_t.reshape(N_HEADS, dec_batch, dec_seq).transpose(1, 0, 2),
             jnp.zeros((dec_batch, N_HEADS, NEW_PAD - dec_seq), F32)], axis=2)
        c_all = _rows_cumsum(lf_all.reshape(dec_batch * N_HEADS, past + NEW_PAD))
        oa, ob = _sample_attn(
            qa, ka, va, qb, kb, vb,
            _cache_feature_major(cache_a_k[l]), _cache_feature_major(cache_a_v[l]),
            _cache_feature_major(cache_b_k[l]), _cache_feature_major(cache_b_v[l]),
            c_all.reshape(dec_batch, N_PAIRS, 2, past + NEW_PAD), tabr, batch=dec_batch, n_q=dec_seq)
        xs = _post(xs, oa, ob, *tail, tm=n_tok_s, n_sub=1)
        outs[5].append(ka32.reshape(dec_batch, dec_seq, N_HEADS, HEAD_DIM))
        outs[6].append(va32.reshape(dec_batch, dec_seq, N_HEADS, HEAD_DIM))
        outs[7].append(logf_t.T.reshape(dec_batch, dec_seq, N_HEADS))
        outs[8].append(kb32.reshape(dec_batch, dec_seq, N_HEADS, HEAD_DIM))
        outs[9].append(vb32.reshape(dec_batch, dec_seq, N_HEADS, HEAD_DIM))

    st = jnp.stack
    return (xp.reshape(batch, seq, d_model), xs.reshape(dec_batch, dec_seq, d_model),
            *[st(o) for o in outs])
```

```python
import functools

import jax
import jax.numpy as jnp
from jax import lax
from jax.experimental import pallas as pl
from jax.experimental.pallas import tpu as pltpu

F32 = jnp.float32
BF16 = jnp.bfloat16

HEAD_DIM = 64
N_HEADS = 8
WIDTH = N_HEADS * HEAD_DIM
PAIR = 2 * HEAD_DIM
N_PAIRS = N_HEADS // 2
CHUNK = 64
LEFT_CHUNKS = 8
LEFT_REACH = LEFT_CHUNKS * CHUNK
REL_CLIP = 128
EPS = 1e-6
SCALE = HEAD_DIM ** -0.5
LOG2E = 1.4426950408889634
Q_TILE = 256
PAIRS_PER_STEP = 4
BAND_KEYS = LEFT_REACH + Q_TILE
TOEPLITZ = 1024
MASKED = -1e30
NEW_PAD = 128
VMEM_LIMIT = 56 * 1024 * 1024
W_IN_COLS = {"qa": 0, "ka": WIDTH, "va": 2 * WIDTH, "forget": 3 * WIDTH,
             "qb": 3 * WIDTH + N_HEADS, "kb": 4 * WIDTH + N_HEADS, "vb": 5 * WIDTH + N_HEADS}


def _params():
    return pltpu.CompilerParams(vmem_limit_bytes=VMEM_LIMIT)


def _resident(shape):
    return pl.BlockSpec(shape, lambda *_: (0,) * len(shape), pipeline_mode=pl.Buffered(1))


def _rms(x, g):
    ms = jnp.mean(x * x, axis=-1, keepdims=True)
    return x * lax.rsqrt(ms + EPS) * g


def _log_sigmoid(x):
    return jnp.minimum(x, 0.0) - jnp.log1p(jnp.exp(-jnp.abs(x)))


def _dot(a, b):
    return jnp.dot(a, b, preferred_element_type=F32)


def _dot_nt(a, b):
    return lax.dot_general(a, b, (((1,), (1,)), ((), ())), preferred_element_type=F32)


def _inproj_kernel(x_ref, g_ref, w_ref, bf_ref,
                   qa_ref, ka_ref, va_ref, qb_ref, kb_ref, vb_ref,
                   ka32_ref, va32_ref, logf_ref, kb32_ref, vb32_ref):
    h = _rms(x_ref[...], g_ref[...]).astype(BF16)

    def proj(name):
        r0 = W_IN_COLS[name]
        return _dot_nt(h, w_ref[0, r0:r0 + WIDTH, :].astype(BF16))

    qa_ref[...] = (proj("qa") * SCALE).astype(BF16)
    z = proj("ka")
    ka32_ref[...] = z
    ka_ref[...] = z.astype(BF16)
    z = proj("va")
    va32_ref[...] = z
    va_ref[...] = z.astype(BF16)
    qb_ref[...] = (proj("qb") * SCALE).astype(BF16)
    z = proj("kb")
    kb32_ref[...] = z
    kb_ref[...] = z.astype(BF16)
    z = proj("vb")
    vb32_ref[...] = z
    vb_ref[...] = z.astype(BF16)
    r0 = W_IN_COLS["forget"]
    ft = _dot_nt(w_ref[0, r0:r0 + 2 * N_HEADS, :].astype(BF16), h)
    logf_ref[...] = _log_sigmoid(ft[0:N_HEADS, :] + bf_ref[...])


def _inproj(x, gain, w_in, layer, b_col):
    n_tok, d_model = x.shape
    full = lambda w: pl.BlockSpec((n_tok, w), lambda i: (0, 0))
    w_spec = pl.BlockSpec((1,) + w_in.shape[1:], lambda i: (layer, 0, 0), pipeline_mode=pl.Buffered(1))
    out_shape = (
        [jax.ShapeDtypeStruct((n_tok, WIDTH), BF16)] * 6
        + [jax.ShapeDtypeStruct((n_tok, WIDTH), F32)] * 2
        + [jax.ShapeDtypeStruct((N_HEADS, n_tok), F32)]
        + [jax.ShapeDtypeStruct((n_tok, WIDTH), F32)] * 2
    )
    out_specs = ([full(WIDTH) for _ in range(8)]
                 + [pl.BlockSpec((N_HEADS, n_tok), lambda i: (0, 0)), full(WIDTH), full(WIDTH)])
    return pl.pallas_call(
        _inproj_kernel,
        grid=(1,),
        in_specs=[full(d_model), _resident((1, d_model)), w_spec, _resident((N_HEADS, 1))],
        out_specs=out_specs,
        out_shape=out_shape,
        compiler_params=_params(),
        name="inproj_sample",
    )(x, gain, w_in, b_col)


def _inproj_t_kernel(x_ref, g_ref, w_ref, bf_ref,
                     qat_ref, qbt_ref, kat_ref, vat_ref, kbt_ref, vbt_ref,
                     kat32_ref, vat32_ref, logf_ref, kbt32_ref, vbt32_ref, wt_ref, *, tiles_per_seq):
    @pl.when(pl.program_id(0) == 0)
    def _():
        for g, name in enumerate(("ka", "va", "kb", "vb", "qa", "qb")):
            r0 = W_IN_COLS[name]
            wt_ref[g * WIDTH:(g + 1) * WIDTH, :] = w_ref[0, r0:r0 + WIDTH, :].astype(BF16)
        r0 = W_IN_COLS["forget"]
        ft = w_ref[0, r0:r0 + 2 * N_HEADS, :]
        live = lax.broadcasted_iota(jnp.int32, ft.shape, 0) < N_HEADS
        wt_ref[6 * WIDTH:6 * WIDTH + 2 * N_HEADS, :] = jnp.where(live, ft, 0.0).astype(BF16)

    sub = Q_TILE
    n_sub = x_ref.shape[0] // sub
    keep_from = x_ref.shape[0] - kbt32_ref.shape[2]
    hs, kept = {}, []

    def norm(i):
        hs[i] = _rms(x_ref[i * sub:(i + 1) * sub, :], g_ref[...]).astype(BF16)

    def project(i, after_first=None):
        cols = slice(i * sub, (i + 1) * sub)

        def proj_t(c, rows=WIDTH):
            return _dot_nt(wt_ref[c * WIDTH:c * WIDTH + rows, :], hs[i])

        z = proj_t(0)
        kat32_ref[0, :, cols] = z
        kat_ref[0, :, cols] = z.astype(BF16)
        if after_first is not None:
            after_first()
        z = proj_t(1)
        vat32_ref[0, :, cols] = z
        vat_ref[0, :, cols] = z.astype(BF16)
        zk = proj_t(2)
        kbt_ref[0, :, cols] = zk.astype(BF16)
        zv = proj_t(3)
        vbt_ref[0, :, cols] = zv.astype(BF16)
        qat_ref[0, :, cols] = (proj_t(4) * (SCALE * LOG2E)).astype(BF16)
        zq = proj_t(5, WIDTH + 2 * N_HEADS)
        qbt_ref[0, :, cols] = (zq[0:WIDTH] * (SCALE * LOG2E)).astype(BF16)
        logf_ref[0, :, cols] = _log_sigmoid(zq[WIDTH:WIDTH + N_HEADS] + bf_ref[...])
        if i * sub >= keep_from:
            kept.append((slice(i * sub - keep_from, (i + 1) * sub - keep_from), zk, zv))

    norm(0)
    for i in range(n_sub):
        project(i, (lambda j=i + 1: norm(j)) if i + 1 < n_sub else None)

    @pl.when(pl.program_id(0) % tiles_per_seq == tiles_per_seq - 1)
    def _():
        for cols, zk, zv in kept:
            kbt32_ref[0, :, cols] = zk
            vbt32_ref[0, :, cols] = zv


def _inproj_t(x, gain, w_in, layer, b_col, *, batch, seq, tm, keep_rows):
    n_tok, d_model = x.shape
    tps = seq // tm
    w_spec = pl.BlockSpec((1,) + w_in.shape[1:], lambda i: (layer, 0, 0), pipeline_mode=pl.Buffered(1))
    feat = lambda: pl.BlockSpec((1, WIDTH, tm), lambda i: (i // tps, 0, i % tps))
    keep = lambda: pl.BlockSpec((1, WIDTH, keep_rows), lambda i: (i // tps, 0, 0))
    out_shape = (
        [jax.ShapeDtypeStruct((batch, WIDTH, seq), BF16)] * 6
        + [jax.ShapeDtypeStruct((batch, WIDTH, seq), F32)] * 2
        + [jax.ShapeDtypeStruct((batch, N_HEADS, seq), F32)]
        + [jax.ShapeDtypeStruct((batch, WIDTH, keep_rows), F32)] * 2
    )
    out_specs = ([feat() for _ in range(8)]
                 + [pl.BlockSpec((1, N_HEADS, tm), lambda i: (i // tps, 0, i % tps)), keep(), keep()])
    return pl.pallas_call(
        functools.partial(_inproj_t_kernel, tiles_per_seq=tps),
        grid=(n_tok // tm,),
        in_specs=[
            pl.BlockSpec((tm, d_model), lambda i: (i, 0)),
            _resident((1, d_model)),
            w_spec,
            _resident((N_HEADS, 1)),
        ],
        out_specs=out_specs,
        out_shape=out_shape,
        scratch_shapes=[pltpu.VMEM((6 * WIDTH + 2 * N_HEADS, d_model), BF16)],
        compiler_params=_params(),
        name="inproj",
    )(x, gain, w_in, b_col)


def _lane_cumsum(x):
    n = x.shape[-1]
    lane = lax.broadcasted_iota(jnp.int32, x.shape, x.ndim - 1)
    shift = 1
    while shift < n:
        x = x + jnp.where(lane >= shift, pltpu.roll(x, shift, axis=x.ndim - 1), 0.0)
        shift *= 2
    return x


def _split3(x):
    hi = x.astype(BF16).astype(F32)
    rest = x - hi
    mid = rest.astype(BF16).astype(F32)
    return hi, mid, rest - mid


def _cumsum_kernel(lf_ref, c2_ref, kf_ref):
    c2 = _lane_cumsum(lf_ref[...]) * LOG2E
    c2_ref[...] = c2
    seq = c2.shape[1]
    hi, mid, lo = _split3(c2)
    pad = jnp.zeros((PAIR - 3 * N_HEADS, seq), F32)
    for b in range(kf_ref.shape[0]):
        rows = slice(b * N_HEADS, (b + 1) * N_HEADS)
        feat = jnp.concatenate([hi[rows], mid[rows], lo[rows], pad], axis=0)
        kf_ref[b] = feat.T.astype(BF16)


def _cumsum(logf, *, batch, seq):
    return pl.pallas_call(
        _cumsum_kernel,
        out_shape=[
            jax.ShapeDtypeStruct((batch * N_HEADS, seq), F32),
            jax.ShapeDtypeStruct((batch, seq, PAIR), BF16),
        ],
        compiler_params=_params(),
        name="cumsum",
    )(logf)


def _split_heads_q(q):
    lane = lax.broadcasted_iota(jnp.int32, q.shape, 1)
    zero = jnp.zeros_like(q)
    return jnp.concatenate([jnp.where(lane < HEAD_DIM, q, zero),
                            jnp.where(lane >= HEAD_DIM, q, zero)], axis=0)


def _augment_v(v, h, axis):
    idx = lax.broadcasted_iota(jnp.int32, v.shape, axis)
    own = (idx < HEAD_DIM) if h == 0 else (idx >= HEAD_DIM)
    return jnp.where(own, v, jnp.ones_like(v))


def _merge_heads_out(acc0, acc1):
    lane = lax.broadcasted_iota(jnp.int32, acc0.shape, 1)
    first = lane < HEAD_DIM
    num = jnp.where(first, acc0, acc1)
    den = pltpu.roll(jnp.where(first, acc1, acc0), HEAD_DIM, axis=1)
    return num / den


def _pair_queries(qt):
    n = qt.shape[1]
    q2 = jnp.concatenate([qt, qt], axis=1)
    r = lax.broadcasted_iota(jnp.int32, q2.shape, 0)
    c = lax.broadcasted_iota(jnp.int32, q2.shape, 1)
    own = jnp.right_shift(r, 6) == jnp.where(c >= n, 1, 0)
    return jnp.where(own, q2, jnp.zeros_like(q2))


def _merge_heads_out_t(acc0, acc1):
    first = lax.broadcasted_iota(jnp.int32, acc0.shape, 0) < HEAD_DIM
    num = jnp.where(first, acc0, acc1)
    den = jnp.where(first, acc0[HEAD_DIM:HEAD_DIM + 1, :], acc1[0:1, :])
    return (num / den).T


def _weighted_values(vaug, p_blocks, key_ranges):
    accs = []
    for h in range(2):
        cols = slice(h * Q_TILE, (h + 1) * Q_TILE)
        acc = None
        for p, (k0, k1) in zip(p_blocks, key_ranges):
            part = _dot(vaug[h][:, k0:k1], p[:, cols])
            acc = part if acc is None else acc + part
        accs.append(acc)
    return accs


def _fox_kernel(qt_ref, kt_ref, vt_ref, c2_ref, kf_ref, o_ref, v0_ref, v1_ref, kx_ref, *, seq):
    for pp in range(PAIRS_PER_STEP):
        feats = slice(pp * PAIR, (pp + 1) * PAIR)
        _fox_pair(PAIRS_PER_STEP * pl.program_id(1) + pp,
                  qt_ref.at[:, feats, :], kt_ref.at[:, feats, :], vt_ref.at[:, feats, :],
                  c2_ref.at[:, pp:pp + 1], kf_ref, o_ref.at[:, feats],
                  v0_ref.at[pp], v1_ref.at[pp], kx_ref.at[pp], seq=seq)


def _fox_pair(pair, qt_ref, kt_ref, vt_ref, c2_ref, kf_ref, o_ref, v0_ref, v1_ref, kx_ref, *, seq):
    vt = vt_ref[0]
    v0_ref[...] = _augment_v(vt, 0, 0)
    v1_ref[...] = _augment_v(vt, 1, 0)
    vaug = (v0_ref, v1_ref)

    kx_ref[:, 0:PAIR] = kt_ref[0].astype(F32).T.astype(BF16)
    kx_ref[:, PAIR:2 * PAIR] = kf_ref[0]
    er = lax.broadcasted_iota(jnp.int32, (PAIR, 2 * Q_TILE), 0)
    ec = lax.broadcasted_iota(jnp.int32, (PAIR, 2 * Q_TILE), 1)
    head = 2 * pair + jnp.where(ec >= Q_TILE, 1, 0)
    piece = er - head
    coef = jnp.where((piece == 0) | (piece == N_HEADS) | (piece == 2 * N_HEADS), -1.0, 0.0).astype(BF16)
    key = lax.broadcasted_iota(jnp.int32, (Q_TILE, 2 * Q_TILE), 0)
    qry = jnp.bitwise_and(lax.broadcasted_iota(jnp.int32, (Q_TILE, 2 * Q_TILE), 1), Q_TILE - 1)
    causal = key <= qry

    n_qb = seq // Q_TILE

    def scores(qb):
        qs, ke = qb * Q_TILE, (qb + 1) * Q_TILE
        rhs = jnp.concatenate([_pair_queries(qt_ref[0, :, qs:ke]), coef], axis=0)
        return _dot(kx_ref[0:ke, :], rhs)

    def softmax(qb, s):
        qs, ke = qb * Q_TILE, (qb + 1) * Q_TILE
        cq = jnp.concatenate([c2_ref[0, 0, 0:1, qs:ke], c2_ref[0, 0, 1:2, qs:ke]], axis=1)
        t_diag = jnp.where(causal, s[qs:ke, :], -jnp.inf)
        m = jnp.max(t_diag, axis=0, keepdims=True)
        if qb > 0:
            t_past = s[0:qs, :]
            m = jnp.maximum(m, jnp.max(t_past, axis=0, keepdims=True))
        shift = cq - (m + cq)
        p_blocks = [jnp.exp2(t_diag + shift).astype(BF16)]
        key_ranges = [(qs, ke)]
        if qb > 0:
            p_blocks.append(jnp.exp2(t_past + shift).astype(BF16))
            key_ranges.append((0, qs))
        return p_blocks, key_ranges

    def finish(qb, p_blocks, key_ranges):
        acc0, acc1 = _weighted_values(vaug, p_blocks, key_ranges)
        o_ref[qb * Q_TILE:(qb + 1) * Q_TILE, :] = _merge_heads_out_t(acc0, acc1).astype(o_ref.dtype)

    order = list(range(n_qb - 1, -1, -1))
    s_next = scores(order[0])
    pending = None
    for i, qb in enumerate(order):
        s = s_next
        if i + 1 < n_qb:
            s_next = scores(order[i + 1])
        probs = softmax(qb, s)
        if pending is not None:
            finish(*pending)
        pending = (qb, *probs)
    finish(*pending)


def _fox(qt, kt, vt, c2, kf, *, batch, seq):
    n_tok = batch * seq
    pps = PAIRS_PER_STEP
    feat = lambda: pl.BlockSpec((1, pps * PAIR, seq), lambda b, p: (b, p, 0))
    return pl.pallas_call(
        functools.partial(_fox_kernel, seq=seq),
        grid=(batch, N_PAIRS // pps),
        in_specs=[
            feat(), feat(), feat(),
            pl.BlockSpec((1, pps, 2, seq), lambda b, p: (b, p, 0, 0)),
            pl.BlockSpec((1, seq, PAIR), lambda b, p: (b, 0, 0)),
        ],
        out_specs=pl.BlockSpec((seq, pps * PAIR), lambda b, p: (b, p)),
        out_shape=jax.ShapeDtypeStruct((n_tok, WIDTH), BF16),
        scratch_shapes=[pltpu.VMEM((pps, PAIR, seq), BF16), pltpu.VMEM((pps, PAIR, seq), BF16),
                        pltpu.VMEM((pps, seq, 2 * PAIR), BF16)],
        compiler_params=_params(),
        name="fox",
    )(qt, kt, vt, c2, kf)


def _bias_kernel(rb_ref, tabm_ref, tabr_ref):
    rb = rb_ref[...]
    n_rows = rb.shape[1]
    r = lax.broadcasted_iota(jnp.int32, (n_rows, TOEPLITZ), 0)
    m = lax.broadcasted_iota(jnp.int32, (n_rows, TOEPLITZ), 1)
    e = jnp.where(m < TOEPLITZ // 2, m, m - TOEPLITZ)
    idx = jnp.clip(LEFT_REACH - Q_TILE - e, -REL_CLIP, REL_CLIP) + REL_CLIP
    onehot = jnp.where(r == idx, 1.0, 0.0).astype(BF16)
    hi = rb.astype(BF16)
    rest = rb - hi.astype(F32)
    mid = rest.astype(BF16)
    lo = (rest - mid.astype(F32)).astype(BF16)
    u = _dot(hi, onehot) + _dot(mid, onehot) + _dot(lo, onehot)

    i = lax.broadcasted_iota(jnp.int32, (Q_TILE, BAND_KEYS), 0)
    j = lax.broadcasted_iota(jnp.int32, (Q_TILE, BAND_KEYS), 1)
    qc = jnp.right_shift(i, 6)
    kc = jnp.right_shift(j, 6)
    valid = (kc >= qc) & (kc <= qc + LEFT_CHUNKS)
    n_new = tabr_ref.shape[1]
    for h in range(N_HEADS):
        base = jnp.broadcast_to(u[h:h + 1, :], (Q_TILE, TOEPLITZ))
        t = pltpu.roll(base, Q_TILE, axis=1, stride=1, stride_axis=0)[:, 0:BAND_KEYS]
        half = (h % 2) * Q_TILE
        tabm_ref[h // 2, :, half:half + Q_TILE] = jnp.where(valid, t * LOG2E, MASKED).T
        tabr_ref[h] = t[0:n_new, :]


def _bias_tables(rb_pad, *, n_new):
    return pl.pallas_call(
        _bias_kernel,
        out_shape=[
            jax.ShapeDtypeStruct((N_PAIRS, BAND_KEYS, 2 * Q_TILE), F32),
            jax.ShapeDtypeStruct((N_HEADS, n_new, BAND_KEYS), F32),
        ],
        compiler_params=_params(),
        name="bias_tables",
    )(rb_pad)


def _band_kernel(qt_ref, kt_ref, vt_ref, tab_ref, o_ref, v0_ref, v1_ref, k_ref, *, seq):
    for pp in range(PAIRS_PER_STEP):
        feats = slice(pp * PAIR, (pp + 1) * PAIR)
        _band_pair(qt_ref.at[:, feats, :], kt_ref.at[:, feats, :], vt_ref.at[:, feats, :],
                   tab_ref.at[pp:pp + 1], o_ref.at[:, feats],
                   v0_ref.at[pp], v1_ref.at[pp], k_ref.at[pp], seq=seq)


def _band_pair(qt_ref, kt_ref, vt_ref, tab_ref, o_ref, v0_ref, v1_ref, k_ref, *, seq):
    vt = vt_ref[0]
    v0_ref[...] = _augment_v(vt, 0, 0)
    v1_ref[...] = _augment_v(vt, 1, 0)
    vaug = (v0_ref, v1_ref)
    k_ref[...] = kt_ref[0].astype(F32).T.astype(BF16)

    n_qb = seq // Q_TILE

    def scores(qb):
        qs, ke = qb * Q_TILE, (qb + 1) * Q_TILE
        ks = max(0, qs - LEFT_REACH)
        return _dot(k_ref[ks:ke, :], _pair_queries(qt_ref[0, :, qs:ke]))

    def softmax(qb, s):
        n_keys = s.shape[0]
        t = s + tab_ref[0, BAND_KEYS - n_keys:BAND_KEYS, :]
        m = jnp.max(t, axis=0, keepdims=True)
        return jnp.exp2(t - m).astype(BF16)

    def finish(qb, p):
        ke = (qb + 1) * Q_TILE
        acc0, acc1 = _weighted_values(vaug, [p], [(ke - p.shape[0], ke)])
        o_ref[qb * Q_TILE:ke, :] = _merge_heads_out_t(acc0, acc1).astype(o_ref.dtype)

    order = list(range(n_qb - 1, -1, -1))
    s_next = scores(order[0])
    pending = None
    for i, qb in enumerate(order):
        s = s_next
        if i + 1 < n_qb:
            s_next = scores(order[i + 1])
        p = softmax(qb, s)
        if pending is not None:
            finish(*pending)
        pending = (qb, p)
    finish(*pending)


def _band(qt, kt, vt, tabm, *, batch, seq):
    n_tok = batch * seq
    pps = PAIRS_PER_STEP
    feat = lambda: pl.BlockSpec((1, pps * PAIR, seq), lambda p, b: (b, p, 0))
    return pl.pallas_call(
        functools.partial(_band_kernel, seq=seq),
        grid=(N_PAIRS // pps, batch),
        in_specs=[
            feat(), feat(), feat(),
            pl.BlockSpec((pps, BAND_KEYS, 2 * Q_TILE), lambda p, b: (p, 0, 0)),
        ],
        out_specs=pl.BlockSpec((seq, pps * PAIR), lambda p, b: (b, p)),
        out_shape=jax.ShapeDtypeStruct((n_tok, WIDTH), BF16),
        scratch_shapes=[pltpu.VMEM((pps, PAIR, seq), BF16), pltpu.VMEM((pps, PAIR, seq), BF16),
                        pltpu.VMEM((pps, seq, PAIR), BF16)],
        compiler_params=_params(),
        name="band",
    )(qt, kt, vt, tabm)


def _two_part_attention(q2, kt_cache, vt_cache, kn_ref, vn_ref, bias_cache, bias_new, n_q):
    s_c = _dot(q2, kt_cache)
    s_n = _dot_nt(q2, kn_ref[...])
    accs = []
    for h in range(2):
        rows = slice(h * n_q, (h + 1) * n_q)
        t_c = s_c[rows, :] + bias_cache(h)
        t_n = s_n[rows, :] + bias_new(h)
        m = jnp.maximum(jnp.max(t_c, axis=1, keepdims=True), jnp.max(t_n, axis=1, keepdims=True))
        acc = _dot_nt(jnp.exp(t_c - m).astype(BF16), _augment_v(vt_cache, h, 0))
        acc = acc + _dot(jnp.exp(t_n - m).astype(BF16), _augment_v(vn_ref[...], h, 1))
        accs.append(acc)
    return _merge_heads_out(accs[0], accs[1])


def _sample_attn_kernel(qa_ref, ka_ref, va_ref, qb_ref, kb_ref, vb_ref,
                        cak_ref, cav_ref, cbk_ref, cbv_ref, c_ref, tab_ref,
                        oa_ref, ob_ref, kn_ref, vn_ref, *, n_q):
    row = lax.broadcasted_iota(jnp.int32, (n_q, NEW_PAD), 0)
    col = lax.broadcasted_iota(jnp.int32, (n_q, NEW_PAD), 1)
    n_cache = cak_ref.shape[2]
    n_band = cbk_ref.shape[2]

    def stage_new(slot, k_new, v_new):
        kn_ref[slot] = jnp.zeros((NEW_PAD, PAIR), BF16)
        vn_ref[slot] = jnp.zeros((NEW_PAD, PAIR), BF16)
        kn_ref[slot, 0:n_q, :] = k_new
        vn_ref[slot, 0:n_q, :] = v_new
        return kn_ref.at[slot], vn_ref.at[slot]

    for p in range(N_PAIRS):
        feats = slice(p * PAIR, (p + 1) * PAIR)

        c_cache = c_ref[0, p, :, 0:n_cache]
        c_new = c_ref[0, p, :, n_cache:n_cache + NEW_PAD]

        def fox_cache(h, c_cache=c_cache, c_new=c_new):
            cq = jnp.sum(jnp.where(col == row, c_new[h:h + 1, :], 0.0), axis=1, keepdims=True)
            return cq - c_cache[h:h + 1, :]

        def fox_new(h, c_new=c_new):
            cq = jnp.sum(jnp.where(col == row, c_new[h:h + 1, :], 0.0), axis=1, keepdims=True)
            return jnp.where(col <= row, cq - c_new[h:h + 1, :], -jnp.inf)

        kn, vn = stage_new(2 * p, ka_ref[:, feats], va_ref[:, feats])
        oa = _two_part_attention(_split_heads_q(qa_ref[:, feats]), cak_ref[0, feats, :].astype(BF16),
                                 cav_ref[0, feats, :].astype(BF16), kn, vn, fox_cache, fox_new, n_q)
        oa_ref[:, feats] = oa.astype(oa_ref.dtype)

        def band_cache(h, p=p):
            return tab_ref[2 * p + h, :, 0:n_band]

        def band_new(h, p=p):
            return jnp.where(col < n_q, tab_ref[2 * p + h, :, n_band:n_band + NEW_PAD], -jnp.inf)

        kn, vn = stage_new(2 * p + 1, kb_ref[:, feats], vb_ref[:, feats])
        ob = _two_part_attention(_split_heads_q(qb_ref[:, feats]), cbk_ref[0, feats, :].astype(BF16),
                                 cbv_ref[0, feats, :].astype(BF16), kn, vn, band_cache, band_new, n_q)
        ob_ref[:, feats] = ob.astype(ob_ref.dtype)


def _rows_cumsum_kernel(x_ref, c_ref):
    c_ref[...] = _lane_cumsum(x_ref[...])


def _rows_cumsum(x):
    return pl.pallas_call(
        _rows_cumsum_kernel,
        out_shape=jax.ShapeDtypeStruct(x.shape, F32),
        compiler_params=_params(),
        name="rows_cumsum",
    )(x)


def _sample_attn(qa, ka, va, qb, kb, vb, cakt, cavt, cbkt, cbvt, c_all, tabr, *, batch, n_q):
    n_tok = batch * n_q
    past = cakt.shape[2]
    n_band = cbkt.shape[2]
    new = lambda: pl.BlockSpec((n_q, WIDTH), lambda b: (b, 0))
    cache = lambda n: pl.BlockSpec((1, WIDTH, n), lambda b: (b, 0, 0))
    return pl.pallas_call(
        functools.partial(_sample_attn_kernel, n_q=n_q),
        grid=(batch,),
        in_specs=[
            new(), new(), new(), new(), new(), new(),
            cache(past), cache(past), cache(n_band), cache(n_band),
            pl.BlockSpec((1, N_PAIRS, 2, past + NEW_PAD), lambda b: (b, 0, 0, 0)),
            _resident(tabr.shape),
        ],
        out_specs=[new(), new()],
        out_shape=[jax.ShapeDtypeStruct((n_tok, WIDTH), BF16)] * 2,
        scratch_shapes=[pltpu.VMEM((2 * N_PAIRS, NEW_PAD, PAIR), BF16),
                        pltpu.VMEM((2 * N_PAIRS, NEW_PAD, PAIR), BF16)],
        compiler_params=_params(),
        name="sample_attn",
    )(qa, ka, va, qb, kb, vb, cakt, cavt, cbkt, cbvt, c_all, tabr)


def _post_kernel(x_ref, oa_ref, ob_ref, ga_ref, gb_ref, wo_ref, gpost_ref, gpre_ref,
                 wg_ref, wu_ref, wd_ref, gffn_ref, y_ref, *, n_sub):
    sub = x_ref.shape[0] // n_sub
    rows = [slice(i * sub, (i + 1) * sub) for i in range(n_sub)]
    st = [dict() for _ in range(n_sub)]

    def norm(i):
        st[i]["na"] = _rms(oa_ref[rows[i], :].astype(F32), ga_ref[...]).astype(BF16)
        st[i]["nb"] = _rms(ob_ref[rows[i], :].astype(F32), gb_ref[...]).astype(BF16)

    def mix(i):
        st[i]["mix"] = _dot(st[i]["na"], wo_ref[0:WIDTH, :]) + _dot(st[i]["nb"], wo_ref[WIDTH:2 * WIDTH, :])

    def mid(i):
        x1 = x_ref[rows[i], :] + _rms(st[i]["mix"], gpost_ref[...])
        st[i]["x1"] = x1
        st[i]["hf"] = _rms(x1, gpre_ref[...]).astype(BF16)

    def gate_up(i):
        st[i]["g"] = _dot(st[i]["hf"], wg_ref[...])
        st[i]["u"] = _dot(st[i]["hf"], wu_ref[...])

    def silu(i):
        g = st[i]["g"]
        st[i]["a"] = (g * jax.nn.sigmoid(g) * st[i]["u"]).astype(BF16)

    def down(i):
        st[i]["ff"] = _dot(st[i]["a"], wd_ref[...])

    def fin(i):
        y_ref[rows[i], :] = st[i]["x1"] + _rms(st[i]["ff"], gffn_ref[...])

    vpu_stages = [norm, mid, silu, fin]
    mxu_stages = [mix, gate_up, down]
    norm(0)
    for k, mxu in enumerate(mxu_stages):
        for i in range(n_sub):
            mxu(i)
            if i + 1 < n_sub:
                vpu_stages[k](i + 1)
            else:
                vpu_stages[k + 1](0)
    for i in range(1, n_sub):
        fin(i)


def _post(x, oa, ob, ga, gb, wo, gpost, gpre, wg, wu, wd, gffn, *, tm, n_sub):
    n_tok, d_model = x.shape
    row = lambda w: pl.BlockSpec((tm, w), lambda i: (i, 0))
    return pl.pallas_call(
        functools.partial(_post_kernel, n_sub=n_sub),
        grid=(n_tok // tm,),
        in_specs=[
            row(d_model), row(WIDTH), row(WIDTH),
            _resident(ga.shape), _resident(gb.shape), _resident(wo.shape),
            _resident(gpost.shape), _resident(gpre.shape),
            _resident(wg.shape), _resident(wu.shape), _resident(wd.shape), _resident(gffn.shape),
        ],
        out_specs=row(d_model),
        out_shape=jax.ShapeDtypeStruct((n_tok, d_model), F32),
        compiler_params=_params(),
        name="post",
    )(x, oa, ob, ga, gb, wo, gpost, gpre, wg, wu, wd, gffn)


def _row(v):
    return v.reshape(1, -1).astype(F32)


def _state_from_feature_major(t, batch, n):
    return t.reshape(batch, N_HEADS, HEAD_DIM, n).transpose(0, 3, 1, 2)


def _cache_feature_major(c):
    b, n = c.shape[0], c.shape[1]
    return c.transpose(0, 2, 3, 1).reshape(b, WIDTH, n)


def kernel(x_prompt, x_sample, cache_a_k, cache_a_v, cache_a_logf, cache_b_k, cache_b_v, norm_mix_pre, w_in, b_forget, rel_bias, gain_out_a, gain_out_b, w_out, norm_mix_post, norm_ffn_pre, w_gate, w_up, w_down, norm_ffn_post):
    batch, seq, d_model = x_prompt.shape
    dec_batch, dec_seq, _ = x_sample.shape
    depth = w_in.shape[0]
    past = cache_a_k.shape[2]
    n_band = cache_b_k.shape[2]
    keep = min(LEFT_REACH, seq)
    assert seq % Q_TILE == 0 and keep == LEFT_REACH and n_band == LEFT_REACH and dec_seq <= CHUNK
    n_tok_p, n_tok_s = batch * seq, dec_batch * dec_seq
    tm_p = 2 * keep
    assert seq % tm_p == 0

    xp = x_prompt.reshape(n_tok_p, d_model)
    xs = x_sample.reshape(n_tok_s, d_model)
    outs = [[] for _ in range(10)]
    w_in_t = jnp.swapaxes(w_in, 1, 2)
    for l in range(depth):
        b_col = b_forget[l].reshape(N_HEADS, 1).astype(F32)
        rb_pad = jnp.pad(rel_bias[l].astype(F32), ((0, 16 - N_HEADS), (0, 384 - (2 * REL_CLIP + 1))))
        tail = (_row(gain_out_a[l]), _row(gain_out_b[l]), w_out[l].astype(BF16), _row(norm_mix_post[l]),
                _row(norm_ffn_pre[l]), w_gate[l].astype(BF16), w_up[l].astype(BF16), w_down[l].astype(BF16),
                _row(norm_ffn_post[l]))
        tabm, tabr = _bias_tables(rb_pad, n_new=dec_seq)

        qat, qbt, kat, vat, kbt, vbt, kat32, vat32, logf_p, kbt32, vbt32 = _inproj_t(
            xp, _row(norm_mix_pre[l]), w_in_t, l, b_col, batch=batch, seq=seq, tm=tm_p, keep_rows=keep)
        c2, kf = _cumsum(logf_p.reshape(batch * N_HEADS, seq), batch=batch, seq=seq)
        oa = _fox(qat, kat, vat, c2.reshape(batch, N_PAIRS, 2, seq), kf, batch=batch, seq=seq)
        ob = _band(qbt, kbt, vbt, tabm, batch=batch, seq=seq)
        xp = _post(xp, oa, ob, *tail, tm=1024, n_sub=4)
        outs[0].append(_state_from_feature_major(kat32, batch, seq))
        outs[1].append(_state_from_feature_major(vat32, batch, seq))
        outs[2].append(logf_p.transpose(0, 2, 1))
        outs[3].append(_state_from_feature_major(kbt32, batch, keep))
        outs[4].append(_state_from_feature_major(vbt32, batch, keep))

        qa, ka, va, qb, kb, vb, ka32, va32, logf_t, kb32, vb32 = _inproj(xs, _row(norm_mix_pre[l]), w_in_t, l, b_col)
        lf_all = jnp.concatenate(
            [cache_a_logf[l].astype(F32).transpose(0, 2, 1),
             logf_t.reshape(N_HEADS, dec_batch, dec_seq).transpose(1, 0, 2),
             jnp.zeros((dec_batch, N_HEADS, NEW_PAD - dec_seq), F32)], axis=2)
        c_all = _rows_cumsum(lf_all.reshape(dec_batch * N_HEADS, past + NEW_PAD))
        oa, ob = _sample_attn(
            qa, ka, va, qb, kb, vb,
            _cache_feature_major(cache_a_k[l]), _cache_feature_major(cache_a_v[l]),
            _cache_feature_major(cache_b_k[l]), _cache_feature_major(cache_b_v[l]),
            c_all.reshape(dec_batch, N_PAIRS, 2, past + NEW_PAD), tabr, batch=dec_batch, n_q=dec_seq)
        xs = _post(xs, oa, ob, *tail, tm=n_tok_s, n_sub=1)
        outs[5].append(ka32.reshape(dec_batch, dec_seq, N_HEADS, HEAD_DIM))
        outs[6].append(va32.reshape(dec_batch, dec_seq, N_HEADS, HEAD_DIM))
        outs[7].append(logf_t.T.reshape(dec_batch, dec_seq, N_HEADS))
        outs[8].append(kb32.reshape(dec_batch, dec_seq, N_HEADS, HEAD_DIM))
        outs[9].append(vb32.reshape(dec_batch, dec_seq, N_HEADS, HEAD_DIM))

    st = jnp.stack
    return (xp.reshape(batch, seq, d_model), xs.reshape(dec_batch, dec_seq, d_model),
            *[st(o) for o in outs])
```

```python
import functools

import jax
import jax.numpy as jnp
from jax import lax
from jax.experimental import pallas as pl
from jax.experimental.pallas import tpu as pltpu

F32 = jnp.float32
BF16 = jnp.bfloat16

HEAD_DIM = 64
N_HEADS = 8
WIDTH = N_HEADS * HEAD_DIM
PAIR = 2 * HEAD_DIM
N_PAIRS = N_HEADS // 2
CHUNK = 64
LEFT_CHUNKS = 8
LEFT_REACH = LEFT_CHUNKS * CHUNK
REL_CLIP = 128
EPS = 1e-6
SCALE = HEAD_DIM ** -0.5
LOG2E = 1.4426950408889634
Q_TILE = 256
PAIRS_PER_STEP = 4
BAND_KEYS = LEFT_REACH + Q_TILE
TOEPLITZ = 1024
RB_ROWS, RB_COLS = 2 * N_HEADS, 384
HEAD_SHIFT = 6
MASKED = -1e30
NEW_PAD = 128
VMEM_LIMIT = 56 * 1024 * 1024
W_IN_COLS = {"qa": 0, "ka": WIDTH, "va": 2 * WIDTH, "forget": 3 * WIDTH,
             "qb": 3 * WIDTH + N_HEADS, "kb": 4 * WIDTH + N_HEADS, "vb": 5 * WIDTH + N_HEADS}


def _params(n_grid_axes=0):
    sem = ("arbitrary",) * n_grid_axes if n_grid_axes else None
    return pltpu.CompilerParams(vmem_limit_bytes=VMEM_LIMIT, dimension_semantics=sem)


def _resident(shape):
    return pl.BlockSpec(shape, lambda *_: (0,) * len(shape), pipeline_mode=pl.Buffered(1))


def _rms(x, g):
    ms = jnp.mean(x * x, axis=-1, keepdims=True)
    return x * lax.rsqrt(ms + EPS) * g


def _log_sigmoid(x):
    return jnp.minimum(x, 0.0) - jnp.log1p(jnp.exp(-jnp.abs(x)))


def _dot(a, b):
    return jnp.dot(a, b, preferred_element_type=F32)


def _dot_nt(a, b):
    return lax.dot_general(a, b, (((1,), (1,)), ((), ())), preferred_element_type=F32)


def _inproj_kernel(x_ref, g_ref, w_ref, bf_ref,
                   qa_ref, ka_ref, va_ref, qb_ref, kb_ref, vb_ref,
                   ka32_ref, va32_ref, logf_ref, kb32_ref, vb32_ref):
    h = _rms(x_ref[...], g_ref[...]).astype(BF16)

    def proj(name):
        r0 = W_IN_COLS[name]
        return _dot_nt(h, w_ref[0, r0:r0 + WIDTH, :].astype(BF16))

    qa_ref[...] = (proj("qa") * SCALE).astype(BF16)
    z = proj("ka")
    ka32_ref[...] = z
    ka_ref[...] = z.astype(BF16)
    z = proj("va")
    va32_ref[...] = z
    va_ref[...] = z.astype(BF16)
    qb_ref[...] = (proj("qb") * SCALE).astype(BF16)
    z = proj("kb")
    kb32_ref[...] = z
    kb_ref[...] = z.astype(BF16)
    z = proj("vb")
    vb32_ref[...] = z
    vb_ref[...] = z.astype(BF16)
    r0 = W_IN_COLS["forget"]
    ft = _dot_nt(w_ref[0, r0:r0 + 2 * N_HEADS, :].astype(BF16), h)
    logf_ref[...] = _log_sigmoid(ft[0:N_HEADS, :] + bf_ref[...])


def _inproj(x, gain, w_in, layer, b_col):
    n_tok, d_model = x.shape
    full = lambda w: pl.BlockSpec((n_tok, w), lambda i: (0, 0))
    w_spec = pl.BlockSpec((1,) + w_in.shape[1:], lambda i: (layer, 0, 0), pipeline_mode=pl.Buffered(1))
    out_shape = (
        [jax.ShapeDtypeStruct((n_tok, WIDTH), BF16)] * 6
        + [jax.ShapeDtypeStruct((n_tok, WIDTH), F32)] * 2
        + [jax.ShapeDtypeStruct((N_HEADS, n_tok), F32)]
        + [jax.ShapeDtypeStruct((n_tok, WIDTH), F32)] * 2
    )
    out_specs = ([full(WIDTH) for _ in range(8)]
                 + [pl.BlockSpec((N_HEADS, n_tok), lambda i: (0, 0)), full(WIDTH), full(WIDTH)])
    return pl.pallas_call(
        _inproj_kernel,
        grid=(1,),
        in_specs=[full(d_model), _resident((1, d_model)), w_spec, _resident((N_HEADS, 1))],
        out_specs=out_specs,
        out_shape=out_shape,
        compiler_params=_params(),
        name="inproj_sample",
    )(x, gain, w_in, b_col)


def _inproj_t_kernel(x_ref, g_ref, w_ref, bf_ref,
                     qat_ref, qbt_ref, kat_ref, vat_ref, kbt_ref, vbt_ref,
                     kat32_ref, vat32_ref, logf_ref, kbt32_ref, vbt32_ref, wt_ref, *, tiles_per_seq):
    @pl.when(pl.program_id(0) == 0)
    def _():
        for g, name in enumerate(("ka", "va", "kb", "vb", "qa", "qb")):
            r0 = W_IN_COLS[name]
            wt_ref[g * WIDTH:(g + 1) * WIDTH, :] = w_ref[0, r0:r0 + WIDTH, :].astype(BF16)
        r0 = W_IN_COLS["forget"]
        ft = w_ref[0, r0:r0 + 2 * N_HEADS, :]
        live = lax.broadcasted_iota(jnp.int32, ft.shape, 0) < N_HEADS
        wt_ref[6 * WIDTH:6 * WIDTH + 2 * N_HEADS, :] = jnp.where(live, ft, 0.0).astype(BF16)

    sub = Q_TILE
    n_sub = x_ref.shape[0] // sub
    keep_from = x_ref.shape[0] - kbt32_ref.shape[2]
    hs, kept = {}, []

    def norm(i):
        hs[i] = _rms(x_ref[i * sub:(i + 1) * sub, :], g_ref[...]).astype(BF16)

    def project(i, after_first=None):
        cols = slice(i * sub, (i + 1) * sub)

        def proj_t(c, rows=WIDTH):
            return _dot_nt(wt_ref[c * WIDTH:c * WIDTH + rows, :], hs[i])

        z = proj_t(0)
        kat32_ref[0, :, cols] = z
        kat_ref[0, :, cols] = z.astype(BF16)
        if after_first is not None:
            after_first()
        z = proj_t(1)
        vat32_ref[0, :, cols] = z
        vat_ref[0, :, cols] = z.astype(BF16)
        zk = proj_t(2)
        kbt_ref[0, :, cols] = zk.astype(BF16)
        zv = proj_t(3)
        vbt_ref[0, :, cols] = zv.astype(BF16)
        qat_ref[0, :, cols] = (proj_t(4) * (SCALE * LOG2E)).astype(BF16)
        zq = proj_t(5, WIDTH + 2 * N_HEADS)
        qbt_ref[0, :, cols] = (zq[0:WIDTH] * (SCALE * LOG2E)).astype(BF16)
        logf_ref[0, :, cols] = _log_sigmoid(zq[WIDTH:WIDTH + N_HEADS] + bf_ref[...])
        if i * sub >= keep_from:
            kept.append((slice(i * sub - keep_from, (i + 1) * sub - keep_from), zk, zv))

    norm(0)
    for i in range(n_sub):
        project(i, (lambda j=i + 1: norm(j)) if i + 1 < n_sub else None)

    @pl.when(pl.program_id(0) % tiles_per_seq == tiles_per_seq - 1)
    def _():
        for cols, zk, zv in kept:
            kbt32_ref[0, :, cols] = zk
            vbt32_ref[0, :, cols] = zv


def _inproj_t(x, gain, w_in, layer, b_col, *, batch, seq, tm, keep_rows):
    n_tok, d_model = x.shape
    tps = seq // tm
    w_spec = pl.BlockSpec((1,) + w_in.shape[1:], lambda i: (layer, 0, 0), pipeline_mode=pl.Buffered(1))
    feat = lambda: pl.BlockSpec((1, WIDTH, tm), lambda i: (i // tps, 0, i % tps))
    keep = lambda: pl.BlockSpec((1, WIDTH, keep_rows), lambda i: (i // tps, 0, 0))
    out_shape = (
        [jax.ShapeDtypeStruct((batch, WIDTH, seq), BF16)] * 6
        + [jax.ShapeDtypeStruct((batch, WIDTH, seq), F32)] * 2
        + [jax.ShapeDtypeStruct((batch, N_HEADS, seq), F32)]
        + [jax.ShapeDtypeStruct((batch, WIDTH, keep_rows), F32)] * 2
    )
    out_specs = ([feat() for _ in range(8)]
                 + [pl.BlockSpec((1, N_HEADS, tm), lambda i: (i // tps, 0, i % tps)), keep(), keep()])
    return pl.pallas_call(
        functools.partial(_inproj_t_kernel, tiles_per_seq=tps),
        grid=(n_tok // tm,),
        in_specs=[
            pl.BlockSpec((tm, d_model), lambda i: (i, 0)),
            _resident((1, d_model)),
            w_spec,
            _resident((N_HEADS, 1)),
        ],
        out_specs=out_specs,
        out_shape=out_shape,
        scratch_shapes=[pltpu.VMEM((6 * WIDTH + 2 * N_HEADS, d_model), BF16)],
        compiler_params=_params(1),
        name="inproj",
    )(x, gain, w_in, b_col)


def _lane_cumsum(x):
    n = x.shape[-1]
    lane = lax.broadcasted_iota(jnp.int32, x.shape, x.ndim - 1)
    shift = 1
    while shift < n:
        x = x + jnp.where(lane >= shift, pltpu.roll(x, shift, axis=x.ndim - 1), 0.0)
        shift *= 2
    return x


def _split3(x):
    hi = x.astype(BF16).astype(F32)
    rest = x - hi
    mid = rest.astype(BF16).astype(F32)
    return hi, mid, rest - mid


def _cumsum_kernel(lf_ref, c2_ref, kf_ref):
    c2 = _lane_cumsum(lf_ref[...]) * LOG2E
    c2_ref[...] = c2
    seq = c2.shape[1]
    hi, mid, lo = _split3(c2)
    pad = jnp.zeros((PAIR - 3 * N_HEADS, seq), F32)
    for b in range(kf_ref.shape[0]):
        rows = slice(b * N_HEADS, (b + 1) * N_HEADS)
        feat = jnp.concatenate([hi[rows], mid[rows], lo[rows], pad], axis=0)
        kf_ref[b] = feat.T.astype(BF16)


def _cumsum(logf, *, batch, seq):
    return pl.pallas_call(
        _cumsum_kernel,
        out_shape=[
            jax.ShapeDtypeStruct((batch * N_HEADS, seq), F32),
            jax.ShapeDtypeStruct((batch, seq, PAIR), BF16),
        ],
        compiler_params=_params(),
        name="cumsum",
    )(logf)


def _split_heads_q(q):
    lane = lax.broadcasted_iota(jnp.int32, q.shape, 1)
    zero = jnp.zeros_like(q)
    return jnp.concatenate([jnp.where(lane < HEAD_DIM, q, zero),
                            jnp.where(lane >= HEAD_DIM, q, zero)], axis=0)


def _augment_v(v, h, axis):
    idx = lax.broadcasted_iota(jnp.int32, v.shape, axis)
    own = (idx < HEAD_DIM) if h == 0 else (idx >= HEAD_DIM)
    return jnp.where(own, v, jnp.ones_like(v))


def _merge_heads_out(acc0, acc1):
    lane = lax.broadcasted_iota(jnp.int32, acc0.shape, 1)
    first = lane < HEAD_DIM
    num = jnp.where(first, acc0, acc1)
    den = pltpu.roll(jnp.where(first, acc1, acc0), HEAD_DIM, axis=1)
    return num / den


def _pair_queries(qt):
    n = qt.shape[1]
    q2 = jnp.concatenate([qt, qt], axis=1)
    r = lax.broadcasted_iota(jnp.int32, q2.shape, 0)
    c = lax.broadcasted_iota(jnp.int32, q2.shape, 1)
    own = jnp.right_shift(r, HEAD_SHIFT) == jnp.where(c >= n, 1, 0)
    return jnp.where(own, q2, jnp.zeros_like(q2))


def _merge_heads_out_t(acc0, acc1):
    first = lax.broadcasted_iota(jnp.int32, acc0.shape, 0) < HEAD_DIM
    num = jnp.where(first, acc0, acc1)
    den = jnp.where(first, acc0[HEAD_DIM:HEAD_DIM + 1, :], acc1[0:1, :])
    return (num / den).T


def _weighted_values(vaug, p_blocks, key_ranges):
    accs = []
    for h in range(2):
        cols = slice(h * Q_TILE, (h + 1) * Q_TILE)
        acc = None
        for p, (k0, k1) in zip(p_blocks, key_ranges):
            part = _dot(vaug[h][:, k0:k1], p[:, cols])
            acc = part if acc is None else acc + part
        accs.append(acc)
    return accs


def _fox_kernel(qt_ref, kt_ref, vt_ref, c2_ref, kf_ref, o_ref, v0_ref, v1_ref, kx_ref, *, seq):
    for pp in range(PAIRS_PER_STEP):
        feats = slice(pp * PAIR, (pp + 1) * PAIR)
        _fox_pair(PAIRS_PER_STEP * pl.program_id(1) + pp,
                  qt_ref.at[:, feats, :], kt_ref.at[:, feats, :], vt_ref.at[:, feats, :],
                  c2_ref.at[:, pp:pp + 1], kf_ref, o_ref.at[:, feats],
                  v0_ref.at[pp], v1_ref.at[pp], kx_ref.at[pp], seq=seq)


def _fox_pair(pair, qt_ref, kt_ref, vt_ref, c2_ref, kf_ref, o_ref, v0_ref, v1_ref, kx_ref, *, seq):
    vt = vt_ref[0]
    v0_ref[...] = _augment_v(vt, 0, 0)
    v1_ref[...] = _augment_v(vt, 1, 0)
    vaug = (v0_ref, v1_ref)

    kx_ref[:, 0:PAIR] = kt_ref[0].astype(F32).T.astype(BF16)
    kx_ref[:, PAIR:2 * PAIR] = kf_ref[0]
    er = lax.broadcasted_iota(jnp.int32, (PAIR, 2 * Q_TILE), 0)
    ec = lax.broadcasted_iota(jnp.int32, (PAIR, 2 * Q_TILE), 1)
    head = 2 * pair + jnp.where(ec >= Q_TILE, 1, 0)
    piece = er - head
    coef = jnp.where((piece == 0) | (piece == N_HEADS) | (piece == 2 * N_HEADS), -1.0, 0.0).astype(BF16)
    key = lax.broadcasted_iota(jnp.int32, (Q_TILE, 2 * Q_TILE), 0)
    qry = jnp.bitwise_and(lax.broadcasted_iota(jnp.int32, (Q_TILE, 2 * Q_TILE), 1), Q_TILE - 1)
    causal = key <= qry

    n_qb = seq // Q_TILE

    def scores(qb):
        qs, ke = qb * Q_TILE, (qb + 1) * Q_TILE
        rhs = jnp.concatenate([_pair_queries(qt_ref[0, :, qs:ke]), coef], axis=0)
        return _dot(kx_ref[0:ke, :], rhs)

    def softmax(qb, s):
        qs, ke = qb * Q_TILE, (qb + 1) * Q_TILE
        cq = jnp.concatenate([c2_ref[0, 0, 0:1, qs:ke], c2_ref[0, 0, 1:2, qs:ke]], axis=1)
        t_diag = jnp.where(causal, s[qs:ke, :], -jnp.inf)
        m = jnp.max(t_diag, axis=0, keepdims=True)
        if qb > 0:
            t_past = s[0:qs, :]
            m = jnp.maximum(m, jnp.max(t_past, axis=0, keepdims=True))
        shift = cq - (m + cq)
        p_blocks = [jnp.exp2(t_diag + shift).astype(BF16)]
        key_ranges = [(qs, ke)]
        if qb > 0:
            p_blocks.append(jnp.exp2(t_past + shift).astype(BF16))
            key_ranges.append((0, qs))
        return p_blocks, key_ranges

    def finish(qb, p_blocks, key_ranges):
        acc0, acc1 = _weighted_values(vaug, p_blocks, key_ranges)
        o_ref[qb * Q_TILE:(qb + 1) * Q_TILE, :] = _merge_heads_out_t(acc0, acc1).astype(o_ref.dtype)

    order = list(range(n_qb - 1, -1, -1))
    s_next = scores(order[0])
    pending = None
    for i, qb in enumerate(order):
        s = s_next
        if i + 1 < n_qb:
            s_next = scores(order[i + 1])
        probs = softmax(qb, s)
        if pending is not None:
            finish(*pending)
        pending = (qb, *probs)
    finish(*pending)


def _fox(qt, kt, vt, c2, kf, *, batch, seq):
    n_tok = batch * seq
    pps = PAIRS_PER_STEP
    feat = lambda: pl.BlockSpec((1, pps * PAIR, seq), lambda b, p: (b, p, 0))
    return pl.pallas_call(
        functools.partial(_fox_kernel, seq=seq),
        grid=(batch, N_PAIRS // pps),
        in_specs=[
            feat(), feat(), feat(),
            pl.BlockSpec((1, pps, 2, seq), lambda b, p: (b, p, 0, 0)),
            pl.BlockSpec((1, seq, PAIR), lambda b, p: (b, 0, 0)),
        ],
        out_specs=pl.BlockSpec((seq, pps * PAIR), lambda b, p: (b, p)),
        out_shape=jax.ShapeDtypeStruct((n_tok, WIDTH), BF16),
        scratch_shapes=[pltpu.VMEM((pps, PAIR, seq), BF16), pltpu.VMEM((pps, PAIR, seq), BF16),
                        pltpu.VMEM((pps, seq, 2 * PAIR), BF16)],
        compiler_params=_params(),
        name="fox",
    )(qt, kt, vt, c2, kf)


def _bias_kernel(rb_ref, tabm_ref, tabr_ref):
    rb = rb_ref[...]
    n_rows = rb.shape[1]
    r = lax.broadcasted_iota(jnp.int32, (n_rows, TOEPLITZ), 0)
    m = lax.broadcasted_iota(jnp.int32, (n_rows, TOEPLITZ), 1)
    e = jnp.where(m < TOEPLITZ // 2, m, m - TOEPLITZ)
    idx = jnp.clip(LEFT_REACH - Q_TILE - e, -REL_CLIP, REL_CLIP) + REL_CLIP
    onehot = jnp.where(r == idx, 1.0, 0.0).astype(BF16)
    hi = rb.astype(BF16)
    rest = rb - hi.astype(F32)
    mid = rest.astype(BF16)
    lo = (rest - mid.astype(F32)).astype(BF16)
    u = _dot(hi, onehot) + _dot(mid, onehot) + _dot(lo, onehot)

    i = lax.broadcasted_iota(jnp.int32, (Q_TILE, BAND_KEYS), 0)
    j = lax.broadcasted_iota(jnp.int32, (Q_TILE, BAND_KEYS), 1)
    qc = jnp.right_shift(i, HEAD_SHIFT)
    kc = jnp.right_shift(j, HEAD_SHIFT)
    valid = (kc >= qc) & (kc <= qc + LEFT_CHUNKS)
    n_new = tabr_ref.shape[1]
    for h in range(N_HEADS):
        base = jnp.broadcast_to(u[h:h + 1, :], (Q_TILE, TOEPLITZ))
        t = pltpu.roll(base, Q_TILE, axis=1, stride=1, stride_axis=0)[:, 0:BAND_KEYS]
        half = (h % 2) * Q_TILE
        tabm_ref[h // 2, :, half:half + Q_TILE] = jnp.where(valid, t * LOG2E, MASKED).T
        tabr_ref[h] = t[0:n_new, :]


def _bias_tables(rb_pad, *, n_new):
    return pl.pallas_call(
        _bias_kernel,
        out_shape=[
            jax.ShapeDtypeStruct((N_PAIRS, BAND_KEYS, 2 * Q_TILE), F32),
            jax.ShapeDtypeStruct((N_HEADS, n_new, BAND_KEYS), F32),
        ],
        compiler_params=_params(),
        name="bias_tables",
    )(rb_pad)


def _band_kernel(qt_ref, kt_ref, vt_ref, tab_ref, o_ref, v0_ref, v1_ref, k_ref, *, seq):
    for pp in range(PAIRS_PER_STEP):
        feats = slice(pp * PAIR, (pp + 1) * PAIR)
        _band_pair(qt_ref.at[:, feats, :], kt_ref.at[:, feats, :], vt_ref.at[:, feats, :],
                   tab_ref.at[pp:pp + 1], o_ref.at[:, feats],
                   v0_ref.at[pp], v1_ref.at[pp], k_ref.at[pp], seq=seq)


def _band_pair(qt_ref, kt_ref, vt_ref, tab_ref, o_ref, v0_ref, v1_ref, k_ref, *, seq):
    vt = vt_ref[0]
    v0_ref[...] = _augment_v(vt, 0, 0)
    v1_ref[...] = _augment_v(vt, 1, 0)
    vaug = (v0_ref, v1_ref)
    k_ref[...] = kt_ref[0].astype(F32).T.astype(BF16)

    n_qb = seq // Q_TILE

    def scores(qb):
        qs, ke = qb * Q_TILE, (qb + 1) * Q_TILE
        ks = max(0, qs - LEFT_REACH)
        return _dot(k_ref[ks:ke, :], _pair_queries(qt_ref[0, :, qs:ke]))

    def softmax(qb, s):
        n_keys = s.shape[0]
        t = s + tab_ref[0, BAND_KEYS - n_keys:BAND_KEYS, :]
        m = jnp.max(t, axis=0, keepdims=True)
        return jnp.exp2(t - m).astype(BF16)

    def finish(qb, p):
        ke = (qb + 1) * Q_TILE
        acc0, acc1 = _weighted_values(vaug, [p], [(ke - p.shape[0], ke)])
        o_ref[qb * Q_TILE:ke, :] = _merge_heads_out_t(acc0, acc1).astype(o_ref.dtype)

    order = list(range(n_qb - 1, -1, -1))
    s_next = scores(order[0])
    pending = None
    for i, qb in enumerate(order):
        s = s_next
        if i + 1 < n_qb:
            s_next = scores(order[i + 1])
        p = softmax(qb, s)
        if pending is not None:
            finish(*pending)
        pending = (qb, p)
    finish(*pending)


def _band(qt, kt, vt, tabm, *, batch, seq):
    n_tok = batch * seq
    pps = PAIRS_PER_STEP
    feat = lambda: pl.BlockSpec((1, pps * PAIR, seq), lambda p, b: (b, p, 0))
    return pl.pallas_call(
        functools.partial(_band_kernel, seq=seq),
        grid=(N_PAIRS // pps, batch),
        in_specs=[
            feat(), feat(), feat(),
            pl.BlockSpec((pps, BAND_KEYS, 2 * Q_TILE), lambda p, b: (p, 0, 0)),
        ],
        out_specs=pl.BlockSpec((seq, pps * PAIR), lambda p, b: (b, p)),
        out_shape=jax.ShapeDtypeStruct((n_tok, WIDTH), BF16),
        scratch_shapes=[pltpu.VMEM((pps, PAIR, seq), BF16), pltpu.VMEM((pps, PAIR, seq), BF16),
                        pltpu.VMEM((pps, seq, PAIR), BF16)],
        compiler_params=_params(),
        name="band",
    )(qt, kt, vt, tabm)


def _two_part_attention(q2, kt_cache, vt_cache, kn_ref, vn_ref, bias_cache, bias_new, n_q):
    s_c = _dot(q2, kt_cache)
    s_n = _dot_nt(q2, kn_ref[...])
    accs = []
    for h in range(2):
        rows = slice(h * n_q, (h + 1) * n_q)
        t_c = s_c[rows, :] + bias_cache(h)
        t_n = s_n[rows, :] + bias_new(h)
        m = jnp.maximum(jnp.max(t_c, axis=1, keepdims=True), jnp.max(t_n, axis=1, keepdims=True))
        acc = _dot_nt(jnp.exp(t_c - m).astype(BF16), _augment_v(vt_cache, h, 0))
        acc = acc + _dot(jnp.exp(t_n - m).astype(BF16), _augment_v(vn_ref[...], h, 1))
        accs.append(acc)
    return _merge_heads_out(accs[0], accs[1])


def _sample_attn_kernel(qa_ref, ka_ref, va_ref, qb_ref, kb_ref, vb_ref,
                        cak_ref, cav_ref, cbk_ref, cbv_ref, c_ref, tab_ref,
                        oa_ref, ob_ref, kn_ref, vn_ref, *, n_q):
    row = lax.broadcasted_iota(jnp.int32, (n_q, NEW_PAD), 0)
    col = lax.broadcasted_iota(jnp.int32, (n_q, NEW_PAD), 1)
    n_cache = cak_ref.shape[2]
    n_band = cbk_ref.shape[2]

    def stage_new(slot, k_new, v_new):
        kn_ref[slot] = jnp.zeros((NEW_PAD, PAIR), BF16)
        vn_ref[slot] = jnp.zeros((NEW_PAD, PAIR), BF16)
        kn_ref[slot, 0:n_q, :] = k_new
        vn_ref[slot, 0:n_q, :] = v_new
        return kn_ref.at[slot], vn_ref.at[slot]

    for p in range(N_PAIRS):
        feats = slice(p * PAIR, (p + 1) * PAIR)

        c_cache = c_ref[0, p, :, 0:n_cache]
        c_new = c_ref[0, p, :, n_cache:n_cache + NEW_PAD]

        def fox_cache(h, c_cache=c_cache, c_new=c_new):
            cq = jnp.sum(jnp.where(col == row, c_new[h:h + 1, :], 0.0), axis=1, keepdims=True)
            return cq - c_cache[h:h + 1, :]

        def fox_new(h, c_new=c_new):
            cq = jnp.sum(jnp.where(col == row, c_new[h:h + 1, :], 0.0), axis=1, keepdims=True)
            return jnp.where(col <= row, cq - c_new[h:h + 1, :], -jnp.inf)

        kn, vn = stage_new(2 * p, ka_ref[:, feats], va_ref[:, feats])
        oa = _two_part_attention(_split_heads_q(qa_ref[:, feats]), cak_ref[0, feats, :].astype(BF16),
                                 cav_ref[0, feats, :].astype(BF16), kn, vn, fox_cache, fox_new, n_q)
        oa_ref[:, feats] = oa.astype(oa_ref.dtype)

        def band_cache(h, p=p):
            return tab_ref[2 * p + h, :, 0:n_band]

        def band_new(h, p=p):
            return jnp.where(col < n_q, tab_ref[2 * p + h, :, n_band:n_band + NEW_PAD], -jnp.inf)

        kn, vn = stage_new(2 * p + 1, kb_ref[:, feats], vb_ref[:, feats])
        ob = _two_part_attention(_split_heads_q(qb_ref[:, feats]), cbk_ref[0, feats, :].astype(BF16),
                                 cbv_ref[0, feats, :].astype(BF16), kn, vn, band_cache, band_new, n_q)
        ob_ref[:, feats] = ob.astype(ob_ref.dtype)


def _rows_cumsum_kernel(x_ref, c_ref):
    c_ref[...] = _lane_cumsum(x_ref[...])


def _rows_cumsum(x):
    return pl.pallas_call(
        _rows_cumsum_kernel,
        out_shape=jax.ShapeDtypeStruct(x.shape, F32),
        compiler_params=_params(),
        name="rows_cumsum",
    )(x)


def _sample_attn(qa, ka, va, qb, kb, vb, cakt, cavt, cbkt, cbvt, c_all, tabr, *, batch, n_q):
    n_tok = batch * n_q
    past = cakt.shape[2]
    n_band = cbkt.shape[2]
    new = lambda: pl.BlockSpec((n_q, WIDTH), lambda b: (b, 0))
    cache = lambda n: pl.BlockSpec((1, WIDTH, n), lambda b: (b, 0, 0))
    return pl.pallas_call(
        functools.partial(_sample_attn_kernel, n_q=n_q),
        grid=(batch,),
        in_specs=[
            new(), new(), new(), new(), new(), new(),
            cache(past), cache(past), cache(n_band), cache(n_band),
            pl.BlockSpec((1, N_PAIRS, 2, past + NEW_PAD), lambda b: (b, 0, 0, 0)),
            _resident(tabr.shape),
        ],
        out_specs=[new(), new()],
        out_shape=[jax.ShapeDtypeStruct((n_tok, WIDTH), BF16)] * 2,
        scratch_shapes=[pltpu.VMEM((2 * N_PAIRS, NEW_PAD, PAIR), BF16),
                        pltpu.VMEM((2 * N_PAIRS, NEW_PAD, PAIR), BF16)],
        compiler_params=_params(),
        name="sample_attn",
    )(qa, ka, va, qb, kb, vb, cakt, cavt, cbkt, cbvt, c_all, tabr)


def _post_kernel(x_ref, oa_ref, ob_ref, ga_ref, gb_ref, wo_ref, gpost_ref, gpre_ref,
                 wg_ref, wu_ref, wd_ref, gffn_ref, y_ref, *, n_sub):
    sub = x_ref.shape[0] // n_sub
    rows = [slice(i * sub, (i + 1) * sub) for i in range(n_sub)]
    st = [dict() for _ in range(n_sub)]

    def norm(i):
        st[i]["na"] = _rms(oa_ref[rows[i], :].astype(F32), ga_ref[...]).astype(BF16)
        st[i]["nb"] = _rms(ob_ref[rows[i], :].astype(F32), gb_ref[...]).astype(BF16)

    def mix(i):
        st[i]["mix"] = _dot(st[i]["na"], wo_ref[0:WIDTH, :]) + _dot(st[i]["nb"], wo_ref[WIDTH:2 * WIDTH, :])

    def mid(i):
        x1 = x_ref[rows[i], :] + _rms(st[i]["mix"], gpost_ref[...])
        st[i]["x1"] = x1
        st[i]["hf"] = _rms(x1, gpre_ref[...]).astype(BF16)

    def gate_up(i):
        st[i]["g"] = _dot(st[i]["hf"], wg_ref[...])
        st[i]["u"] = _dot(st[i]["hf"], wu_ref[...])

    def silu(i):
        g = st[i]["g"]
        st[i]["a"] = (g * jax.nn.sigmoid(g) * st[i]["u"]).astype(BF16)

    def down(i):
        st[i]["ff"] = _dot(st[i]["a"], wd_ref[...])

    def fin(i):
        y_ref[rows[i], :] = st[i]["x1"] + _rms(st[i]["ff"], gffn_ref[...])

    vpu_stages = [norm, mid, silu, fin]
    mxu_stages = [mix, gate_up, down]
    norm(0)
    for k, mxu in enumerate(mxu_stages):
        for i in range(n_sub):
            mxu(i)
            if i + 1 < n_sub:
                vpu_stages[k](i + 1)
            else:
                vpu_stages[k + 1](0)
    for i in range(1, n_sub):
        fin(i)


def _post(x, oa, ob, ga, gb, wo, gpost, gpre, wg, wu, wd, gffn, *, tm, n_sub):
    n_tok, d_model = x.shape
    row = lambda w: pl.BlockSpec((tm, w), lambda i: (i, 0))
    return pl.pallas_call(
        functools.partial(_post_kernel, n_sub=n_sub),
        grid=(n_tok // tm,),
        in_specs=[
            row(d_model), row(WIDTH), row(WIDTH),
            _resident(ga.shape), _resident(gb.shape), _resident(wo.shape),
            _resident(gpost.shape), _resident(gpre.shape),
            _resident(wg.shape), _resident(wu.shape), _resident(wd.shape), _resident(gffn.shape),
        ],
        out_specs=row(d_model),
        out_shape=jax.ShapeDtypeStruct((n_tok, d_model), F32),
        compiler_params=_params(),
        name="post",
    )(x, oa, ob, ga, gb, wo, gpost, gpre, wg, wu, wd, gffn)


def _row(v):
    return v.reshape(1, -1).astype(F32)


def _state_from_feature_major(t, batch, n):
    return t.reshape(batch, N_HEADS, HEAD_DIM, n).transpose(0, 3, 1, 2)


def _cache_feature_major(c):
    b, n = c.shape[0], c.shape[1]
    return c.transpose(0, 2, 3, 1).reshape(b, WIDTH, n)


def kernel(x_prompt, x_sample, cache_a_k, cache_a_v, cache_a_logf, cache_b_k, cache_b_v, norm_mix_pre, w_in, b_forget, rel_bias, gain_out_a, gain_out_b, w_out, norm_mix_post, norm_ffn_pre, w_gate, w_up, w_down, norm_ffn_post):
    batch, seq, d_model = x_prompt.shape
    dec_batch, dec_seq, _ = x_sample.shape
    depth = w_in.shape[0]
    past = cache_a_k.shape[2]
    n_band = cache_b_k.shape[2]
    keep = min(LEFT_REACH, seq)
    assert seq % Q_TILE == 0 and keep == LEFT_REACH and n_band == LEFT_REACH and dec_seq <= CHUNK
    n_tok_p, n_tok_s = batch * seq, dec_batch * dec_seq
    tm_p = 2 * keep
    assert seq % tm_p == 0

    xp = x_prompt.reshape(n_tok_p, d_model)
    xs = x_sample.reshape(n_tok_s, d_model)
    outs = [[] for _ in range(10)]
    w_in_t = jnp.swapaxes(w_in, 1, 2)
    for l in range(depth):
        b_col = b_forget[l].reshape(N_HEADS, 1).astype(F32)
        rb_pad = jnp.pad(rel_bias[l].astype(F32),
                         ((0, RB_ROWS - N_HEADS), (0, RB_COLS - (2 * REL_CLIP + 1))))
        tail = (_row(gain_out_a[l]), _row(gain_out_b[l]), w_out[l].astype(BF16), _row(norm_mix_post[l]),
                _row(norm_ffn_pre[l]), w_gate[l].astype(BF16), w_up[l].astype(BF16), w_down[l].astype(BF16),
                _row(norm_ffn_post[l]))
        tabm, tabr = _bias_tables(rb_pad, n_new=dec_seq)

        qat, qbt, kat, vat, kbt, vbt, kat32, vat32, logf_p, kbt32, vbt32 = _inproj_t(
            xp, _row(norm_mix_pre[l]), w_in_t, l, b_col, batch=batch, seq=seq, tm=tm_p, keep_rows=keep)
        c2, kf = _cumsum(logf_p.reshape(batch * N_HEADS, seq), batch=batch, seq=seq)
        oa = _fox(qat, kat, vat, c2.reshape(batch, N_PAIRS, 2, seq), kf, batch=batch, seq=seq)
        ob = _band(qbt, kbt, vbt, tabm, batch=batch, seq=seq)
        xp = _post(xp, oa, ob, *tail, tm=512, n_sub=2)
        outs[0].append(_state_from_feature_major(kat32, batch, seq))
        outs[1].append(_state_from_feature_major(vat32, batch, seq))
        outs[2].append(logf_p.transpose(0, 2, 1))
        outs[3].append(_state_from_feature_major(kbt32, batch, keep))
        outs[4].append(_state_from_feature_major(vbt32, batch, keep))

        qa, ka, va, qb, kb, vb, ka32, va32, logf_t, kb32, vb32 = _inproj(xs, _row(norm_mix_pre[l]), w_in_t, l, b_col)
        lf_all = jnp.concatenate(
            [cache_a_logf[l].astype(F32).transpose(0, 2, 1),
             logf_t.reshape(N_HEADS, dec_batch, dec_seq).transpose(1, 0, 2),
             jnp.zeros((dec_batch, N_HEADS, NEW_PAD - dec_seq), F32)], axis=2)
        c_all = _rows_cumsum(lf_all.reshape(dec_batch * N_HEADS, past + NEW_PAD))
        oa, ob = _sample_attn(
            qa, ka, va, qb, kb, vb,
            _cache_feature_major(cache_a_k[l]), _cache_feature_major(cache_a_v[l]),
            _cache_feature_major(cache_b_k[l]), _cache_feature_major(cache_b_v[l]),
            c_all.reshape(dec_batch, N_PAIRS, 2, past + NEW_PAD), tabr, batch=dec_batch, n_q=dec_seq)
        xs = _post(xs, oa, ob, *tail, tm=n_tok_s, n_sub=1)
        outs[5].append(ka32.reshape(dec_batch, dec_seq, N_HEADS, HEAD_DIM))
        outs[6].append(va32.reshape(dec_batch, dec_seq, N_HEADS, HEAD_DIM))
        outs[7].append(logf_t.T.reshape(dec_batch, dec_seq, N_HEADS))
        outs[8].append(kb32.reshape(dec_batch, dec_seq, N_HEADS, HEAD_DIM))
        outs[9].append(vb32.reshape(dec_batch, dec_seq, N_HEADS, HEAD_DIM))

    st = jnp.stack
    return (xp.reshape(batch, seq, d_model), xs.reshape(dec_batch, dec_seq, d_model),
            *[st(o) for o in outs])
```

```python
import functools

import jax
import jax.numpy as jnp
from jax import lax
from jax.experimental import pallas as pl
from jax.experimental.pallas import tpu as pltpu

F32 = jnp.float32
BF16 = jnp.bfloat16

HEAD_DIM = 64
N_HEADS = 8
WIDTH = N_HEADS * HEAD_DIM
PAIR = 2 * HEAD_DIM
N_PAIRS = N_HEADS // 2
CHUNK = 64
LEFT_CHUNKS = 8
LEFT_REACH = LEFT_CHUNKS * CHUNK
REL_CLIP = 128
EPS = 1e-6
SCALE = HEAD_DIM ** -0.5
LOG2E = 1.4426950408889634
Q_TILE = 256
PAIRS_PER_STEP = 4
BAND_KEYS = LEFT_REACH + Q_TILE
TOEPLITZ = 1024
RB_ROWS, RB_COLS = 2 * N_HEADS, 384
HEAD_SHIFT = 6
MASKED = -1e30
NEW_PAD = 128
VMEM_LIMIT = 56 * 1024 * 1024
W_IN_COLS = {"qa": 0, "ka": WIDTH, "va": 2 * WIDTH, "forget": 3 * WIDTH,
             "qb": 3 * WIDTH + N_HEADS, "kb": 4 * WIDTH + N_HEADS, "vb": 5 * WIDTH + N_HEADS}


def _params(n_grid_axes=0):
    sem = ("arbitrary",) * n_grid_axes if n_grid_axes else None
    return pltpu.CompilerParams(vmem_limit_bytes=VMEM_LIMIT, dimension_semantics=sem)


def _resident(shape):
    return pl.BlockSpec(shape, lambda *_: (0,) * len(shape), pipeline_mode=pl.Buffered(1))


def _rms(x, g):
    ms = jnp.mean(x * x, axis=-1, keepdims=True)
    return x * lax.rsqrt(ms + EPS) * g


def _log_sigmoid(x):
    return jnp.minimum(x, 0.0) - jnp.log1p(jnp.exp(-jnp.abs(x)))


def _dot(a, b):
    return jnp.dot(a, b, preferred_element_type=F32)


def _dot_nt(a, b):
    return lax.dot_general(a, b, (((1,), (1,)), ((), ())), preferred_element_type=F32)


def _inproj_kernel(x_ref, g_ref, w_ref, bf_ref,
                   qa_ref, ka_ref, va_ref, qb_ref, kb_ref, vb_ref,
                   ka32_ref, va32_ref, logf_ref, kb32_ref, vb32_ref):
    h = _rms(x_ref[...], g_ref[...]).astype(BF16)

    def proj(name):
        r0 = W_IN_COLS[name]
        return _dot_nt(h, w_ref[0, r0:r0 + WIDTH, :].astype(BF16))

    qa_ref[...] = (proj("qa") * SCALE).astype(BF16)
    z = proj("ka")
    ka32_ref[...] = z
    ka_ref[...] = z.astype(BF16)
    z = proj("va")
    va32_ref[...] = z
    va_ref[...] = z.astype(BF16)
    qb_ref[...] = (proj("qb") * SCALE).astype(BF16)
    z = proj("kb")
    kb32_ref[...] = z
    kb_ref[...] = z.astype(BF16)
    z = proj("vb")
    vb32_ref[...] = z
    vb_ref[...] = z.astype(BF16)
    r0 = W_IN_COLS["forget"]
    ft = _dot_nt(w_ref[0, r0:r0 + 2 * N_HEADS, :].astype(BF16), h)
    logf_ref[...] = _log_sigmoid(ft[0:N_HEADS, :] + bf_ref[...])


def _inproj(x, gain, w_in, layer, b_col):
    n_tok, d_model = x.shape
    full = lambda w: pl.BlockSpec((n_tok, w), lambda i: (0, 0))
    w_spec = pl.BlockSpec((1,) + w_in.shape[1:], lambda i: (layer, 0, 0), pipeline_mode=pl.Buffered(1))
    out_shape = (
        [jax.ShapeDtypeStruct((n_tok, WIDTH), BF16)] * 6
        + [jax.ShapeDtypeStruct((n_tok, WIDTH), F32)] * 2
        + [jax.ShapeDtypeStruct((N_HEADS, n_tok), F32)]
        + [jax.ShapeDtypeStruct((n_tok, WIDTH), F32)] * 2
    )
    out_specs = ([full(WIDTH) for _ in range(8)]
                 + [pl.BlockSpec((N_HEADS, n_tok), lambda i: (0, 0)), full(WIDTH), full(WIDTH)])
    return pl.pallas_call(
        _inproj_kernel,
        grid=(1,),
        in_specs=[full(d_model), _resident((1, d_model)), w_spec, _resident((N_HEADS, 1))],
        out_specs=out_specs,
        out_shape=out_shape,
        compiler_params=_params(),
        name="inproj_sample",
    )(x, gain, w_in, b_col)


def _inproj_t_kernel(x_ref, g_ref, w_ref, bf_ref,
                     qat_ref, qbt_ref, kat_ref, vat_ref, kbt_ref, vbt_ref,
                     kat32_ref, vat32_ref, logf_ref, kbt32_ref, vbt32_ref, wt_ref, *, tiles_per_seq):
    @pl.when(pl.program_id(0) == 0)
    def _():
        for g, name in enumerate(("ka", "va", "kb", "vb", "qa", "qb")):
            r0 = W_IN_COLS[name]
            wt_ref[g * WIDTH:(g + 1) * WIDTH, :] = w_ref[0, r0:r0 + WIDTH, :].astype(BF16)
        r0 = W_IN_COLS["forget"]
        ft = w_ref[0, r0:r0 + 2 * N_HEADS, :]
        live = lax.broadcasted_iota(jnp.int32, ft.shape, 0) < N_HEADS
        wt_ref[6 * WIDTH:6 * WIDTH + 2 * N_HEADS, :] = jnp.where(live, ft, 0.0).astype(BF16)

    sub = Q_TILE
    n_sub = x_ref.shape[0] // sub
    keep_from = x_ref.shape[0] - kbt32_ref.shape[2]
    hs, kept = {}, []

    def norm(i):
        hs[i] = _rms(x_ref[i * sub:(i + 1) * sub, :], g_ref[...]).astype(BF16)

    def project(i, after_first=None):
        cols = slice(i * sub, (i + 1) * sub)

        def proj_t(c, rows=WIDTH):
            return _dot_nt(wt_ref[c * WIDTH:c * WIDTH + rows, :], hs[i])

        z = proj_t(0)
        kat32_ref[0, :, cols] = z
        kat_ref[0, :, cols] = z.astype(BF16)
        if after_first is not None:
            after_first()
        z = proj_t(1)
        vat32_ref[0, :, cols] = z
        vat_ref[0, :, cols] = z.astype(BF16)
        zk = proj_t(2)
        kbt_ref[0, :, cols] = zk.astype(BF16)
        zv = proj_t(3)
        vbt_ref[0, :, cols] = zv.astype(BF16)
        qat_ref[0, :, cols] = (proj_t(4) * (SCALE * LOG2E)).astype(BF16)
        zq = proj_t(5, WIDTH + 2 * N_HEADS)
        qbt_ref[0, :, cols] = (zq[0:WIDTH] * (SCALE * LOG2E)).astype(BF16)
        logf_ref[0, :, cols] = _log_sigmoid(zq[WIDTH:WIDTH + N_HEADS] + bf_ref[...])
        if i * sub >= keep_from:
            kept.append((slice(i * sub - keep_from, (i + 1) * sub - keep_from), zk, zv))

    norm(0)
    for i in range(n_sub):
        project(i, (lambda j=i + 1: norm(j)) if i + 1 < n_sub else None)

    @pl.when(pl.program_id(0) % tiles_per_seq == tiles_per_seq - 1)
    def _():
        for cols, zk, zv in kept:
            kbt32_ref[0, :, cols] = zk
            vbt32_ref[0, :, cols] = zv


def _inproj_t(x, gain, w_in, layer, b_col, *, batch, seq, tm, keep_rows):
    n_tok, d_model = x.shape
    tps = seq // tm
    w_spec = pl.BlockSpec((1,) + w_in.shape[1:], lambda i: (layer, 0, 0), pipeline_mode=pl.Buffered(1))
    feat = lambda: pl.BlockSpec((1, WIDTH, tm), lambda i: (i // tps, 0, i % tps))
    keep = lambda: pl.BlockSpec((1, WIDTH, keep_rows), lambda i: (i // tps, 0, 0))
    out_shape = (
        [jax.ShapeDtypeStruct((batch, WIDTH, seq), BF16)] * 6
        + [jax.ShapeDtypeStruct((batch, WIDTH, seq), F32)] * 2
        + [jax.ShapeDtypeStruct((batch, N_HEADS, seq), F32)]
        + [jax.ShapeDtypeStruct((batch, WIDTH, keep_rows), F32)] * 2
    )
    out_specs = ([feat() for _ in range(8)]
                 + [pl.BlockSpec((1, N_HEADS, tm), lambda i: (i // tps, 0, i % tps)), keep(), keep()])
    return pl.pallas_call(
        functools.partial(_inproj_t_kernel, tiles_per_seq=tps),
        grid=(n_tok // tm,),
        in_specs=[
            pl.BlockSpec((tm, d_model), lambda i: (i, 0)),
            _resident((1, d_model)),
            w_spec,
            _resident((N_HEADS, 1)),
        ],
        out_specs=out_specs,
        out_shape=out_shape,
        scratch_shapes=[pltpu.VMEM((6 * WIDTH + 2 * N_HEADS, d_model), BF16)],
        compiler_params=_params(1),
        name="inproj",
    )(x, gain, w_in, b_col)


def _lane_cumsum(x):
    n = x.shape[-1]
    lane = lax.broadcasted_iota(jnp.int32, x.shape, x.ndim - 1)
    shift = 1
    while shift < n:
        x = x + jnp.where(lane >= shift, pltpu.roll(x, shift, axis=x.ndim - 1), 0.0)
        shift *= 2
    return x


def _split3(x):
    hi = x.astype(BF16).astype(F32)
    rest = x - hi
    mid = rest.astype(BF16).astype(F32)
    return hi, mid, rest - mid


def _cumsum_kernel(lf_ref, c2_ref, kf_ref):
    c2 = _lane_cumsum(lf_ref[...]) * LOG2E
    c2_ref[...] = c2
    seq = c2.shape[1]
    hi, mid, lo = _split3(c2)
    pad = jnp.zeros((PAIR - 3 * N_HEADS, seq), F32)
    for b in range(kf_ref.shape[0]):
        rows = slice(b * N_HEADS, (b + 1) * N_HEADS)
        feat = jnp.concatenate([hi[rows], mid[rows], lo[rows], pad], axis=0)
        kf_ref[b] = feat.T.astype(BF16)


def _cumsum(logf, *, batch, seq):
    return pl.pallas_call(
        _cumsum_kernel,
        out_shape=[
            jax.ShapeDtypeStruct((batch * N_HEADS, seq), F32),
            jax.ShapeDtypeStruct((batch, seq, PAIR), BF16),
        ],
        compiler_params=_params(),
        name="cumsum",
    )(logf)


def _split_heads_q(q):
    lane = lax.broadcasted_iota(jnp.int32, q.shape, 1)
    zero = jnp.zeros_like(q)
    return jnp.concatenate([jnp.where(lane < HEAD_DIM, q, zero),
                            jnp.where(lane >= HEAD_DIM, q, zero)], axis=0)


def _augment_v(v, h, axis):
    idx = lax.broadcasted_iota(jnp.int32, v.shape, axis)
    own = (idx < HEAD_DIM) if h == 0 else (idx >= HEAD_DIM)
    return jnp.where(own, v, jnp.ones_like(v))


def _merge_heads_out(acc0, acc1):
    lane = lax.broadcasted_iota(jnp.int32, acc0.shape, 1)
    first = lane < HEAD_DIM
    num = jnp.where(first, acc0, acc1)
    den = pltpu.roll(jnp.where(first, acc1, acc0), HEAD_DIM, axis=1)
    return num / den


def _pair_queries(qt):
    n = qt.shape[1]
    q2 = jnp.concatenate([qt, qt], axis=1)
    r = lax.broadcasted_iota(jnp.int32, q2.shape, 0)
    c = lax.broadcasted_iota(jnp.int32, q2.shape, 1)
    own = jnp.right_shift(r, HEAD_SHIFT) == jnp.where(c >= n, 1, 0)
    return jnp.where(own, q2, jnp.zeros_like(q2))


def _merge_heads_out_t(acc0, acc1):
    first = lax.broadcasted_iota(jnp.int32, acc0.shape, 0) < HEAD_DIM
    num = jnp.where(first, acc0, acc1)
    den = jnp.where(first, acc0[HEAD_DIM:HEAD_DIM + 1, :], acc1[0:1, :])
    return (num / den).T


def _weighted_values(vaug, p_blocks, key_ranges):
    accs = []
    for h in range(2):
        cols = slice(h * Q_TILE, (h + 1) * Q_TILE)
        acc = None
        for p, (k0, k1) in zip(p_blocks, key_ranges):
            part = _dot(vaug[h][:, k0:k1], p[:, cols])
            acc = part if acc is None else acc + part
        accs.append(acc)
    return accs


def _fox_kernel(qt_ref, kt_ref, vt_ref, c2_ref, kf_ref, o_ref, v0_ref, v1_ref, kx_ref, *, seq):
    for pp in range(PAIRS_PER_STEP):
        feats = slice(pp * PAIR, (pp + 1) * PAIR)
        _fox_pair(PAIRS_PER_STEP * pl.program_id(1) + pp,
                  qt_ref.at[:, feats, :], kt_ref.at[:, feats, :], vt_ref.at[:, feats, :],
                  c2_ref.at[:, pp:pp + 1], kf_ref, o_ref.at[:, feats],
                  v0_ref.at[pp], v1_ref.at[pp], kx_ref.at[pp], seq=seq)


def _fox_pair(pair, qt_ref, kt_ref, vt_ref, c2_ref, kf_ref, o_ref, v0_ref, v1_ref, kx_ref, *, seq):
    vt = vt_ref[0]
    v0_ref[...] = _augment_v(vt, 0, 0)
    v1_ref[...] = _augment_v(vt, 1, 0)
    vaug = (v0_ref, v1_ref)

    kx_ref[:, 0:PAIR] = kt_ref[0].astype(F32).T.astype(BF16)
    kx_ref[:, PAIR:2 * PAIR] = kf_ref[0]
    er = lax.broadcasted_iota(jnp.int32, (PAIR, 2 * Q_TILE), 0)
    ec = lax.broadcasted_iota(jnp.int32, (PAIR, 2 * Q_TILE), 1)
    head = 2 * pair + jnp.where(ec >= Q_TILE, 1, 0)
    piece = er - head
    coef = jnp.where((piece == 0) | (piece == N_HEADS) | (piece == 2 * N_HEADS), -1.0, 0.0).astype(BF16)
    key = lax.broadcasted_iota(jnp.int32, (Q_TILE, 2 * Q_TILE), 0)
    qry = jnp.bitwise_and(lax.broadcasted_iota(jnp.int32, (Q_TILE, 2 * Q_TILE), 1), Q_TILE - 1)
    causal = key <= qry

    n_qb = seq // Q_TILE

    def scores(qb):
        qs, ke = qb * Q_TILE, (qb + 1) * Q_TILE
        rhs = jnp.concatenate([_pair_queries(qt_ref[0, :, qs:ke]), coef], axis=0)
        return _dot(kx_ref[0:ke, :], rhs)

    def softmax(qb, s):
        qs, ke = qb * Q_TILE, (qb + 1) * Q_TILE
        p_diag, p_past = [], []
        for h in range(2):
            cols = slice(h * Q_TILE, (h + 1) * Q_TILE)
            cq = c2_ref[0, 0, h:h + 1, qs:ke]
            t_diag = jnp.where(causal[:, cols], s[qs:ke, cols], -jnp.inf)
            m = jnp.max(t_diag, axis=0, keepdims=True)
            if qb > 0:
                t_past = s[0:qs, cols]
                m = jnp.maximum(m, jnp.max(t_past, axis=0, keepdims=True))
            shift = cq - (m + cq)
            p_diag.append(jnp.exp2(t_diag + shift).astype(BF16))
            if qb > 0:
                p_past.append(jnp.exp2(t_past + shift).astype(BF16))
        p_blocks = [jnp.concatenate(p_diag, axis=1)]
        key_ranges = [(qs, ke)]
        if qb > 0:
            p_blocks.append(jnp.concatenate(p_past, axis=1))
            key_ranges.append((0, qs))
        return p_blocks, key_ranges

    def finish(qb, p_blocks, key_ranges):
        acc0, acc1 = _weighted_values(vaug, p_blocks, key_ranges)
        o_ref[qb * Q_TILE:(qb + 1) * Q_TILE, :] = _merge_heads_out_t(acc0, acc1).astype(o_ref.dtype)

    order = list(range(n_qb - 1, -1, -1))
    s_next = scores(order[0])
    pending = None
    for i, qb in enumerate(order):
        s = s_next
        if i + 1 < n_qb:
            s_next = scores(order[i + 1])
        probs = softmax(qb, s)
        if pending is not None:
            finish(*pending)
        pending = (qb, *probs)
    finish(*pending)


def _fox(qt, kt, vt, c2, kf, *, batch, seq):
    n_tok = batch * seq
    pps = PAIRS_PER_STEP
    feat = lambda: pl.BlockSpec((1, pps * PAIR, seq), lambda b, p: (b, p, 0))
    return pl.pallas_call(
        functools.partial(_fox_kernel, seq=seq),
        grid=(batch, N_PAIRS // pps),
        in_specs=[
            feat(), feat(), feat(),
            pl.BlockSpec((1, pps, 2, seq), lambda b, p: (b, p, 0, 0)),
            pl.BlockSpec((1, seq, PAIR), lambda b, p: (b, 0, 0)),
        ],
        out_specs=pl.BlockSpec((seq, pps * PAIR), lambda b, p: (b, p)),
        out_shape=jax.ShapeDtypeStruct((n_tok, WIDTH), BF16),
        scratch_shapes=[pltpu.VMEM((pps, PAIR, seq), BF16), pltpu.VMEM((pps, PAIR, seq), BF16),
                        pltpu.VMEM((pps, seq, 2 * PAIR), BF16)],
        compiler_params=_params(),
        name="fox",
    )(qt, kt, vt, c2, kf)


def _bias_kernel(rb_ref, tabm_ref, tabr_ref):
    rb = rb_ref[...]
    n_rows = rb.shape[1]
    r = lax.broadcasted_iota(jnp.int32, (n_rows, TOEPLITZ), 0)
    m = lax.broadcasted_iota(jnp.int32, (n_rows, TOEPLITZ), 1)
    e = jnp.where(m < TOEPLITZ // 2, m, m - TOEPLITZ)
    idx = jnp.clip(LEFT_REACH - Q_TILE - e, -REL_CLIP, REL_CLIP) + REL_CLIP
    onehot = jnp.where(r == idx, 1.0, 0.0).astype(BF16)
    hi = rb.astype(BF16)
    rest = rb - hi.astype(F32)
    mid = rest.astype(BF16)
    lo = (rest - mid.astype(F32)).astype(BF16)
    u = _dot(hi, onehot) + _dot(mid, onehot) + _dot(lo, onehot)

    i = lax.broadcasted_iota(jnp.int32, (Q_TILE, BAND_KEYS), 0)
    j = lax.broadcasted_iota(jnp.int32, (Q_TILE, BAND_KEYS), 1)
    qc = jnp.right_shift(i, HEAD_SHIFT)
    kc = jnp.right_shift(j, HEAD_SHIFT)
    valid = (kc >= qc) & (kc <= qc + LEFT_CHUNKS)
    n_new = tabr_ref.shape[1]
    for h in range(N_HEADS):
        base = jnp.broadcast_to(u[h:h + 1, :], (Q_TILE, TOEPLITZ))
        t = pltpu.roll(base, Q_TILE, axis=1, stride=1, stride_axis=0)[:, 0:BAND_KEYS]
        half = (h % 2) * Q_TILE
        tabm_ref[h // 2, :, half:half + Q_TILE] = jnp.where(valid, t * LOG2E, MASKED).T
        tabr_ref[h] = t[0:n_new, :]


def _bias_tables(rb_pad, *, n_new):
    return pl.pallas_call(
        _bias_kernel,
        out_shape=[
            jax.ShapeDtypeStruct((N_PAIRS, BAND_KEYS, 2 * Q_TILE), F32),
            jax.ShapeDtypeStruct((N_HEADS, n_new, BAND_KEYS), F32),
        ],
        compiler_params=_params(),
        name="bias_tables",
    )(rb_pad)


def _band_kernel(qt_ref, kt_ref, vt_ref, tab_ref, o_ref, v0_ref, v1_ref, k_ref, *, seq):
    for pp in range(PAIRS_PER_STEP):
        feats = slice(pp * PAIR, (pp + 1) * PAIR)
        _band_pair(qt_ref.at[:, feats, :], kt_ref.at[:, feats, :], vt_ref.at[:, feats, :],
                   tab_ref.at[pp:pp + 1], o_ref.at[:, feats],
                   v0_ref.at[pp], v1_ref.at[pp], k_ref.at[pp], seq=seq)


def _band_pair(qt_ref, kt_ref, vt_ref, tab_ref, o_ref, v0_ref, v1_ref, k_ref, *, seq):
    vt = vt_ref[0]
    v0_ref[...] = _augment_v(vt, 0, 0)
    v1_ref[...] = _augment_v(vt, 1, 0)
    vaug = (v0_ref, v1_ref)
    k_ref[...] = kt_ref[0].astype(F32).T.astype(BF16)

    n_qb = seq // Q_TILE

    def scores(qb):
        qs, ke = qb * Q_TILE, (qb + 1) * Q_TILE
        ks = max(0, qs - LEFT_REACH)
        return _dot(k_ref[ks:ke, :], _pair_queries(qt_ref[0, :, qs:ke]))

    def softmax(qb, s):
        n_keys = s.shape[0]
        halves = []
        for h in range(2):
            cols = slice(h * Q_TILE, (h + 1) * Q_TILE)
            t = s[:, cols] + tab_ref[0, BAND_KEYS - n_keys:BAND_KEYS, cols]
            m = jnp.max(t, axis=0, keepdims=True)
            halves.append(jnp.exp2(t - m).astype(BF16))
        return jnp.concatenate(halves, axis=1)

    def finish(qb, p):
        ke = (qb + 1) * Q_TILE
        acc0, acc1 = _weighted_values(vaug, [p], [(ke - p.shape[0], ke)])
        o_ref[qb * Q_TILE:ke, :] = _merge_heads_out_t(acc0, acc1).astype(o_ref.dtype)

    order = list(range(n_qb - 1, -1, -1))
    s_next = scores(order[0])
    pending = None
    for i, qb in enumerate(order):
        s = s_next
        if i + 1 < n_qb:
            s_next = scores(order[i + 1])
        p = softmax(qb, s)
        if pending is not None:
            finish(*pending)
        pending = (qb, p)
    finish(*pending)


def _band(qt, kt, vt, tabm, *, batch, seq):
    n_tok = batch * seq
    pps = PAIRS_PER_STEP
    feat = lambda: pl.BlockSpec((1, pps * PAIR, seq), lambda p, b: (b, p, 0))
    return pl.pallas_call(
        functools.partial(_band_kernel, seq=seq),
        grid=(N_PAIRS // pps, batch),
        in_specs=[
            feat(), feat(), feat(),
            pl.BlockSpec((pps, BAND_KEYS, 2 * Q_TILE), lambda p, b: (p, 0, 0)),
        ],
        out_specs=pl.BlockSpec((seq, pps * PAIR), lambda p, b: (b, p)),
        out_shape=jax.ShapeDtypeStruct((n_tok, WIDTH), BF16),
        scratch_shapes=[pltpu.VMEM((pps, PAIR, seq), BF16), pltpu.VMEM((pps, PAIR, seq), BF16),
                        pltpu.VMEM((pps, seq, PAIR), BF16)],
        compiler_params=_params(),
        name="band",
    )(qt, kt, vt, tabm)


def _two_part_attention(q2, kt_cache, vt_cache, kn_ref, vn_ref, bias_cache, bias_new, n_q):
    s_c = _dot(q2, kt_cache)
    s_n = _dot_nt(q2, kn_ref[...])
    accs = []
    for h in range(2):
        rows = slice(h * n_q, (h + 1) * n_q)
        t_c = s_c[rows, :] + bias_cache(h)
        t_n = s_n[rows, :] + bias_new(h)
        m = jnp.maximum(jnp.max(t_c, axis=1, keepdims=True), jnp.max(t_n, axis=1, keepdims=True))
        acc = _dot_nt(jnp.exp(t_c - m).astype(BF16), _augment_v(vt_cache, h, 0))
        acc = acc + _dot(jnp.exp(t_n - m).astype(BF16), _augment_v(vn_ref[...], h, 1))
        accs.append(acc)
    return _merge_heads_out(accs[0], accs[1])


def _sample_attn_kernel(qa_ref, ka_ref, va_ref, qb_ref, kb_ref, vb_ref,
                        cak_ref, cav_ref, cbk_ref, cbv_ref, c_ref, tab_ref,
                        oa_ref, ob_ref, kn_ref, vn_ref, *, n_q):
    row = lax.broadcasted_iota(jnp.int32, (n_q, NEW_PAD), 0)
    col = lax.broadcasted_iota(jnp.int32, (n_q, NEW_PAD), 1)
    n_cache = cak_ref.shape[2]
    n_band = cbk_ref.shape[2]

    def stage_new(slot, k_new, v_new):
        kn_ref[slot] = jnp.zeros((NEW_PAD, PAIR), BF16)
        vn_ref[slot] = jnp.zeros((NEW_PAD, PAIR), BF16)
        kn_ref[slot, 0:n_q, :] = k_new
        vn_ref[slot, 0:n_q, :] = v_new
        return kn_ref.at[slot], vn_ref.at[slot]

    for p in range(N_PAIRS):
        feats = slice(p * PAIR, (p + 1) * PAIR)

        c_cache = c_ref[0, p, :, 0:n_cache]
        c_new = c_ref[0, p, :, n_cache:n_cache + NEW_PAD]

        def fox_cache(h, c_cache=c_cache, c_new=c_new):
            cq = jnp.sum(jnp.where(col == row, c_new[h:h + 1, :], 0.0), axis=1, keepdims=True)
            return cq - c_cache[h:h + 1, :]

        def fox_new(h, c_new=c_new):
            cq = jnp.sum(jnp.where(col == row, c_new[h:h + 1, :], 0.0), axis=1, keepdims=True)
            return jnp.where(col <= row, cq - c_new[h:h + 1, :], -jnp.inf)

        kn, vn = stage_new(2 * p, ka_ref[:, feats], va_ref[:, feats])
        oa = _two_part_attention(_split_heads_q(qa_ref[:, feats]), cak_ref[0, feats, :].astype(BF16),
                                 cav_ref[0, feats, :].astype(BF16), kn, vn, fox_cache, fox_new, n_q)
        oa_ref[:, feats] = oa.astype(oa_ref.dtype)

        def band_cache(h, p=p):
            return tab_ref[2 * p + h, :, 0:n_band]

        def band_new(h, p=p):
            return jnp.where(col < n_q, tab_ref[2 * p + h, :, n_band:n_band + NEW_PAD], -jnp.inf)

        kn, vn = stage_new(2 * p + 1, kb_ref[:, feats], vb_ref[:, feats])
        ob = _two_part_attention(_split_heads_q(qb_ref[:, feats]), cbk_ref[0, feats, :].astype(BF16),
                                 cbv_ref[0, feats, :].astype(BF16), kn, vn, band_cache, band_new, n_q)
        ob_ref[:, feats] = ob.astype(ob_ref.dtype)


def _rows_cumsum_kernel(x_ref, c_ref):
    c_ref[...] = _lane_cumsum(x_ref[...])


def _rows_cumsum(x):
    return pl.pallas_call(
        _rows_cumsum_kernel,
        out_shape=jax.ShapeDtypeStruct(x.shape, F32),
        compiler_params=_params(),
        name="rows_cumsum",
    )(x)


def _sample_attn(qa, ka, va, qb, kb, vb, cakt, cavt, cbkt, cbvt, c_all, tabr, *, batch, n_q):
    n_tok = batch * n_q
    past = cakt.shape[2]
    n_band = cbkt.shape[2]
    new = lambda: pl.BlockSpec((n_q, WIDTH), lambda b: (b, 0))
    cache = lambda n: pl.BlockSpec((1, WIDTH, n), lambda b: (b, 0, 0))
    return pl.pallas_call(
        functools.partial(_sample_attn_kernel, n_q=n_q),
        grid=(batch,),
        in_specs=[
            new(), new(), new(), new(), new(), new(),
            cache(past), cache(past), cache(n_band), cache(n_band),
            pl.BlockSpec((1, N_PAIRS, 2, past + NEW_PAD), lambda b: (b, 0, 0, 0)),
            _resident(tabr.shape),
        ],
        out_specs=[new(), new()],
        out_shape=[jax.ShapeDtypeStruct((n_tok, WIDTH), BF16)] * 2,
        scratch_shapes=[pltpu.VMEM((2 * N_PAIRS, NEW_PAD, PAIR), BF16),
                        pltpu.VMEM((2 * N_PAIRS, NEW_PAD, PAIR), BF16)],
        compiler_params=_params(),
        name="sample_attn",
    )(qa, ka, va, qb, kb, vb, cakt, cavt, cbkt, cbvt, c_all, tabr)


def _post_kernel(x_ref, oa_ref, ob_ref, ga_ref, gb_ref, wo_ref, gpost_ref, gpre_ref,
                 wg_ref, wu_ref, wd_ref, gffn_ref, y_ref, *, n_sub):
    sub = x_ref.shape[0] // n_sub
    rows = [slice(i * sub, (i + 1) * sub) for i in range(n_sub)]
    st = [dict() for _ in range(n_sub)]

    def norm(i):
        st[i]["na"] = _rms(oa_ref[rows[i], :].astype(F32), ga_ref[...]).astype(BF16)
        st[i]["nb"] = _rms(ob_ref[rows[i], :].astype(F32), gb_ref[...]).astype(BF16)

    def mix(i):
        st[i]["mix"] = _dot(st[i]["na"], wo_ref[0:WIDTH, :]) + _dot(st[i]["nb"], wo_ref[WIDTH:2 * WIDTH, :])

    def mid(i):
        x1 = x_ref[rows[i], :] + _rms(st[i]["mix"], gpost_ref[...])
        st[i]["x1"] = x1
        st[i]["hf"] = _rms(x1, gpre_ref[...]).astype(BF16)

    def gate_up(i):
        st[i]["g"] = _dot(st[i]["hf"], wg_ref[...])
        st[i]["u"] = _dot(st[i]["hf"], wu_ref[...])

    def silu(i):
        g = st[i]["g"]
        st[i]["a"] = (g * jax.nn.sigmoid(g) * st[i]["u"]).astype(BF16)

    def down(i):
        st[i]["ff"] = _dot(st[i]["a"], wd_ref[...])

    def fin(i):
        y_ref[rows[i], :] = st[i]["x1"] + _rms(st[i]["ff"], gffn_ref[...])

    vpu_stages = [norm, mid, silu, fin]
    mxu_stages = [mix, gate_up, down]
    norm(0)
    for k, mxu in enumerate(mxu_stages):
        for i in range(n_sub):
            mxu(i)
            if i + 1 < n_sub:
                vpu_stages[k](i + 1)
            else:
                vpu_stages[k + 1](0)
    for i in range(1, n_sub):
        fin(i)


def _post(x, oa, ob, ga, gb, wo, gpost, gpre, wg, wu, wd, gffn, *, tm, n_sub):
    n_tok, d_model = x.shape
    row = lambda w: pl.BlockSpec((tm, w), lambda i: (i, 0))
    return pl.pallas_call(
        functools.partial(_post_kernel, n_sub=n_sub),
        grid=(n_tok // tm,),
        in_specs=[
            row(d_model), row(WIDTH), row(WIDTH),
            _resident(ga.shape), _resident(gb.shape), _resident(wo.shape),
            _resident(gpost.shape), _resident(gpre.shape),
            _resident(wg.shape), _resident(wu.shape), _resident(wd.shape), _resident(gffn.shape),
        ],
        out_specs=row(d_model),
        out_shape=jax.ShapeDtypeStruct((n_tok, d_model), F32),
        compiler_params=_params(),
        name="post",
    )(x, oa, ob, ga, gb, wo, gpost, gpre, wg, wu, wd, gffn)


def _row(v):
    return v.reshape(1, -1).astype(F32)


def _state_from_feature_major(t, batch, n):
    return t.reshape(batch, N_HEADS, HEAD_DIM, n).transpose(0, 3, 1, 2)


def _cache_feature_major(c):
    b, n = c.shape[0], c.shape[1]
    return c.transpose(0, 2, 3, 1).reshape(b, WIDTH, n)


def kernel(x_prompt, x_sample, cache_a_k, cache_a_v, cache_a_logf, cache_b_k, cache_b_v, norm_mix_pre, w_in, b_forget, rel_bias, gain_out_a, gain_out_b, w_out, norm_mix_post, norm_ffn_pre, w_gate, w_up, w_down, norm_ffn_post):
    batch, seq, d_model = x_prompt.shape
    dec_batch, dec_seq, _ = x_sample.shape
    depth = w_in.shape[0]
    past = cache_a_k.shape[2]
    n_band = cache_b_k.shape[2]
    keep = min(LEFT_REACH, seq)
    assert seq % Q_TILE == 0 and keep == LEFT_REACH and n_band == LEFT_REACH and dec_seq <= CHUNK
    n_tok_p, n_tok_s = batch * seq, dec_batch * dec_seq
    tm_p = 2 * keep
    assert seq % tm_p == 0

    xp = x_prompt.reshape(n_tok_p, d_model)
    xs = x_sample.reshape(n_tok_s, d_model)
    outs = [[] for _ in range(10)]
    w_in_t = jnp.swapaxes(w_in, 1, 2)
    for l in range(depth):
        b_col = b_forget[l].reshape(N_HEADS, 1).astype(F32)
        rb_pad = jnp.pad(rel_bias[l].astype(F32),
                         ((0, RB_ROWS - N_HEADS), (0, RB_COLS - (2 * REL_CLIP + 1))))
        tail = (_row(gain_out_a[l]), _row(gain_out_b[l]), w_out[l].astype(BF16), _row(norm_mix_post[l]),
                _row(norm_ffn_pre[l]), w_gate[l].astype(BF16), w_up[l].astype(BF16), w_down[l].astype(BF16),
                _row(norm_ffn_post[l]))
        tabm, tabr = _bias_tables(rb_pad, n_new=dec_seq)

        qat, qbt, kat, vat, kbt, vbt, kat32, vat32, logf_p, kbt32, vbt32 = _inproj_t(
            xp, _row(norm_mix_pre[l]), w_in_t, l, b_col, batch=batch, seq=seq, tm=tm_p, keep_rows=keep)
        c2, kf = _cumsum(logf_p.reshape(batch * N_HEADS, seq), batch=batch, seq=seq)
        oa = _fox(qat, kat, vat, c2.reshape(batch, N_PAIRS, 2, seq), kf, batch=batch, seq=seq)
        ob = _band(qbt, kbt, vbt, tabm, batch=batch, seq=seq)
        xp = _post(xp, oa, ob, *tail, tm=512, n_sub=2)
        outs[0].append(_state_from_feature_major(kat32, batch, seq))
        outs[1].append(_state_from_feature_major(vat32, batch, seq))
        outs[2].append(logf_p.transpose(0, 2, 1))
        outs[3].append(_state_from_feature_major(kbt32, batch, keep))
        outs[4].append(_state_from_feature_major(vbt32, batch, keep))

        qa, ka, va, qb, kb, vb, ka32, va32, logf_t, kb32, vb32 = _inproj(xs, _row(norm_mix_pre[l]), w_in_t, l, b_col)
        lf_all = jnp.concatenate(
            [cache_a_logf[l].astype(F32).transpose(0, 2, 1),
             logf_t.reshape(N_HEADS, dec_batch, dec_seq).transpose(1, 0, 2),
             jnp.zeros((dec_batch, N_HEADS, NEW_PAD - dec_seq), F32)], axis=2)
        c_all = _rows_cumsum(lf_all.reshape(dec_batch * N_HEADS, past + NEW_PAD))
        oa, ob = _sample_attn(
            qa, ka, va, qb, kb, vb,
            _cache_feature_major(cache_a_k[l]), _cache_feature_major(cache_a_v[l]),
            _cache_feature_major(cache_b_k[l]), _cache_feature_major(cache_b_v[l]),
            c_all.reshape(dec_batch, N_PAIRS, 2, past + NEW_PAD), tabr, batch=dec_batch, n_q=dec_seq)
        xs = _post(xs, oa, ob, *tail, tm=n_tok_s, n_sub=1)
        outs[5].append(ka32.reshape(dec_batch, dec_seq, N_HEADS, HEAD_DIM))
        outs[6].append(va32.reshape(dec_batch, dec_seq, N_HEADS, HEAD_DIM))
        outs[7].append(logf_t.T.reshape(dec_batch, dec_seq, N_HEADS))
        outs[8].append(kb32.reshape(dec_batch, dec_seq, N_HEADS, HEAD_DIM))
        outs[9].append(vb32.reshape(dec_batch, dec_seq, N_HEADS, HEAD_DIM))

    st = jnp.stack
    return (xp.reshape(batch, seq, d_model), xs.reshape(dec_batch, dec_seq, d_model),
            *[st(o) for o in outs])
```

```python
import functools

import jax
import jax.numpy as jnp
from jax import lax
from jax.experimental import pallas as pl
from jax.experimental.pallas import tpu as pltpu

F32 = jnp.float32
BF16 = jnp.bfloat16

HEAD_DIM = 64
N_HEADS = 8
WIDTH = N_HEADS * HEAD_DIM
PAIR = 2 * HEAD_DIM
N_PAIRS = N_HEADS // 2
CHUNK = 64
LEFT_CHUNKS = 8
LEFT_REACH = LEFT_CHUNKS * CHUNK
REL_CLIP = 128
EPS = 1e-6
SCALE = HEAD_DIM ** -0.5
LOG2E = 1.4426950408889634
Q_TILE = 256
PAIRS_PER_STEP = 4
BAND_KEYS = LEFT_REACH + Q_TILE
TOEPLITZ = 1024
RB_ROWS, RB_COLS = 2 * N_HEADS, 384
HEAD_SHIFT = 6
MASKED = -1e30
NEW_PAD = 128
VMEM_LIMIT = 56 * 1024 * 1024
W_IN_COLS = {"qa": 0, "ka": WIDTH, "va": 2 * WIDTH, "forget": 3 * WIDTH,
             "qb": 3 * WIDTH + N_HEADS, "kb": 4 * WIDTH + N_HEADS, "vb": 5 * WIDTH + N_HEADS}


def _params(n_grid_axes=0):
    sem = ("arbitrary",) * n_grid_axes if n_grid_axes else None
    return pltpu.CompilerParams(vmem_limit_bytes=VMEM_LIMIT, dimension_semantics=sem)


def _resident(shape):
    return pl.BlockSpec(shape, lambda *_: (0,) * len(shape), pipeline_mode=pl.Buffered(1))


def _rms(x, g):
    ms = jnp.mean(x * x, axis=-1, keepdims=True)
    return x * lax.rsqrt(ms + EPS) * g


def _log_sigmoid(x):
    return jnp.minimum(x, 0.0) - jnp.log1p(jnp.exp(-jnp.abs(x)))


def _dot(a, b):
    return jnp.dot(a, b, preferred_element_type=F32)


def _dot_nt(a, b):
    return lax.dot_general(a, b, (((1,), (1,)), ((), ())), preferred_element_type=F32)


def _inproj_kernel(x_ref, g_ref, w_ref, bf_ref,
                   qa_ref, ka_ref, va_ref, qb_ref, kb_ref, vb_ref,
                   ka32_ref, va32_ref, logf_ref, kb32_ref, vb32_ref):
    h = _rms(x_ref[...], g_ref[...]).astype(BF16)

    def proj(name):
        r0 = W_IN_COLS[name]
        return _dot_nt(h, w_ref[0, r0:r0 + WIDTH, :].astype(BF16))

    qa_ref[...] = (proj("qa") * SCALE).astype(BF16)
    z = proj("ka")
    ka32_ref[...] = z
    ka_ref[...] = z.astype(BF16)
    z = proj("va")
    va32_ref[...] = z
    va_ref[...] = z.astype(BF16)
    qb_ref[...] = (proj("qb") * SCALE).astype(BF16)
    z = proj("kb")
    kb32_ref[...] = z
    kb_ref[...] = z.astype(BF16)
    z = proj("vb")
    vb32_ref[...] = z
    vb_ref[...] = z.astype(BF16)
    r0 = W_IN_COLS["forget"]
    ft = _dot_nt(w_ref[0, r0:r0 + 2 * N_HEADS, :].astype(BF16), h)
    logf_ref[...] = _log_sigmoid(ft[0:N_HEADS, :] + bf_ref[...])


def _inproj(x, gain, w_in, layer, b_col):
    n_tok, d_model = x.shape
    full = lambda w: pl.BlockSpec((n_tok, w), lambda i: (0, 0))
    w_spec = pl.BlockSpec((1,) + w_in.shape[1:], lambda i: (layer, 0, 0), pipeline_mode=pl.Buffered(1))
    out_shape = (
        [jax.ShapeDtypeStruct((n_tok, WIDTH), BF16)] * 6
        + [jax.ShapeDtypeStruct((n_tok, WIDTH), F32)] * 2
        + [jax.ShapeDtypeStruct((N_HEADS, n_tok), F32)]
        + [jax.ShapeDtypeStruct((n_tok, WIDTH), F32)] * 2
    )
    out_specs = ([full(WIDTH) for _ in range(8)]
                 + [pl.BlockSpec((N_HEADS, n_tok), lambda i: (0, 0)), full(WIDTH), full(WIDTH)])
    return pl.pallas_call(
        _inproj_kernel,
        grid=(1,),
        in_specs=[full(d_model), _resident((1, d_model)), w_spec, _resident((N_HEADS, 1))],
        out_specs=out_specs,
        out_shape=out_shape,
        compiler_params=_params(),
        name="inproj_sample",
    )(x, gain, w_in, b_col)


def _inproj_t_kernel(x_ref, g_ref, w_ref, bf_ref,
                     qat_ref, qbt_ref, kat_ref, vat_ref, kbt_ref, vbt_ref,
                     kat32_ref, vat32_ref, logf_ref, kbt32_ref, vbt32_ref, wt_ref, *, tiles_per_seq):
    @pl.when(pl.program_id(0) == 0)
    def _():
        for g, name in enumerate(("ka", "va", "kb", "vb", "qa", "qb")):
            r0 = W_IN_COLS[name]
            wt_ref[g * WIDTH:(g + 1) * WIDTH, :] = w_ref[0, r0:r0 + WIDTH, :].astype(BF16)
        r0 = W_IN_COLS["forget"]
        ft = w_ref[0, r0:r0 + 2 * N_HEADS, :]
        live = lax.broadcasted_iota(jnp.int32, ft.shape, 0) < N_HEADS
        wt_ref[6 * WIDTH:6 * WIDTH + 2 * N_HEADS, :] = jnp.where(live, ft, 0.0).astype(BF16)

    sub = Q_TILE
    n_sub = x_ref.shape[0] // sub
    keep_from = x_ref.shape[0] - kbt32_ref.shape[2]
    hs, kept = {}, []

    def norm(i):
        hs[i] = _rms(x_ref[i * sub:(i + 1) * sub, :], g_ref[...]).astype(BF16)

    def project(i, after_first=None):
        cols = slice(i * sub, (i + 1) * sub)

        def proj_t(c, rows=WIDTH):
            return _dot_nt(wt_ref[c * WIDTH:c * WIDTH + rows, :], hs[i])

        z = proj_t(0)
        kat32_ref[0, :, cols] = z
        kat_ref[0, :, cols] = z.astype(BF16)
        if after_first is not None:
            after_first()
        z = proj_t(1)
        vat32_ref[0, :, cols] = z
        vat_ref[0, :, cols] = z.astype(BF16)
        zk = proj_t(2)
        kbt_ref[0, :, cols] = zk.astype(BF16)
        zv = proj_t(3)
        vbt_ref[0, :, cols] = zv.astype(BF16)
        qat_ref[0, :, cols] = (proj_t(4) * (SCALE * LOG2E)).astype(BF16)
        zq = proj_t(5, WIDTH + 2 * N_HEADS)
        qbt_ref[0, :, cols] = (zq[0:WIDTH] * (SCALE * LOG2E)).astype(BF16)
        logf_ref[0, :, cols] = _log_sigmoid(zq[WIDTH:WIDTH + N_HEADS] + bf_ref[...])
        if i * sub >= keep_from:
            kept.append((slice(i * sub - keep_from, (i + 1) * sub - keep_from), zk, zv))

    norm(0)
    for i in range(n_sub):
        project(i, (lambda j=i + 1: norm(j)) if i + 1 < n_sub else None)

    @pl.when(pl.program_id(0) % tiles_per_seq == tiles_per_seq - 1)
    def _():
        for cols, zk, zv in kept:
            kbt32_ref[0, :, cols] = zk
            vbt32_ref[0, :, cols] = zv


def _inproj_t(x, gain, w_in, layer, b_col, *, batch, seq, tm, keep_rows):
    n_tok, d_model = x.shape
    tps = seq // tm
    w_spec = pl.BlockSpec((1,) + w_in.shape[1:], lambda i: (layer, 0, 0), pipeline_mode=pl.Buffered(1))
    feat = lambda: pl.BlockSpec((1, WIDTH, tm), lambda i: (i // tps, 0, i % tps))
    keep = lambda: pl.BlockSpec((1, WIDTH, keep_rows), lambda i: (i // tps, 0, 0))
    out_shape = (
        [jax.ShapeDtypeStruct((batch, WIDTH, seq), BF16)] * 6
        + [jax.ShapeDtypeStruct((batch, WIDTH, seq), F32)] * 2
        + [jax.ShapeDtypeStruct((batch, N_HEADS, seq), F32)]
        + [jax.ShapeDtypeStruct((batch, WIDTH, keep_rows), F32)] * 2
    )
    out_specs = ([feat() for _ in range(8)]
                 + [pl.BlockSpec((1, N_HEADS, tm), lambda i: (i // tps, 0, i % tps)), keep(), keep()])
    return pl.pallas_call(
        functools.partial(_inproj_t_kernel, tiles_per_seq=tps),
        grid=(n_tok // tm,),
        in_specs=[
            pl.BlockSpec((tm, d_model), lambda i: (i, 0)),
            _resident((1, d_model)),
            w_spec,
            _resident((N_HEADS, 1)),
        ],
        out_specs=out_specs,
        out_shape=out_shape,
        scratch_shapes=[pltpu.VMEM((6 * WIDTH + 2 * N_HEADS, d_model), BF16)],
        compiler_params=_params(1),
        name="inproj",
    )(x, gain, w_in, b_col)


def _lane_cumsum(x):
    n = x.shape[-1]
    lane = lax.broadcasted_iota(jnp.int32, x.shape, x.ndim - 1)
    shift = 1
    while shift < n:
        x = x + jnp.where(lane >= shift, pltpu.roll(x, shift, axis=x.ndim - 1), 0.0)
        shift *= 2
    return x


def _split3(x):
    hi = x.astype(BF16).astype(F32)
    rest = x - hi
    mid = rest.astype(BF16).astype(F32)
    return hi, mid, rest - mid


def _cumsum_kernel(lf_ref, c2_ref, kf_ref):
    c2 = _lane_cumsum(lf_ref[...]) * LOG2E
    c2_ref[...] = c2
    seq = c2.shape[1]
    hi, mid, lo = _split3(c2)
    pad = jnp.zeros((PAIR - 3 * N_HEADS, seq), F32)
    for b in range(kf_ref.shape[0]):
        rows = slice(b * N_HEADS, (b + 1) * N_HEADS)
        feat = jnp.concatenate([hi[rows], mid[rows], lo[rows], pad], axis=0)
        kf_ref[b] = feat.T.astype(BF16)


def _cumsum(logf, *, batch, seq):
    return pl.pallas_call(
        _cumsum_kernel,
        out_shape=[
            jax.ShapeDtypeStruct((batch * N_HEADS, seq), F32),
            jax.ShapeDtypeStruct((batch, seq, PAIR), BF16),
        ],
        compiler_params=_params(),
        name="cumsum",
    )(logf)


def _split_heads_q(q):
    lane = lax.broadcasted_iota(jnp.int32, q.shape, 1)
    zero = jnp.zeros_like(q)
    return jnp.concatenate([jnp.where(lane < HEAD_DIM, q, zero),
                            jnp.where(lane >= HEAD_DIM, q, zero)], axis=0)


def _augment_v(v, h, axis):
    idx = lax.broadcasted_iota(jnp.int32, v.shape, axis)
    own = (idx < HEAD_DIM) if h == 0 else (idx >= HEAD_DIM)
    return jnp.where(own, v, jnp.ones_like(v))


def _merge_heads_out(acc0, acc1):
    lane = lax.broadcasted_iota(jnp.int32, acc0.shape, 1)
    first = lane < HEAD_DIM
    num = jnp.where(first, acc0, acc1)
    den = pltpu.roll(jnp.where(first, acc1, acc0), HEAD_DIM, axis=1)
    return num / den


def _pair_queries(qt):
    n = qt.shape[1]
    q2 = jnp.concatenate([qt, qt], axis=1)
    r = lax.broadcasted_iota(jnp.int32, q2.shape, 0)
    c = lax.broadcasted_iota(jnp.int32, q2.shape, 1)
    own = jnp.right_shift(r, HEAD_SHIFT) == jnp.where(c >= n, 1, 0)
    return jnp.where(own, q2, jnp.zeros_like(q2))


def _merge_heads_out_t(acc0, acc1):
    first = lax.broadcasted_iota(jnp.int32, acc0.shape, 0) < HEAD_DIM
    num = jnp.where(first, acc0, acc1)
    den = jnp.where(first, acc0[HEAD_DIM:HEAD_DIM + 1, :], acc1[0:1, :])
    return (num / den).T


def _weighted_values(vaug, p_blocks, key_ranges):
    accs = []
    for h in range(2):
        cols = slice(h * Q_TILE, (h + 1) * Q_TILE)
        acc = None
        for p, (k0, k1) in zip(p_blocks, key_ranges):
            part = _dot(_augment_v(vaug[0, :, k0:k1], h, 0), p[:, cols])
            acc = part if acc is None else acc + part
        accs.append(acc)
    return accs


def _fox_kernel(qt_ref, kt_ref, vt_ref, c2_ref, kf_ref, o_ref, v0_ref, v1_ref, kx_ref, *, seq):
    for pp in range(PAIRS_PER_STEP):
        feats = slice(pp * PAIR, (pp + 1) * PAIR)
        _fox_pair(PAIRS_PER_STEP * pl.program_id(1) + pp,
                  qt_ref.at[:, feats, :], kt_ref.at[:, feats, :], vt_ref.at[:, feats, :],
                  c2_ref.at[:, pp:pp + 1], kf_ref, o_ref.at[:, feats],
                  v0_ref.at[pp], v1_ref.at[pp], kx_ref.at[pp], seq=seq)


def _fox_pair(pair, qt_ref, kt_ref, vt_ref, c2_ref, kf_ref, o_ref, v0_ref, v1_ref, kx_ref, *, seq):
    vaug = vt_ref

    kx_ref[:, 0:PAIR] = kt_ref[0].astype(F32).T.astype(BF16)
    kx_ref[:, PAIR:2 * PAIR] = kf_ref[0]
    er = lax.broadcasted_iota(jnp.int32, (PAIR, 2 * Q_TILE), 0)
    ec = lax.broadcasted_iota(jnp.int32, (PAIR, 2 * Q_TILE), 1)
    head = 2 * pair + jnp.where(ec >= Q_TILE, 1, 0)
    piece = er - head
    coef = jnp.where((piece == 0) | (piece == N_HEADS) | (piece == 2 * N_HEADS), -1.0, 0.0).astype(BF16)
    key = lax.broadcasted_iota(jnp.int32, (Q_TILE, 2 * Q_TILE), 0)
    qry = jnp.bitwise_and(lax.broadcasted_iota(jnp.int32, (Q_TILE, 2 * Q_TILE), 1), Q_TILE - 1)
    causal = key <= qry

    n_qb = seq // Q_TILE

    def scores(qb):
        qs, ke = qb * Q_TILE, (qb + 1) * Q_TILE
        rhs = jnp.concatenate([_pair_queries(qt_ref[0, :, qs:ke]), coef], axis=0)
        return _dot(kx_ref[0:ke, :], rhs)

    def softmax(qb, s):
        qs, ke = qb * Q_TILE, (qb + 1) * Q_TILE
        cq = jnp.concatenate([c2_ref[0, 0, 0:1, qs:ke], c2_ref[0, 0, 1:2, qs:ke]], axis=1)
        t_diag = jnp.where(causal, s[qs:ke, :], -jnp.inf)
        m = jnp.max(t_diag, axis=0, keepdims=True)
        if qb > 0:
            t_past = s[0:qs, :]
            m = jnp.maximum(m, jnp.max(t_past, axis=0, keepdims=True))
        shift = cq - (m + cq)
        p_blocks = [jnp.exp2(t_diag + shift).astype(BF16)]
        key_ranges = [(qs, ke)]
        if qb > 0:
            p_blocks.append(jnp.exp2(t_past + shift).astype(BF16))
            key_ranges.append((0, qs))
        return p_blocks, key_ranges

    def finish(qb, p_blocks, key_ranges):
        acc0, acc1 = _weighted_values(vaug, p_blocks, key_ranges)
        o_ref[qb * Q_TILE:(qb + 1) * Q_TILE, :] = _merge_heads_out_t(acc0, acc1).astype(o_ref.dtype)

    order = list(range(n_qb - 1, -1, -1))
    s_next = scores(order[0])
    pending = None
    for i, qb in enumerate(order):
        s = s_next
        if i + 1 < n_qb:
            s_next = scores(order[i + 1])
        probs = softmax(qb, s)
        if pending is not None:
            finish(*pending)
        pending = (qb, *probs)
    finish(*pending)


def _fox(qt, kt, vt, c2, kf, *, batch, seq):
    n_tok = batch * seq
    pps = PAIRS_PER_STEP
    feat = lambda: pl.BlockSpec((1, pps * PAIR, seq), lambda b, p: (b, p, 0))
    return pl.pallas_call(
        functools.partial(_fox_kernel, seq=seq),
        grid=(batch, N_PAIRS // pps),
        in_specs=[
            feat(), feat(), feat(),
            pl.BlockSpec((1, pps, 2, seq), lambda b, p: (b, p, 0, 0)),
            pl.BlockSpec((1, seq, PAIR), lambda b, p: (b, 0, 0)),
        ],
        out_specs=pl.BlockSpec((seq, pps * PAIR), lambda b, p: (b, p)),
        out_shape=jax.ShapeDtypeStruct((n_tok, WIDTH), BF16),
        scratch_shapes=[pltpu.VMEM((pps, PAIR, seq), BF16), pltpu.VMEM((pps, PAIR, seq), BF16),
                        pltpu.VMEM((pps, seq, 2 * PAIR), BF16)],
        compiler_params=_params(),
        name="fox",
    )(qt, kt, vt, c2, kf)


def _bias_kernel(rb_ref, tabm_ref, tabr_ref):
    rb = rb_ref[...]
    n_rows = rb.shape[1]
    r = lax.broadcasted_iota(jnp.int32, (n_rows, TOEPLITZ), 0)
    m = lax.broadcasted_iota(jnp.int32, (n_rows, TOEPLITZ), 1)
    e = jnp.where(m < TOEPLITZ // 2, m, m - TOEPLITZ)
    idx = jnp.clip(LEFT_REACH - Q_TILE - e, -REL_CLIP, REL_CLIP) + REL_CLIP
    onehot = jnp.where(r == idx, 1.0, 0.0).astype(BF16)
    hi = rb.astype(BF16)
    rest = rb - hi.astype(F32)
    mid = rest.astype(BF16)
    lo = (rest - mid.astype(F32)).astype(BF16)
    u = _dot(hi, onehot) + _dot(mid, onehot) + _dot(lo, onehot)

    i = lax.broadcasted_iota(jnp.int32, (Q_TILE, BAND_KEYS), 0)
    j = lax.broadcasted_iota(jnp.int32, (Q_TILE, BAND_KEYS), 1)
    qc = jnp.right_shift(i, HEAD_SHIFT)
    kc = jnp.right_shift(j, HEAD_SHIFT)
    valid = (kc >= qc) & (kc <= qc + LEFT_CHUNKS)
    n_new = tabr_ref.shape[1]
    for h in range(N_HEADS):
        base = jnp.broadcast_to(u[h:h + 1, :], (Q_TILE, TOEPLITZ))
        t = pltpu.roll(base, Q_TILE, axis=1, stride=1, stride_axis=0)[:, 0:BAND_KEYS]
        half = (h % 2) * Q_TILE
        tabm_ref[h // 2, :, half:half + Q_TILE] = jnp.where(valid, t * LOG2E, MASKED).T
        tabr_ref[h] = t[0:n_new, :]


def _bias_tables(rb_pad, *, n_new):
    return pl.pallas_call(
        _bias_kernel,
        out_shape=[
            jax.ShapeDtypeStruct((N_PAIRS, BAND_KEYS, 2 * Q_TILE), F32),
            jax.ShapeDtypeStruct((N_HEADS, n_new, BAND_KEYS), F32),
        ],
        compiler_params=_params(),
        name="bias_tables",
    )(rb_pad)


def _band_kernel(qt_ref, kt_ref, vt_ref, tab_ref, o_ref, v0_ref, v1_ref, k_ref, *, seq):
    for pp in range(PAIRS_PER_STEP):
        feats = slice(pp * PAIR, (pp + 1) * PAIR)
        _band_pair(qt_ref.at[:, feats, :], kt_ref.at[:, feats, :], vt_ref.at[:, feats, :],
                   tab_ref.at[pp:pp + 1], o_ref.at[:, feats],
                   v0_ref.at[pp], v1_ref.at[pp], k_ref.at[pp], seq=seq)


def _band_pair(qt_ref, kt_ref, vt_ref, tab_ref, o_ref, v0_ref, v1_ref, k_ref, *, seq):
    vaug = vt_ref
    k_ref[...] = kt_ref[0].astype(F32).T.astype(BF16)

    n_qb = seq // Q_TILE

    def scores(qb):
        qs, ke = qb * Q_TILE, (qb + 1) * Q_TILE
        ks = max(0, qs - LEFT_REACH)
        return _dot(k_ref[ks:ke, :], _pair_queries(qt_ref[0, :, qs:ke]))

    def softmax(qb, s):
        n_keys = s.shape[0]
        t = s + tab_ref[0, BAND_KEYS - n_keys:BAND_KEYS, :]
        m = jnp.max(t, axis=0, keepdims=True)
        return jnp.exp2(t - m).astype(BF16)

    def finish(qb, p):
        ke = (qb + 1) * Q_TILE
        acc0, acc1 = _weighted_values(vaug, [p], [(ke - p.shape[0], ke)])
        o_ref[qb * Q_TILE:ke, :] = _merge_heads_out_t(acc0, acc1).astype(o_ref.dtype)

    order = list(range(n_qb - 1, -1, -1))
    s_next = scores(order[0])
    pending = None
    for i, qb in enumerate(order):
        s = s_next
        if i + 1 < n_qb:
            s_next = scores(order[i + 1])
        p = softmax(qb, s)
        if pending is not None:
            finish(*pending)
        pending = (qb, p)
    finish(*pending)


def _band(qt, kt, vt, tabm, *, batch, seq):
    n_tok = batch * seq
    pps = PAIRS_PER_STEP
    feat = lambda: pl.BlockSpec((1, pps * PAIR, seq), lambda p, b: (b, p, 0))
    return pl.pallas_call(
        functools.partial(_band_kernel, seq=seq),
        grid=(N_PAIRS // pps, batch),
        in_specs=[
            feat(), feat(), feat(),
            pl.BlockSpec((pps, BAND_KEYS, 2 * Q_TILE), lambda p, b: (p, 0, 0)),
        ],
        out_specs=pl.BlockSpec((seq, pps * PAIR), lambda p, b: (b, p)),
        out_shape=jax.ShapeDtypeStruct((n_tok, WIDTH), BF16),
        scratch_shapes=[pltpu.VMEM((pps, PAIR, seq), BF16), pltpu.VMEM((pps, PAIR, seq), BF16),
                        pltpu.VMEM((pps, seq, PAIR), BF16)],
        compiler_params=_params(),
        name="band",
    )(qt, kt, vt, tabm)


def _two_part_attention(q2, kt_cache, vt_cache, kn_ref, vn_ref, bias_cache, bias_new, n_q):
    s_c = _dot(q2, kt_cache)
    s_n = _dot_nt(q2, kn_ref[...])
    accs = []
    for h in range(2):
        rows = slice(h * n_q, (h + 1) * n_q)
        t_c = s_c[rows, :] + bias_cache(h)
        t_n = s_n[rows, :] + bias_new(h)
        m = jnp.maximum(jnp.max(t_c, axis=1, keepdims=True), jnp.max(t_n, axis=1, keepdims=True))
        acc = _dot_nt(jnp.exp(t_c - m).astype(BF16), _augment_v(vt_cache, h, 0))
        acc = acc + _dot(jnp.exp(t_n - m).astype(BF16), _augment_v(vn_ref[...], h, 1))
        accs.append(acc)
    return _merge_heads_out(accs[0], accs[1])


def _sample_attn_kernel(qa_ref, ka_ref, va_ref, qb_ref, kb_ref, vb_ref,
                        cak_ref, cav_ref, cbk_ref, cbv_ref, c_ref, tab_ref,
                        oa_ref, ob_ref, kn_ref, vn_ref, *, n_q):
    row = lax.broadcasted_iota(jnp.int32, (n_q, NEW_PAD), 0)
    col = lax.broadcasted_iota(jnp.int32, (n_q, NEW_PAD), 1)
    n_cache = cak_ref.shape[2]
    n_band = cbk_ref.shape[2]

    def stage_new(slot, k_new, v_new):
        kn_ref[slot] = jnp.zeros((NEW_PAD, PAIR), BF16)
        vn_ref[slot] = jnp.zeros((NEW_PAD, PAIR), BF16)
        kn_ref[slot, 0:n_q, :] = k_new
        vn_ref[slot, 0:n_q, :] = v_new
        return kn_ref.at[slot], vn_ref.at[slot]

    for p in range(N_PAIRS):
        feats = slice(p * PAIR, (p + 1) * PAIR)

        c_cache = c_ref[0, p, :, 0:n_cache]
        c_new = c_ref[0, p, :, n_cache:n_cache + NEW_PAD]

        def fox_cache(h, c_cache=c_cache, c_new=c_new):
            cq = jnp.sum(jnp.where(col == row, c_new[h:h + 1, :], 0.0), axis=1, keepdims=True)
            return cq - c_cache[h:h + 1, :]

        def fox_new(h, c_new=c_new):
            cq = jnp.sum(jnp.where(col == row, c_new[h:h + 1, :], 0.0), axis=1, keepdims=True)
            return jnp.where(col <= row, cq - c_new[h:h + 1, :], -jnp.inf)

        kn, vn = stage_new(2 * p, ka_ref[:, feats], va_ref[:, feats])
        oa = _two_part_attention(_split_heads_q(qa_ref[:, feats]), cak_ref[0, feats, :].astype(BF16),
                                 cav_ref[0, feats, :].astype(BF16), kn, vn, fox_cache, fox_new, n_q)
        oa_ref[:, feats] = oa.astype(oa_ref.dtype)

        def band_cache(h, p=p):
            return tab_ref[2 * p + h, :, 0:n_band]

        def band_new(h, p=p):
            return jnp.where(col < n_q, tab_ref[2 * p + h, :, n_band:n_band + NEW_PAD], -jnp.inf)

        kn, vn = stage_new(2 * p + 1, kb_ref[:, feats], vb_ref[:, feats])
        ob = _two_part_attention(_split_heads_q(qb_ref[:, feats]), cbk_ref[0, feats, :].astype(BF16),
                                 cbv_ref[0, feats, :].astype(BF16), kn, vn, band_cache, band_new, n_q)
        ob_ref[:, feats] = ob.astype(ob_ref.dtype)


def _rows_cumsum_kernel(x_ref, c_ref):
    c_ref[...] = _lane_cumsum(x_ref[...])


def _rows_cumsum(x):
    return pl.pallas_call(
        _rows_cumsum_kernel,
        out_shape=jax.ShapeDtypeStruct(x.shape, F32),
        compiler_params=_params(),
        name="rows_cumsum",
    )(x)


def _sample_attn(qa, ka, va, qb, kb, vb, cakt, cavt, cbkt, cbvt, c_all, tabr, *, batch, n_q):
    n_tok = batch * n_q
    past = cakt.shape[2]
    n_band = cbkt.shape[2]
    new = lambda: pl.BlockSpec((n_q, WIDTH), lambda b: (b, 0))
    cache = lambda n: pl.BlockSpec((1, WIDTH, n), lambda b: (b, 0, 0))
    return pl.pallas_call(
        functools.partial(_sample_attn_kernel, n_q=n_q),
        grid=(batch,),
        in_specs=[
            new(), new(), new(), new(), new(), new(),
            cache(past), cache(past), cache(n_band), cache(n_band),
            pl.BlockSpec((1, N_PAIRS, 2, past + NEW_PAD), lambda b: (b, 0, 0, 0)),
            _resident(tabr.shape),
        ],
        out_specs=[new(), new()],
        out_shape=[jax.ShapeDtypeStruct((n_tok, WIDTH), BF16)] * 2,
        scratch_shapes=[pltpu.VMEM((2 * N_PAIRS, NEW_PAD, PAIR), BF16),
                        pltpu.VMEM((2 * N_PAIRS, NEW_PAD, PAIR), BF16)],
        compiler_params=_params(),
        name="sample_attn",
    )(qa, ka, va, qb, kb, vb, cakt, cavt, cbkt, cbvt, c_all, tabr)


def _post_kernel(x_ref, oa_ref, ob_ref, ga_ref, gb_ref, wo_ref, gpost_ref, gpre_ref,
                 wg_ref, wu_ref, wd_ref, gffn_ref, y_ref, *, n_sub):
    sub = x_ref.shape[0] // n_sub
    rows = [slice(i * sub, (i + 1) * sub) for i in range(n_sub)]
    st = [dict() for _ in range(n_sub)]

    def norm(i):
        st[i]["na"] = _rms(oa_ref[rows[i], :].astype(F32), ga_ref[...]).astype(BF16)
        st[i]["nb"] = _rms(ob_ref[rows[i], :].astype(F32), gb_ref[...]).astype(BF16)

    def mix(i):
        st[i]["mix"] = _dot(st[i]["na"], wo_ref[0:WIDTH, :]) + _dot(st[i]["nb"], wo_ref[WIDTH:2 * WIDTH, :])

    def mid(i):
        x1 = x_ref[rows[i], :] + _rms(st[i]["mix"], gpost_ref[...])
        st[i]["x1"] = x1
        st[i]["hf"] = _rms(x1, gpre_ref[...]).astype(BF16)

    def gate_up(i):
        st[i]["g"] = _dot(st[i]["hf"], wg_ref[...])
        st[i]["u"] = _dot(st[i]["hf"], wu_ref[...])

    def silu(i):
        g = st[i]["g"]
        st[i]["a"] = (g * jax.nn.sigmoid(g) * st[i]["u"]).astype(BF16)

    def down(i):
        st[i]["ff"] = _dot(st[i]["a"], wd_ref[...])

    def fin(i):
        y_ref[rows[i], :] = st[i]["x1"] + _rms(st[i]["ff"], gffn_ref[...])

    vpu_stages = [norm, mid, silu, fin]
    mxu_stages = [mix, gate_up, down]
    norm(0)
    for k, mxu in enumerate(mxu_stages):
        for i in range(n_sub):
            mxu(i)
            if i + 1 < n_sub:
                vpu_stages[k](i + 1)
            else:
                vpu_stages[k + 1](0)
    for i in range(1, n_sub):
        fin(i)


def _post(x, oa, ob, ga, gb, wo, gpost, gpre, wg, wu, wd, gffn, *, tm, n_sub):
    n_tok, d_model = x.shape
    row = lambda w: pl.BlockSpec((tm, w), lambda i: (i, 0))
    return pl.pallas_call(
        functools.partial(_post_kernel, n_sub=n_sub),
        grid=(n_tok // tm,),
        in_specs=[
            row(d_model), row(WIDTH), row(WIDTH),
            _resident(ga.shape), _resident(gb.shape), _resident(wo.shape),
            _resident(gpost.shape), _resident(gpre.shape),
            _resident(wg.shape), _resident(wu.shape), _resident(wd.shape), _resident(gffn.shape),
        ],
        out_specs=row(d_model),
        out_shape=jax.ShapeDtypeStruct((n_tok, d_model), F32),
        compiler_params=_params(),
        name="post",
    )(x, oa, ob, ga, gb, wo, gpost, gpre, wg, wu, wd, gffn)


def _row(v):
    return v.reshape(1, -1).astype(F32)


def _state_from_feature_major(t, batch, n):
    return t.reshape(batch, N_HEADS, HEAD_DIM, n).transpose(0, 3, 1, 2)


def _cache_feature_major(c):
    b, n = c.shape[0], c.shape[1]
    return c.transpose(0, 2, 3, 1).reshape(b, WIDTH, n)


def kernel(x_prompt, x_sample, cache_a_k, cache_a_v, cache_a_logf, cache_b_k, cache_b_v, norm_mix_pre, w_in, b_forget, rel_bias, gain_out_a, gain_out_b, w_out, norm_mix_post, norm_ffn_pre, w_gate, w_up, w_down, norm_ffn_post):
    batch, seq, d_model = x_prompt.shape
    dec_batch, dec_seq, _ = x_sample.shape
    depth = w_in.shape[0]
    past = cache_a_k.shape[2]
    n_band = cache_b_k.shape[2]
    keep = min(LEFT_REACH, seq)
    assert seq % Q_TILE == 0 and keep == LEFT_REACH and n_band == LEFT_REACH and dec_seq <= CHUNK
    n_tok_p, n_tok_s = batch * seq, dec_batch * dec_seq
    tm_p = 2 * keep
    assert seq % tm_p == 0

    xp = x_prompt.reshape(n_tok_p, d_model)
    xs = x_sample.reshape(n_tok_s, d_model)
    outs = [[] for _ in range(10)]
    w_in_t = jnp.swapaxes(w_in, 1, 2)
    for l in range(depth):
        b_col = b_forget[l].reshape(N_HEADS, 1).astype(F32)
        rb_pad = jnp.pad(rel_bias[l].astype(F32),
                         ((0, RB_ROWS - N_HEADS), (0, RB_COLS - (2 * REL_CLIP + 1))))
        tail = (_row(gain_out_a[l]), _row(gain_out_b[l]), w_out[l].astype(BF16), _row(norm_mix_post[l]),
                _row(norm_ffn_pre[l]), w_gate[l].astype(BF16), w_up[l].astype(BF16), w_down[l].astype(BF16),
                _row(norm_ffn_post[l]))
        tabm, tabr = _bias_tables(rb_pad, n_new=dec_seq)

        qat, qbt, kat, vat, kbt, vbt, kat32, vat32, logf_p, kbt32, vbt32 = _inproj_t(
            xp, _row(norm_mix_pre[l]), w_in_t, l, b_col, batch=batch, seq=seq, tm=tm_p, keep_rows=keep)
        c2, kf = _cumsum(logf_p.reshape(batch * N_HEADS, seq), batch=batch, seq=seq)
        oa = _fox(qat, kat, vat, c2.reshape(batch, N_PAIRS, 2, seq), kf, batch=batch, seq=seq)
        ob = _band(qbt, kbt, vbt, tabm, batch=batch, seq=seq)
        xp = _post(xp, oa, ob, *tail, tm=512, n_sub=2)
        outs[0].append(_state_from_feature_major(kat32, batch, seq))
        outs[1].append(_state_from_feature_major(vat32, batch, seq))
        outs[2].append(logf_p.transpose(0, 2, 1))
        outs[3].append(_state_from_feature_major(kbt32, batch, keep))
        outs[4].append(_state_from_feature_major(vbt32, batch, keep))

        qa, ka, va, qb, kb, vb, ka32, va32, logf_t, kb32, vb32 = _inproj(xs, _row(norm_mix_pre[l]), w_in_t, l, b_col)
        lf_all = jnp.concatenate(
            [cache_a_logf[l].astype(F32).transpose(0, 2, 1),
             logf_t.reshape(N_HEADS, dec_batch, dec_seq).transpose(1, 0, 2),
             jnp.zeros((dec_batch, N_HEADS, NEW_PAD - dec_seq), F32)], axis=2)
        c_all = _rows_cumsum(lf_all.reshape(dec_batch * N_HEADS, past + NEW_PAD))
        oa, ob = _sample_attn(
            qa, ka, va, qb, kb, vb,
            _cache_feature_major(cache_a_k[l]), _cache_feature_major(cache_a_v[l]),
            _cache_feature_major(cache_b_k[l]), _cache_feature_major(cache_b_v[l]),
            c_all.reshape(dec_batch, N_PAIRS, 2, past + NEW_PAD), tabr, batch=dec_batch, n_q=dec_seq)
        xs = _post(xs, oa, ob, *tail, tm=n_tok_s, n_sub=1)
        outs[5].append(ka32.reshape(dec_batch, dec_seq, N_HEADS, HEAD_DIM))
        outs[6].append(va32.reshape(dec_batch, dec_seq, N_HEADS, HEAD_DIM))
        outs[7].append(logf_t.T.reshape(dec_batch, dec_seq, N_HEADS))
        outs[8].append(kb32.reshape(dec_batch, dec_seq, N_HEADS, HEAD_DIM))
        outs[9].append(vb32.reshape(dec_batch, dec_seq, N_HEADS, HEAD_DIM))

    st = jnp.stack
    return (xp.reshape(batch, seq, d_model), xs.reshape(dec_batch, dec_seq, d_model),
            *[st(o) for o in outs])
```
